```python
import jax, jax.numpy as jnp
from jax import lax
import numpy as np

D_MODEL = 1024
BATCH = 4
SEQ = 4096
DEPTH = 4

HEAD_DIM = 128
MOBA_HEADS = 4
RET_HEADS = 4
MOBA_BLOCK = 256
MOBA_TOPK = 3
MOBA_QCHUNK = 32
RET_CHUNK = 128
CONV_WIDTH = 3
PEER_HEADS = 8
PEER_NKEYS = 128
PEER_NEXPERTS = PEER_NKEYS * PEER_NKEYS
PEER_TOPK = 16
PEER_DKEY = 256
PEER_TOKCHUNK = 128
ROPE_THETA = 10000.0
EPS = 1e-6
N_EVEN = (DEPTH + 1) // 2
N_ODD = DEPTH // 2
MOBA_W = MOBA_HEADS * HEAD_DIM
RET_W = RET_HEADS * HEAD_DIM
EVEN_SPLITS = (MOBA_W, MOBA_W, MOBA_W, RET_W, RET_W, RET_W, RET_W)
EVEN_IN = sum(EVEN_SPLITS)
EVEN_OUT_IN = MOBA_W + RET_W

kernel_name = "moba_retention_shortconv_peer_hybrid"


def rmsnorm(x, g):
    xf = x.astype(jnp.float32)
    y = xf * lax.rsqrt(jnp.mean(xf * xf, axis=-1, keepdims=True) + EPS)
    return (y * g.astype(jnp.float32)).astype(x.dtype)


def rope(t):
    s, dh = t.shape[2], t.shape[3]
    half = dh // 2
    inv = ROPE_THETA ** (-jnp.arange(half, dtype=jnp.float32) / half)
    ang = jnp.arange(s, dtype=jnp.float32)[:, None] * inv[None, :]
    cos = jnp.cos(ang).astype(t.dtype)
    sin = jnp.sin(ang).astype(t.dtype)
    t1, t2 = t[..., :half], t[..., half:]
    return jnp.concatenate([t1 * cos - t2 * sin, t2 * cos + t1 * sin], axis=-1)


def split_heads(t, n):
    b, s, _ = t.shape
    return t.reshape(b, s, n, HEAD_DIM).transpose(0, 2, 1, 3)


def merge_heads(t):
    b, h, s, dh = t.shape
    return t.transpose(0, 2, 1, 3).reshape(b, s, h * dh)


def moba_attention(q, k, v):
    b, h, s, dh = q.shape
    nb = -(-s // MOBA_BLOCK)
    pad = nb * MOBA_BLOCK - s
    kb = jnp.pad(k, ((0, 0), (0, 0), (0, pad), (0, 0))).reshape(b, h, nb, MOBA_BLOCK, dh)
    vb = jnp.pad(v, ((0, 0), (0, 0), (0, pad), (0, 0))).reshape(b, h, nb, MOBA_BLOCK, dh)
    counts = jnp.minimum(MOBA_BLOCK, s - jnp.arange(nb) * MOBA_BLOCK).astype(jnp.float32)
    kmean = kb.astype(jnp.float32).sum(axis=3) / counts[None, None, :, None]
    gate = jnp.einsum('bhsd,bhnd->bhsn', q.astype(jnp.float32), kmean)
    qblk = jnp.arange(s) // MOBA_BLOCK
    past = jnp.arange(nb)[None, :] < qblk[:, None]
    gate = jnp.where(past, gate, -jnp.inf)
    kk = min(MOBA_TOPK, nb)
    _, sel = lax.top_k(gate, kk)
    valid = sel < qblk[:, None]
    nq = s // MOBA_QCHUNK
    def chunked(t):
        return t.reshape(b, h, nq, MOBA_QCHUNK, *t.shape[3:]).transpose(2, 0, 1, 3, *range(4, t.ndim + 1))
    qs, sels, valids = chunked(q), chunked(sel), chunked(valid)
    scale = dh ** -0.5
    bi = jnp.arange(b)[:, None, None, None]
    hi = jnp.arange(h)[None, :, None, None]

    def step(args):
        ci, qc, selc, validc = args
        tpos = ci * MOBA_QCHUNK + jnp.arange(MOBA_QCHUNK)
        ob = (ci * MOBA_QCHUNK) // MOBA_BLOCK
        k_own = lax.dynamic_index_in_dim(kb, ob, axis=2, keepdims=False)
        v_own = lax.dynamic_index_in_dim(vb, ob, axis=2, keepdims=False)
        kpos = ob * MOBA_BLOCK + jnp.arange(MOBA_BLOCK)
        s_own = jnp.einsum('bhqd,bhkd->bhqk', qc, k_own).astype(jnp.float32) * scale
        s_own = jnp.where(kpos[None, :] <= tpos[:, None], s_own, -jnp.inf)
        kg = kb[bi, hi, selc]
        vg = vb[bi, hi, selc]
        s_sel = jnp.einsum('bhqd,bhqjkd->bhqjk', qc, kg).astype(jnp.float32) * scale
        s_sel = jnp.where(validc[..., None], s_sel, -jnp.inf)
        scores = jnp.concatenate([s_sel.reshape(b, h, MOBA_QCHUNK, kk * MOBA_BLOCK), s_own], axis=-1)
        p = jax.nn.softmax(scores, axis=-1).astype(v.dtype)
        p_sel = p[..., :kk * MOBA_BLOCK].reshape(b, h, MOBA_QCHUNK, kk, MOBA_BLOCK)
        p_own = p[..., kk * MOBA_BLOCK:]
        return (jnp.einsum('bhqjk,bhqjkd->bhqd', p_sel, vg)
                + jnp.einsum('bhqk,bhkd->bhqd', p_own, v_own))

    out = lax.map(step, (jnp.arange(nq), qs, sels, valids))
    return out.transpose(1, 2, 0, 3, 4).reshape(b, h, s, dh)


def retention(q, k, v):
    b, h, s, dh = q.shape
    c = RET_CHUNK
    nc = s // c
    log_g = jnp.log(1.0 - 2.0 ** (-5.0 - jnp.arange(h, dtype=jnp.float32)))
    idx = jnp.arange(c, dtype=jnp.float32)
    diff = idx[:, None] - idx[None, :]
    dmask = jnp.where(diff >= 0, jnp.exp(log_g[:, None, None] * jnp.maximum(diff, 0.0)), 0.0)
    xi = jnp.exp(log_g[:, None] * (idx + 1.0))[..., None]
    zeta = jnp.exp(log_g[:, None] * (c - 1.0 - idx))[..., None]
    gc = jnp.exp(log_g * c)[:, None, None]
    def chunked(t):
        return t.astype(jnp.float32).reshape(b, h, nc, c, dh).transpose(2, 0, 1, 3, 4)
    qs, ks, vs = chunked(q), chunked(k * (dh ** -0.5)), chunked(v)

    def step(state, inp):
        qc, kc, vc = inp
        inner = jnp.einsum('bhnd,bhmd->bhnm', qc, kc) * dmask
        o = jnp.einsum('bhnm,bhmd->bhnd', inner, vc) + jnp.einsum('bhnd,bhde->bhne', qc, state) * xi
        state = state * gc + jnp.einsum('bhmd,bhme->bhde', kc * zeta, vc)
        return state, o

    state0 = jnp.zeros((b, h, dh, dh), jnp.float32)
    _, outs = lax.scan(step, state0, (qs, ks, vs))
    return outs.transpose(1, 2, 0, 3, 4).reshape(b, h, s, dh)


def attn_retention_mixer(xn, w_in, w_out):
    proj = xn @ w_in
    cuts = list(np.cumsum(EVEN_SPLITS)[:-1])
    mq, mk, mv, rq, rk, rv, rg = jnp.split(proj, cuts, axis=-1)
    mq, mk, mv = split_heads(mq, MOBA_HEADS), split_heads(mk, MOBA_HEADS), split_heads(mv, MOBA_HEADS)
    rq, rk, rv = split_heads(rq, RET_HEADS), split_heads(rk, RET_HEADS), split_heads(rv, RET_HEADS)
    mo = merge_heads(moba_attention(rope(mq), rope(mk), mv))
    ro = retention(rope(rq), rope(rk), rv)
    ro = ro * lax.rsqrt(jnp.mean(ro * ro, axis=-1, keepdims=True) + EPS)
    ro = merge_heads(ro).astype(xn.dtype) * jax.nn.silu(rg)
    return jnp.concatenate([mo, ro], axis=-1) @ w_out


def short_conv_mixer(xn, w_in, conv_w, w_out):
    d = xn.shape[-1]
    bg, cg, hx = jnp.split(xn @ w_in, 3, axis=-1)
    u = cg * hx
    y = lax.conv_general_dilated(u, conv_w[:, None, :], window_strides=(1,),
                                 padding=[(CONV_WIDTH - 1, 0)],
                                 dimension_numbers=('NWC', 'WIO', 'NWC'),
                                 feature_group_count=d)
    return (bg * y) @ w_out


def peer(xn, w_q, sub_keys, u, v):
    b, s, d = xn.shape
    t = xn.reshape(b * s, d)
    n_tok = b * s
    q = (t @ w_q).reshape(n_tok, PEER_HEADS, 2, PEER_DKEY // 2)
    sc = jnp.einsum('thpd,hpnd->thpn', q, sub_keys).astype(jnp.float32)
    s1, i1 = lax.top_k(sc[:, :, 0], PEER_TOPK)
    s2, i2 = lax.top_k(sc[:, :, 1], PEER_TOPK)
    cand = (s1[..., :, None] + s2[..., None, :]).reshape(n_tok, PEER_HEADS, PEER_TOPK * PEER_TOPK)
    cidx = (i1[..., :, None] * PEER_NKEYS + i2[..., None, :]).reshape(n_tok, PEER_HEADS, PEER_TOPK * PEER_TOPK)
    top_s, pos = lax.top_k(cand, PEER_TOPK)
    eidx = jnp.take_along_axis(cidx, pos, axis=-1)
    gate = jax.nn.softmax(top_s, axis=-1).astype(xn.dtype)
    nt = n_tok // PEER_TOKCHUNK

    def step(args):
        tc, ec, gc = args
        hid = jax.nn.gelu(jnp.einsum('td,thkd->thk', tc, u[ec]), approximate=False)
        return jnp.einsum('thk,thkd->td', gc * hid, v[ec])

    out = lax.map(step, (t.reshape(nt, PEER_TOKCHUNK, d),
                         eidx.reshape(nt, PEER_TOKCHUNK, PEER_HEADS, PEER_TOPK),
                         gate.reshape(nt, PEER_TOKCHUNK, PEER_HEADS, PEER_TOPK)))
    return out.reshape(b, s, d)


def setup_inputs(seed: int = 0) -> dict:
    key = jax.random.key(seed)
    ks = jax.random.split(key, 16)
    f32 = jnp.float32
    nrm = lambda k, shape, sc: jax.random.normal(k, shape, f32) * sc
    return {
        "x": nrm(ks[0], (BATCH, SEQ, D_MODEL), 1.0),
        "norm_mix": 1.0 + nrm(ks[1], (DEPTH, D_MODEL), 0.02),
        "norm_ffn": 1.0 + nrm(ks[2], (DEPTH, D_MODEL), 0.02),
        "even_w_in": nrm(ks[3], (N_EVEN, D_MODEL, EVEN_IN), D_MODEL ** -0.5),
        "even_w_out": nrm(ks[4], (N_EVEN, EVEN_OUT_IN, D_MODEL), EVEN_OUT_IN ** -0.5),
        "odd_w_in": nrm(ks[5], (N_ODD, D_MODEL, 3 * D_MODEL), D_MODEL ** -0.5),
        "odd_conv": nrm(ks[6], (N_ODD, CONV_WIDTH, D_MODEL), CONV_WIDTH ** -0.5),
        "odd_w_out": nrm(ks[7], (N_ODD, D_MODEL, D_MODEL), D_MODEL ** -0.5),
        "peer_w_q": nrm(ks[8], (DEPTH, D_MODEL, PEER_HEADS * PEER_DKEY), D_MODEL ** -0.5),
        "peer_sub_keys": nrm(ks[9], (DEPTH, PEER_HEADS, 2, PEER_NKEYS, PEER_DKEY // 2), (PEER_DKEY // 2) ** -0.5),
        "peer_u": nrm(ks[10], (DEPTH, PEER_NEXPERTS, D_MODEL), D_MODEL ** -0.5),
        "peer_v": nrm(ks[11], (DEPTH, PEER_NEXPERTS, D_MODEL), 0.1),
        "final_norm": 1.0 + nrm(ks[12], (D_MODEL,), 0.02),
    }


def reference(x, norm_mix, norm_ffn, even_w_in, even_w_out, odd_w_in, odd_conv, odd_w_out,
              peer_w_q, peer_sub_keys, peer_u, peer_v, final_norm):
    h = x
    for layer in range(DEPTH):
        xn = rmsnorm(h, norm_mix[layer])
        i = layer // 2
        if layer % 2 == 0:
            h = h + attn_retention_mixer(xn, even_w_in[i], even_w_out[i])
        else:
            h = h + short_conv_mixer(xn, odd_w_in[i], odd_conv[i], odd_w_out[i])
        h = h + peer(rmsnorm(h, norm_ffn[layer]), peer_w_q[layer], peer_sub_keys[layer],
                     peer_u[layer], peer_v[layer])
    return rmsnorm(h, final_norm)
```

```python
import functools

import jax
import jax.numpy as jnp
import numpy as np
from jax import lax
from jax.experimental import pallas as pl
from jax.experimental.pallas import tpu as pltpu

HEAD_DIM = 128
MOBA_HEADS = 4
RET_HEADS = 4
MOBA_BLOCK = 256
MOBA_TOPK = 3
RET_CHUNK = 128
PEER_HEADS = 8
PEER_NKEYS = 128
PEER_TOPK = 16
PEER_HALF = 128
ROPE_THETA = 10000.0
EPS = 1e-6

V7X_LANES = 128
V7X_SUBLANES = 8
V7X_VMEM_BYTES = 64 * 2**20
VMEM_LIMIT = V7X_VMEM_BYTES - 8 * 2**20

MXU_DTYPE = jnp.bfloat16
F32 = jnp.float32
NEG_INF = float("-inf")
NT_DIMS = (((1,), (1,)), ((), ()))
TN_DIMS = (((0,), (0,)), ((), ()))

PROJ_TM = 256
OUT_TM = 512
CONV_TM = 256
PEER_Q_TM = 256
PEER_TOPK_TL = 256
PEER_TM = 512
PEER_ECHUNK = 2 * PEER_NKEYS
W_PITCH = 72


def _params(n_axes):
    return pltpu.CompilerParams(dimension_semantics=("arbitrary",) * n_axes, vmem_limit_bytes=VMEM_LIMIT)


def _mxu(x):
    return x.astype(MXU_DTYPE)


def _rms(x, g):
    return x * lax.rsqrt(jnp.mean(x * x, axis=-1, keepdims=True) + EPS) * g


def _norm_proj_kernel(h_ref, g_ref, w_ref, cos_ref, sin_ref, o_ref, *, rope_blocks, scale_blocks, scale, chunk):
    xn = _mxu(_rms(h_ref[...], g_ref[...]))
    n = o_ref.shape[1]
    for c0 in range(0, n, chunk):
        y = jnp.dot(xn, w_ref[:, c0:c0 + chunk], preferred_element_type=F32)
        for u in range(chunk // HEAD_DIM):
            blk = (c0 // HEAD_DIM) + u
            t = y[:, u * HEAD_DIM:(u + 1) * HEAD_DIM]
            if blk in rope_blocks:
                t = t * cos_ref[...] + pltpu.roll(t, HEAD_DIM // 2, axis=1) * sin_ref[...]
            if blk in scale_blocks:
                t = t * scale
            o_ref[:, blk * HEAD_DIM:(blk + 1) * HEAD_DIM] = t


def norm_proj(h, g, w, cos_t, sin_t, seq, rope_blocks=(), scale_blocks=(), scale=1.0):
    t, d = h.shape
    n = w.shape[1]
    tm = PROJ_TM
    per_seq = seq // tm
    return pl.pallas_call(
        functools.partial(_norm_proj_kernel, rope_blocks=frozenset(rope_blocks), scale_blocks=frozenset(scale_blocks),
                          scale=scale, chunk=512),
        out_shape=jax.ShapeDtypeStruct((t, n), F32),
        grid=(t // tm,),
        in_specs=[
            pl.BlockSpec((tm, d), lambda i: (i, 0)),
            pl.BlockSpec((1, d), lambda i: (0, 0)),
            pl.BlockSpec((d, n), lambda i: (0, 0)),
            pl.BlockSpec((tm, HEAD_DIM), lambda i: (i % per_seq, 0)),
            pl.BlockSpec((tm, HEAD_DIM), lambda i: (i % per_seq, 0)),
        ],
        out_specs=pl.BlockSpec((tm, n), lambda i: (i, 0)),
        compiler_params=_params(1),
        name="norm_proj",
    )(h, g.reshape(1, d), w, cos_t, sin_t)


def _moba_kernel(q_ref, k_ref, v_ref, o_ref, kb_ref, vb_ref, km_ref, *, nb, nbp):
    i = pl.program_id(2)
    blk = MOBA_BLOCK

    @pl.when(i == 0)
    def _():
        kb_ref[...] = _mxu(k_ref[...])
        vb_ref[...] = _mxu(v_ref[...])
        km_ref[...] = jnp.zeros_like(km_ref)
        for j in range(nb):
            km_ref[j:j + 1, :] = jnp.sum(k_ref[j * blk:(j + 1) * blk, :], axis=0, keepdims=True) / float(blk)

    q = q_ref[...]
    qb = _mxu(q)
    gate = lax.dot_general(q, km_ref[...], NT_DIMS, precision=lax.Precision.HIGHEST, preferred_element_type=F32)
    lane = lax.broadcasted_iota(jnp.int32, (blk, nbp), 1)
    rank = jnp.zeros((blk, nbp), jnp.int32)
    for jp in range(nb):
        col = gate[:, jp:jp + 1]
        beats = jnp.where(col > gate, 1, jnp.where(col == gate, jnp.where(jp < lane, 1, 0), 0))
        rank = rank + jnp.where(jp < i, beats, 0)
    sel = jnp.where(lane < i, jnp.where(rank < MOBA_TOPK, 1.0, 0.0), 0.0)

    scale = HEAD_DIM ** -0.5
    row = lax.broadcasted_iota(jnp.int32, (blk, blk), 0)
    colk = lax.broadcasted_iota(jnp.int32, (blk, blk), 1)

    start = pl.multiple_of(i * blk, blk)
    s = lax.dot_general(qb, kb_ref[pl.ds(start, blk), :], NT_DIMS, preferred_element_type=F32) * scale
    s = jnp.where(colk <= row, s, NEG_INF)
    m = jnp.max(s, axis=1, keepdims=True)
    p = jnp.exp(s - m)
    l = jnp.sum(p, axis=1, keepdims=True)
    acc = jnp.dot(_mxu(p), vb_ref[pl.ds(start, blk), :], preferred_element_type=F32)

    def body(j, carry):
        m, l, acc = carry
        st = pl.multiple_of(j * blk, blk)
        s = lax.dot_general(qb, kb_ref[pl.ds(st, blk), :], NT_DIMS, preferred_element_type=F32) * scale
        selcol = jnp.sum(jnp.where(lane == j, sel, 0.0), axis=1, keepdims=True)
        s = jnp.where(selcol > 0.0, s, NEG_INF)
        m_new = jnp.maximum(m, jnp.max(s, axis=1, keepdims=True))
        alpha = jnp.exp(m - m_new)
        p = jnp.exp(s - m_new)
        l = alpha * l + jnp.sum(p, axis=1, keepdims=True)
        acc = alpha * acc + jnp.dot(_mxu(p), vb_ref[pl.ds(st, blk), :], preferred_element_type=F32)
        return m_new, l, acc

    m, l, acc = lax.fori_loop(0, i, body, (m, l, acc))
    o_ref[...] = acc / l


def moba(proj, batch, seq):
    t = proj.shape[0]
    nb = seq // MOBA_BLOCK
    nbp = max(nb, V7X_SUBLANES)
    hd = HEAD_DIM
    return pl.pallas_call(
        functools.partial(_moba_kernel, nb=nb, nbp=nbp),
        out_shape=jax.ShapeDtypeStruct((t, MOBA_HEADS * hd), F32),
        grid=(batch, MOBA_HEADS, nb),
        in_specs=[
            pl.BlockSpec((MOBA_BLOCK, hd), lambda b, h, i: (b * nb + i, h)),
            pl.BlockSpec((seq, hd), lambda b, h, i: (b, MOBA_HEADS + h)),
            pl.BlockSpec((seq, hd), lambda b, h, i: (b, 2 * MOBA_HEADS + h)),
        ],
        out_specs=pl.BlockSpec((MOBA_BLOCK, hd), lambda b, h, i: (b * nb + i, h)),
        scratch_shapes=[
            pltpu.VMEM((seq, hd), MXU_DTYPE),
            pltpu.VMEM((seq, hd), MXU_DTYPE),
            pltpu.VMEM((nbp, hd), F32),
        ],
        compiler_params=_params(3),
        name="moba",
    )(proj, proj, proj)


def _retention_kernel(q_ref, k_ref, v_ref, g_ref, dm_ref, xi_ref, ze_ref, gc_ref, o_ref, st_ref):
    c = pl.program_id(1)

    @pl.when(c == 0)
    def _():
        st_ref[...] = jnp.zeros_like(st_ref)

    for h in range(RET_HEADS):
        sl = slice(h * HEAD_DIM, (h + 1) * HEAD_DIM)
        q = _mxu(q_ref[:, sl])
        k = k_ref[:, sl]
        v = _mxu(v_ref[:, sl])
        st = st_ref[h]
        inner = lax.dot_general(q, _mxu(k), NT_DIMS, preferred_element_type=F32) * dm_ref[h]
        o = (jnp.dot(_mxu(inner), v, preferred_element_type=F32)
             + jnp.dot(q, _mxu(st), preferred_element_type=F32) * xi_ref[h])
        st_ref[h] = st * gc_ref[h] + lax.dot_general(_mxu(k * ze_ref[h]), v, TN_DIMS, preferred_element_type=F32)
        o = o * lax.rsqrt(jnp.mean(o * o, axis=-1, keepdims=True) + EPS)
        gate = g_ref[:, sl]
        o_ref[:, sl] = o * (gate * jax.nn.sigmoid(gate))


def retention(proj, tables, batch, seq):
    t = proj.shape[0]
    nc = seq // RET_CHUNK
    w = RET_HEADS * HEAD_DIM
    base = 3 * MOBA_HEADS * HEAD_DIM // w
    tab_spec = pl.BlockSpec((RET_HEADS, RET_CHUNK, HEAD_DIM), lambda b, c: (0, 0, 0))
    return pl.pallas_call(
        _retention_kernel,
        out_shape=jax.ShapeDtypeStruct((t, w), F32),
        grid=(batch, nc),
        in_specs=[
            pl.BlockSpec((RET_CHUNK, w), lambda b, c: (b * nc + c, base)),
            pl.BlockSpec((RET_CHUNK, w), lambda b, c: (b * nc + c, base + 1)),
            pl.BlockSpec((RET_CHUNK, w), lambda b, c: (b * nc + c, base + 2)),
            pl.BlockSpec((RET_CHUNK, w), lambda b, c: (b * nc + c, base + 3)),
            tab_spec, tab_spec, tab_spec, tab_spec,
        ],
        out_specs=pl.BlockSpec((RET_CHUNK, w), lambda b, c: (b * nc + c, 0)),
        scratch_shapes=[pltpu.VMEM((RET_HEADS, HEAD_DIM, HEAD_DIM), F32)],
        compiler_params=_params(2),
        name="retention",
    )(proj, proj, proj, proj, *tables)


def _retention_tables():
    c = RET_CHUNK
    log_g = jnp.log(1.0 - 2.0 ** (-5.0 - jnp.arange(RET_HEADS, dtype=F32)))
    idx = jnp.arange(c, dtype=F32)
    diff = idx[:, None] - idx[None, :]
    dmask = jnp.where(diff >= 0, jnp.exp(log_g[:, None, None] * jnp.maximum(diff, 0.0)), 0.0)
    xi = jnp.exp(log_g[:, None] * (idx + 1.0))[..., None]
    zeta = jnp.exp(log_g[:, None] * (c - 1.0 - idx))[..., None]
    gc = jnp.exp(log_g * c)[:, None, None]
    full = (RET_HEADS, c, HEAD_DIM)
    return (dmask, jnp.broadcast_to(xi, full), jnp.broadcast_to(zeta, full), jnp.broadcast_to(gc, full))


def _even_out_kernel(h_ref, a_ref, b_ref, w_ref, o_ref):
    y = jnp.concatenate([_mxu(a_ref[...]), _mxu(b_ref[...])], axis=1)
    o_ref[...] = h_ref[...] + jnp.dot(y, w_ref[...], preferred_element_type=F32)


def even_out(h, mo, ro, w):
    t, d = h.shape
    tm = OUT_TM
    return pl.pallas_call(
        _even_out_kernel,
        out_shape=jax.ShapeDtypeStruct((t, d), F32),
        grid=(t // tm,),
        in_specs=[
            pl.BlockSpec((tm, d), lambda i: (i, 0)),
            pl.BlockSpec((tm, mo.shape[1]), lambda i: (i, 0)),
            pl.BlockSpec((tm, ro.shape[1]), lambda i: (i, 0)),
            pl.BlockSpec(w.shape, lambda i: (0, 0)),
        ],
        out_specs=pl.BlockSpec((tm, d), lambda i: (i, 0)),
        compiler_params=_params(1),
        name="even_out",
    )(h, mo, ro, w)


def _conv_out_kernel(h_ref, bg_ref, cg_ref, hx_ref, cgp_ref, hxp_ref, cw_ref, w_ref, o_ref, *, per_seq):
    i = pl.program_id(0)
    u = cg_ref[...] * hx_ref[...]
    up = jnp.where(i % per_seq == 0, 0.0, cgp_ref[...] * hxp_ref[...])
    row = lax.broadcasted_iota(jnp.int32, u.shape, 0)
    p1 = up[V7X_SUBLANES - 1:V7X_SUBLANES, :]
    p2 = up[V7X_SUBLANES - 2:V7X_SUBLANES - 1, :]
    u1 = jnp.where(row == 0, p1, pltpu.roll(u, 1, axis=0))
    u2 = jnp.where(row == 0, p2, jnp.where(row == 1, p1, pltpu.roll(u, 2, axis=0)))
    y = cw_ref[0:1, :] * u2 + cw_ref[1:2, :] * u1 + cw_ref[2:3, :] * u
    z = _mxu(bg_ref[...] * y)
    o_ref[...] = h_ref[...] + jnp.dot(z, w_ref[...], preferred_element_type=F32)


def conv_out(h, proj, conv_w, w, seq):
    t, d = h.shape
    tm = CONV_TM
    per_seq = seq // tm
    halo = tm // V7X_SUBLANES
    return pl.pallas_call(
        functools.partial(_conv_out_kernel, per_seq=per_seq),
        out_shape=jax.ShapeDtypeStruct((t, d), F32),
        grid=(t // tm,),
        in_specs=[
            pl.BlockSpec((tm, d), lambda i: (i, 0)),
            pl.BlockSpec((tm, d), lambda i: (i, 0)),
            pl.BlockSpec((tm, d), lambda i: (i, 1)),
            pl.BlockSpec((tm, d), lambda i: (i, 2)),
            pl.BlockSpec((V7X_SUBLANES, d), lambda i: (jnp.maximum(i * halo - 1, 0), 1)),
            pl.BlockSpec((V7X_SUBLANES, d), lambda i: (jnp.maximum(i * halo - 1, 0), 2)),
            pl.BlockSpec(conv_w.shape, lambda i: (0, 0)),
            pl.BlockSpec(w.shape, lambda i: (0, 0)),
        ],
        out_specs=pl.BlockSpec((tm, d), lambda i: (i, 0)),
        compiler_params=_params(1),
        name="conv_out",
    )(h, proj, proj, proj, proj, proj, conv_w, w)


def _peer_scores_kernel(h_ref, g_ref, wq_ref, keys_ref, xn_ref, sc_ref):
    xn = _mxu(_rms(h_ref[...], g_ref[...]))
    xn_ref[...] = xn
    n_hp = keys_ref.shape[0]
    per = 512 // PEER_HALF
    for c in range(n_hp // per):
        q = jnp.dot(xn, wq_ref[:, c * 512:(c + 1) * 512], preferred_element_type=F32)
        for u in range(per):
            hp = c * per + u
            qs = _mxu(q[:, u * PEER_HALF:(u + 1) * PEER_HALF])
            sc_ref[hp] = lax.dot_general(keys_ref[hp], qs, NT_DIMS, preferred_element_type=F32)


def peer_scores(h, g, wq, keys):
    t, d = h.shape
    tm = PEER_Q_TM
    n_hp = keys.shape[0]
    return pl.pallas_call(
        _peer_scores_kernel,
        out_shape=(jax.ShapeDtypeStruct((t, d), MXU_DTYPE), jax.ShapeDtypeStruct((n_hp, PEER_NKEYS, t), F32)),
        grid=(t // tm,),
        in_specs=[
            pl.BlockSpec((tm, d), lambda i: (i, 0)),
            pl.BlockSpec((1, d), lambda i: (0, 0)),
            pl.BlockSpec(wq.shape, lambda i: (0, 0)),
            pl.BlockSpec(keys.shape, lambda i: (0, 0, 0)),
        ],
        out_specs=(pl.BlockSpec((tm, d), lambda i: (i, 0)), pl.BlockSpec((n_hp, PEER_NKEYS, tm), lambda i: (0, 0, i))),
        compiler_params=_params(1),
        name="peer_scores",
    )(h, g.reshape(1, d), wq, keys)


def _topk_rows(x, pos, k):
    big = jnp.int32(2**30)
    vals, ids = [], []
    for _ in range(k):
        m = jnp.max(x, axis=0, keepdims=True)
        sel = jnp.min(jnp.where(x == m, pos, big), axis=0, keepdims=True)
        vals.append(m)
        ids.append(sel)
        x = jnp.where(pos == sel, NEG_INF, x)
    return jnp.concatenate(vals, axis=0), jnp.concatenate(ids, axis=0)


def _lookup_rows(table, idx):
    out = jnp.zeros(idx.shape, table.dtype)
    for a in range(table.shape[0]):
        out = jnp.where(idx == a, table[a:a + 1, :], out)
    return out


def _peer_topk_kernel(sc_ref, i1_ref, i2_ref, g_ref, l1_ref, l2_ref, lg_ref):
    k = PEER_TOPK
    tl = sc_ref.shape[2]
    key_pos = lax.broadcasted_iota(jnp.int32, (PEER_NKEYS, tl), 0)
    r16 = lax.broadcasted_iota(jnp.int32, (k, tl), 0)
    r8 = lax.broadcasted_iota(jnp.int32, (V7X_SUBLANES, tl), 0)
    cand_pos = [r16]
    for a in range(1, V7X_SUBLANES):
        cand_pos.append(a * k + r8)
    cand_pos.append((V7X_SUBLANES + r8) * k)
    cand_pos = jnp.concatenate(cand_pos, axis=0)

    def head(h, carry):
        v1, n1 = _topk_rows(sc_ref[2 * h], key_pos, k)
        v2, n2 = _topk_rows(sc_ref[2 * h + 1], key_pos, k)
        cand = [v1[0:1, :] + v2]
        for a in range(1, V7X_SUBLANES):
            piece = v1[a:a + 1, :] + v2[0:V7X_SUBLANES, :]
            cand.append(jnp.where(r8 < k // (a + 1), piece, NEG_INF))
        cand.append(v1[V7X_SUBLANES:, :] + v2[0:1, :])
        top_s, top_pos = _topk_rows(jnp.concatenate(cand, axis=0), cand_pos, k)
        e1 = _lookup_rows(n1, lax.shift_right_logical(top_pos, 4))
        e2 = _lookup_rows(n2, top_pos & (k - 1))
        ex = jnp.exp(top_s - top_s[0:1, :])
        gate = ex / jnp.sum(ex, axis=0, keepdims=True)
        base = pl.multiple_of(h * k, k)
        l1_ref[pl.ds(base, k), :] = e1
        l2_ref[pl.ds(base, k), :] = e2
        lg_ref[pl.ds(base, k), :] = gate
        return carry

    lax.fori_loop(0, sc_ref.shape[0] // 2, head, 0)
    for c in range(tl // V7X_LANES):
        cs = slice(c * V7X_LANES, (c + 1) * V7X_LANES)
        i1_ref[cs, :] = l1_ref[:, cs].T
        i2_ref[cs, :] = l2_ref[:, cs].T
        g_ref[cs, :] = lg_ref[:, cs].T


def peer_topk(sc):
    n_hp, nk, t = sc.shape
    tl = PEER_TOPK_TL
    nj = (n_hp // 2) * PEER_TOPK
    out_spec = pl.BlockSpec((tl, nj), lambda i: (i, 0))
    return pl.pallas_call(
        _peer_topk_kernel,
        out_shape=(jax.ShapeDtypeStruct((t, nj), jnp.int32), jax.ShapeDtypeStruct((t, nj), jnp.int32),
                   jax.ShapeDtypeStruct((t, nj), F32)),
        grid=(t // tl,),
        in_specs=[pl.BlockSpec((n_hp, nk, tl), lambda i: (0, 0, i))],
        out_specs=(out_spec, out_spec, out_spec),
        scratch_shapes=[pltpu.VMEM((nj, tl), jnp.int32), pltpu.VMEM((nj, tl), jnp.int32), pltpu.VMEM((nj, tl), F32)],
        compiler_params=_params(1),
        name="peer_topk",
    )(sc)


def _peer_mix_kernel(h_ref, xn_ref, i1_ref, i2_ref, g_ref, u_ref, v_ref, o_ref, w_ref, acc_ref, *, unroll):
    c = pl.program_id(1)
    tm = xn_ref.shape[0]
    nk = PEER_NKEYS
    half = nk // 2

    @pl.when(c == 0)
    def _():
        acc_ref[...] = jnp.zeros_like(acc_ref)
        r = lax.broadcasted_iota(jnp.int32, (nk, nk), 0)
        perm = jnp.where(r < half, 2 * r, 2 * (r - half) + 1)

        def tok(t, carry):
            i1 = i1_ref[pl.ds(t, 1), :]
            i2 = i2_ref[pl.ds(t, 1), :]
            g = g_ref[pl.ds(t, 1), :]
            a = _mxu(jnp.where(perm == i1, g, 0.0))
            b = _mxu(jnp.where(r == i2, 1.0, 0.0))
            w = lax.dot_general(a, b, NT_DIMS, preferred_element_type=F32)
            w = w.astype(jnp.bfloat16).astype(F32)
            hi = lax.bitcast_convert_type(w[:half, :], jnp.uint32) & jnp.uint32(0xFFFF0000)
            lo = lax.bitcast_convert_type(w[half:, :], jnp.uint32) >> 16
            w_ref[pl.ds(pl.multiple_of(t * W_PITCH, V7X_SUBLANES), half), :] = hi | lo
            return carry

        lax.fori_loop(0, tm, tok, 0, unroll=unroll)

    packed = w_ref[pl.ds(c, tm, stride=W_PITCH), :]
    w_even = lax.bitcast_convert_type(packed & jnp.uint32(0xFFFF0000), F32)
    w_odd = lax.bitcast_convert_type(packed << 16, F32)
    hid = lax.dot_general(xn_ref[...], u_ref[...], NT_DIMS, preferred_element_type=F32)
    act = 0.5 * hid * (1.0 + lax.erf(hid * (2.0 ** -0.5)))
    p = jnp.concatenate([w_even * act[:, :nk], w_odd * act[:, nk:]], axis=1)
    acc_ref[...] += jnp.dot(_mxu(p), v_ref[...], preferred_element_type=F32)

    @pl.when(c == pl.num_programs(1) - 1)
    def _():
        o_ref[...] = h_ref[...] + acc_ref[...]


def peer_mix(h, xn, i1, i2, g, u, v):
    t, d = h.shape
    tm = min(PEER_TM, t)
    ne = u.shape[0]
    tok_spec = pl.BlockSpec((tm, d), lambda i, c: (i, 0))
    list_spec = pl.BlockSpec((tm, i1.shape[1]), lambda i, c: (i, 0))
    exp_spec = pl.BlockSpec((PEER_ECHUNK, d), lambda i, c: (c, 0))
    return pl.pallas_call(
        functools.partial(_peer_mix_kernel, unroll=8),
        out_shape=jax.ShapeDtypeStruct((t, d), F32),
        grid=(t // tm, ne // PEER_ECHUNK),
        in_specs=[tok_spec, tok_spec, list_spec, list_spec, list_spec, exp_spec, exp_spec],
        out_specs=tok_spec,
        scratch_shapes=[pltpu.VMEM((tm * W_PITCH, PEER_NKEYS), jnp.uint32), pltpu.VMEM((tm, d), F32)],
        compiler_params=_params(2),
        name="peer_mix",
    )(h, xn, i1, i2, g, u, v)


def _final_norm_kernel(h_ref, g_ref, o_ref):
    o_ref[...] = _rms(h_ref[...], g_ref[...])


def final_rmsnorm(h, g):
    t, d = h.shape
    tm = OUT_TM
    return pl.pallas_call(
        _final_norm_kernel,
        out_shape=jax.ShapeDtypeStruct((t, d), F32),
        grid=(t // tm,),
        in_specs=[pl.BlockSpec((tm, d), lambda i: (i, 0)), pl.BlockSpec((1, d), lambda i: (0, 0))],
        out_specs=pl.BlockSpec((tm, d), lambda i: (i, 0)),
        compiler_params=_params(1),
        name="final_norm",
    )(h, g.reshape(1, d))


def _rope_tables(seq):
    half = HEAD_DIM // 2
    inv = ROPE_THETA ** (-jnp.arange(half, dtype=F32) / half)
    ang = jnp.arange(seq, dtype=F32)[:, None] * inv[None, :]
    cos, sin = jnp.cos(ang), jnp.sin(ang)
    return jnp.concatenate([cos, cos], axis=1), jnp.concatenate([-sin, sin], axis=1)


def kernel(x, norm_mix, norm_ffn, even_w_in, even_w_out, odd_w_in, odd_conv, odd_w_out, peer_w_q, peer_sub_keys,
           peer_u, peer_v, final_norm):
    batch, seq, d = x.shape
    depth = norm_mix.shape[0]
    cos_t, sin_t = _rope_tables(seq)
    ret_tables = _retention_tables()
    mh, rh = MOBA_HEADS, RET_HEADS
    rope_blocks = tuple(range(0, 2 * mh)) + tuple(range(3 * mh, 3 * mh + 2 * rh))
    ret_k_blocks = tuple(range(3 * mh + rh, 3 * mh + 2 * rh))

    h = x.reshape(batch * seq, d)
    for layer in range(depth):
        i = layer // 2
        if layer % 2 == 0:
            proj = norm_proj(h, norm_mix[layer], _mxu(even_w_in[i]), cos_t, sin_t, seq, rope_blocks, ret_k_blocks,
                             HEAD_DIM ** -0.5)
            mo = moba(proj, batch, seq)
            ro = retention(proj, ret_tables, batch, seq)
            h = even_out(h, mo, ro, _mxu(even_w_out[i]))
        else:
            proj = norm_proj(h, norm_mix[layer], _mxu(odd_w_in[i]), cos_t, sin_t, seq)
            h = conv_out(h, proj, odd_conv[i], _mxu(odd_w_out[i]), seq)
        keys = _mxu(peer_sub_keys[layer].reshape(2 * PEER_HEADS, PEER_NKEYS, PEER_HALF))
        xn, sc = peer_scores(h, norm_ffn[layer], _mxu(peer_w_q[layer]), keys)
        i1, i2, g = peer_topk(sc)
        h = peer_mix(h, xn, i1, i2, g, _mxu(peer_u[layer]), _mxu(peer_v[layer]))
    return final_rmsnorm(h, final_norm).reshape(batch, seq, d)
```

```python
import functools

import jax
import jax.numpy as jnp
import numpy as np
from jax import lax
from jax.experimental import pallas as pl
from jax.experimental.pallas import tpu as pltpu

HEAD_DIM = 128
MOBA_HEADS = 4
RET_HEADS = 4
MOBA_BLOCK = 256
MOBA_TOPK = 3
RET_CHUNK = 128
PEER_HEADS = 8
PEER_NKEYS = 128
PEER_TOPK = 16
PEER_HALF = 128
ROPE_THETA = 10000.0
EPS = 1e-6

V7X_LANES = 128
V7X_SUBLANES = 8
V7X_VMEM_BYTES = 64 * 2**20
VMEM_LIMIT = V7X_VMEM_BYTES - 8 * 2**20

MXU_DTYPE = jnp.bfloat16
F32 = jnp.float32
NEG_INF = float("-inf")
NT_DIMS = (((1,), (1,)), ((), ()))
TN_DIMS = (((0,), (0,)), ((), ()))

PROJ_TM = 256
OUT_TM = 512
CONV_TM = 256
PEER_Q_TM = 256
PEER_TOPK_TL = 256
PEER_TM = 512
PEER_ECHUNK = 2 * PEER_NKEYS
W_PITCH = 72


def _params(n_axes):
    return pltpu.CompilerParams(dimension_semantics=("arbitrary",) * n_axes, vmem_limit_bytes=VMEM_LIMIT)


def _mxu(x):
    return x.astype(MXU_DTYPE)


def _rms(x, g):
    return x * lax.rsqrt(jnp.mean(x * x, axis=-1, keepdims=True) + EPS) * g


def _norm_proj_kernel(h_ref, g_ref, w_ref, cos_ref, sin_ref, o_ref, *, rope_blocks, scale_blocks, scale, chunk):
    xn = _mxu(_rms(h_ref[...], g_ref[...]))
    n = o_ref.shape[1]
    for c0 in range(0, n, chunk):
        y = jnp.dot(xn, w_ref[:, c0:c0 + chunk], preferred_element_type=F32)
        for u in range(chunk // HEAD_DIM):
            blk = (c0 // HEAD_DIM) + u
            t = y[:, u * HEAD_DIM:(u + 1) * HEAD_DIM]
            if blk in rope_blocks:
                t = t * cos_ref[...] + pltpu.roll(t, HEAD_DIM // 2, axis=1) * sin_ref[...]
            if blk in scale_blocks:
                t = t * scale
            o_ref[:, blk * HEAD_DIM:(blk + 1) * HEAD_DIM] = t


def norm_proj(h, g, w, cos_t, sin_t, seq, rope_blocks=(), scale_blocks=(), scale=1.0):
    t, d = h.shape
    n = w.shape[1]
    tm = PROJ_TM
    per_seq = seq // tm
    return pl.pallas_call(
        functools.partial(_norm_proj_kernel, rope_blocks=frozenset(rope_blocks), scale_blocks=frozenset(scale_blocks),
                          scale=scale, chunk=512),
        out_shape=jax.ShapeDtypeStruct((t, n), F32),
        grid=(t // tm,),
        in_specs=[
            pl.BlockSpec((tm, d), lambda i: (i, 0)),
            pl.BlockSpec((1, d), lambda i: (0, 0)),
            pl.BlockSpec((d, n), lambda i: (0, 0)),
            pl.BlockSpec((tm, HEAD_DIM), lambda i: (i % per_seq, 0)),
            pl.BlockSpec((tm, HEAD_DIM), lambda i: (i % per_seq, 0)),
        ],
        out_specs=pl.BlockSpec((tm, n), lambda i: (i, 0)),
        compiler_params=_params(1),
        name="norm_proj",
    )(h, g.reshape(1, d), w, cos_t, sin_t)


def _moba_kernel(q_ref, k_ref, v_ref, o_ref, kb_ref, vt_ref, km_ref, sel_ref, s_ref, *, nb, nbp):
    i = pl.program_id(2)
    blk = MOBA_BLOCK

    @pl.when(i == 0)
    def _():
        km_ref[...] = jnp.zeros_like(km_ref)
        for j in range(nb):
            rows = slice(j * blk, (j + 1) * blk)
            kb_ref[j] = _mxu(k_ref[rows, :])
            vt_ref[j] = _mxu(v_ref[rows, :].T)
            km_ref[j:j + 1, :] = jnp.sum(k_ref[rows, :], axis=0, keepdims=True) / float(blk)

    q = q_ref[...]
    qb = _mxu(q)
    gate = lax.dot_general(km_ref[...], q, NT_DIMS, precision=lax.Precision.HIGHEST, preferred_element_type=F32)
    sub = lax.broadcasted_iota(jnp.int32, (nbp, blk), 0)
    rank = jnp.zeros((nbp, blk), jnp.int32)
    for jp in range(nb):
        other = gate[jp:jp + 1, :]
        beats = jnp.where(other > gate, 1, jnp.where(other == gate, jnp.where(jp < sub, 1, 0), 0))
        rank = rank + jnp.where(jp < i, beats, 0)
    sel_ref[...] = jnp.where(sub < i, jnp.where(rank < MOBA_TOPK, 1.0, 0.0), 0.0)

    scale = HEAD_DIM ** -0.5

    def scores_t(j):
        return lax.dot_general(kb_ref[j], qb, NT_DIMS, preferred_element_type=F32) * scale

    kpos = lax.broadcasted_iota(jnp.int32, (blk, blk), 0)
    qpos = lax.broadcasted_iota(jnp.int32, (blk, blk), 1)
    s_own = jnp.where(kpos <= qpos, scores_t(i), NEG_INF)
    s_ref[nb] = s_own
    m = jnp.max(s_own, axis=0, keepdims=True)

    def past_scores(j):
        s = jnp.where(sel_ref[pl.ds(j, 1), :] > 0.0, scores_t(j), NEG_INF)
        s_ref[j] = s
        return jnp.max(s, axis=0, keepdims=True)

    def max_body(n, m):
        return jnp.maximum(m, jnp.maximum(past_scores(2 * n), past_scores(2 * n + 1)))

    n_pairs = (i + 1) // 2
    m = lax.fori_loop(0, n_pairs, max_body, m)

    def weigh(slot, j):
        p = jnp.exp(s_ref[slot] - m)
        return jnp.sum(p, axis=0, keepdims=True), jnp.dot(vt_ref[j], _mxu(p), preferred_element_type=F32)

    def sum_body(n, carry):
        l, acc = carry
        l0, a0 = weigh(2 * n, 2 * n)
        l1, a1 = weigh(2 * n + 1, 2 * n + 1)
        return l + (l0 + l1), acc + (a0 + a1)

    l, acc = lax.fori_loop(0, n_pairs, sum_body, weigh(nb, i))
    o_ref[...] = (acc / l).T


def moba(proj, batch, seq):
    t = proj.shape[0]
    nb = seq // MOBA_BLOCK
    nbp = max(nb, V7X_SUBLANES)
    hd = HEAD_DIM
    return pl.pallas_call(
        functools.partial(_moba_kernel, nb=nb, nbp=nbp),
        out_shape=jax.ShapeDtypeStruct((t, MOBA_HEADS * hd), F32),
        grid=(batch, MOBA_HEADS, nb),
        in_specs=[
            pl.BlockSpec((MOBA_BLOCK, hd), lambda b, h, i: (b * nb + i, h)),
            pl.BlockSpec((seq, hd), lambda b, h, i: (b, MOBA_HEADS + h)),
            pl.BlockSpec((seq, hd), lambda b, h, i: (b, 2 * MOBA_HEADS + h)),
        ],
        out_specs=pl.BlockSpec((MOBA_BLOCK, hd), lambda b, h, i: (b * nb + i, h)),
        scratch_shapes=[
            pltpu.VMEM((nb, MOBA_BLOCK, hd), MXU_DTYPE),
            pltpu.VMEM((nb, hd, MOBA_BLOCK), MXU_DTYPE),
            pltpu.VMEM((nbp, hd), F32),
            pltpu.VMEM((nbp, MOBA_BLOCK), F32),
            pltpu.VMEM((nb + 1, MOBA_BLOCK, MOBA_BLOCK), F32),
        ],
        compiler_params=_params(3),
        name="moba",
    )(proj, proj, proj)


def _retention_kernel(q_ref, k_ref, v_ref, g_ref, dm_ref, xi_ref, ze_ref, gc_ref, o_ref, st_ref):
    c = pl.program_id(1)

    @pl.when(c == 0)
    def _():
        st_ref[...] = jnp.zeros_like(st_ref)

    for h in range(RET_HEADS):
        sl = slice(h * HEAD_DIM, (h + 1) * HEAD_DIM)
        q = _mxu(q_ref[:, sl])
        k = k_ref[:, sl]
        v = _mxu(v_ref[:, sl])
        st = st_ref[h]
        inner = lax.dot_general(q, _mxu(k), NT_DIMS, preferred_element_type=F32) * dm_ref[h]
        o = (jnp.dot(_mxu(inner), v, preferred_element_type=F32)
             + jnp.dot(q, _mxu(st), preferred_element_type=F32) * xi_ref[h])
        st_ref[h] = st * gc_ref[h] + lax.dot_general(_mxu(k * ze_ref[h]), v, TN_DIMS, preferred_element_type=F32)
        o = o * lax.rsqrt(jnp.mean(o * o, axis=-1, keepdims=True) + EPS)
        gate = g_ref[:, sl]
        o_ref[:, sl] = o * (gate * jax.nn.sigmoid(gate))


def retention(proj, tables, batch, seq):
    t = proj.shape[0]
    nc = seq // RET_CHUNK
    w = RET_HEADS * HEAD_DIM
    base = 3 * MOBA_HEADS * HEAD_DIM // w
    tab_spec = pl.BlockSpec((RET_HEADS, RET_CHUNK, HEAD_DIM), lambda b, c: (0, 0, 0))
    return pl.pallas_call(
        _retention_kernel,
        out_shape=jax.ShapeDtypeStruct((t, w), F32),
        grid=(batch, nc),
        in_specs=[
            pl.BlockSpec((RET_CHUNK, w), lambda b, c: (b * nc + c, base)),
            pl.BlockSpec((RET_CHUNK, w), lambda b, c: (b * nc + c, base + 1)),
            pl.BlockSpec((RET_CHUNK, w), lambda b, c: (b * nc + c, base + 2)),
            pl.BlockSpec((RET_CHUNK, w), lambda b, c: (b * nc + c, base + 3)),
            tab_spec, tab_spec, tab_spec, tab_spec,
        ],
        out_specs=pl.BlockSpec((RET_CHUNK, w), lambda b, c: (b * nc + c, 0)),
        scratch_shapes=[pltpu.VMEM((RET_HEADS, HEAD_DIM, HEAD_DIM), F32)],
        compiler_params=_params(2),
        name="retention",
    )(proj, proj, proj, proj, *tables)


def _retention_tables():
    c = RET_CHUNK
    log_g = jnp.log(1.0 - 2.0 ** (-5.0 - jnp.arange(RET_HEADS, dtype=F32)))
    idx = jnp.arange(c, dtype=F32)
    diff = idx[:, None] - idx[None, :]
    dmask = jnp.where(diff >= 0, jnp.exp(log_g[:, None, None] * jnp.maximum(diff, 0.0)), 0.0)
    xi = jnp.exp(log_g[:, None] * (idx + 1.0))[..., None]
    zeta = jnp.exp(log_g[:, None] * (c - 1.0 - idx))[..., None]
    gc = jnp.exp(log_g * c)[:, None, None]
    full = (RET_HEADS, c, HEAD_DIM)
    return (dmask, jnp.broadcast_to(xi, full), jnp.broadcast_to(zeta, full), jnp.broadcast_to(gc, full))


def _even_out_kernel(h_ref, a_ref, b_ref, w_ref, o_ref):
    y = jnp.concatenate([_mxu(a_ref[...]), _mxu(b_ref[...])], axis=1)
    o_ref[...] = h_ref[...] + jnp.dot(y, w_ref[...], preferred_element_type=F32)


def even_out(h, mo, ro, w):
    t, d = h.shape
    tm = OUT_TM
    return pl.pallas_call(
        _even_out_kernel,
        out_shape=jax.ShapeDtypeStruct((t, d), F32),
        grid=(t // tm,),
        in_specs=[
            pl.BlockSpec((tm, d), lambda i: (i, 0)),
            pl.BlockSpec((tm, mo.shape[1]), lambda i: (i, 0)),
            pl.BlockSpec((tm, ro.shape[1]), lambda i: (i, 0)),
            pl.BlockSpec(w.shape, lambda i: (0, 0)),
        ],
        out_specs=pl.BlockSpec((tm, d), lambda i: (i, 0)),
        compiler_params=_params(1),
        name="even_out",
    )(h, mo, ro, w)


def _conv_out_kernel(h_ref, bg_ref, cg_ref, hx_ref, cgp_ref, hxp_ref, cw_ref, w_ref, o_ref, *, per_seq):
    i = pl.program_id(0)
    u = cg_ref[...] * hx_ref[...]
    up = jnp.where(i % per_seq == 0, 0.0, cgp_ref[...] * hxp_ref[...])
    row = lax.broadcasted_iota(jnp.int32, u.shape, 0)
    p1 = up[V7X_SUBLANES - 1:V7X_SUBLANES, :]
    p2 = up[V7X_SUBLANES - 2:V7X_SUBLANES - 1, :]
    u1 = jnp.where(row == 0, p1, pltpu.roll(u, 1, axis=0))
    u2 = jnp.where(row == 0, p2, jnp.where(row == 1, p1, pltpu.roll(u, 2, axis=0)))
    y = cw_ref[0:1, :] * u2 + cw_ref[1:2, :] * u1 + cw_ref[2:3, :] * u
    z = _mxu(bg_ref[...] * y)
    o_ref[...] = h_ref[...] + jnp.dot(z, w_ref[...], preferred_element_type=F32)


def conv_out(h, proj, conv_w, w, seq):
    t, d = h.shape
    tm = CONV_TM
    per_seq = seq // tm
    halo = tm // V7X_SUBLANES
    return pl.pallas_call(
        functools.partial(_conv_out_kernel, per_seq=per_seq),
        out_shape=jax.ShapeDtypeStruct((t, d), F32),
        grid=(t // tm,),
        in_specs=[
            pl.BlockSpec((tm, d), lambda i: (i, 0)),
            pl.BlockSpec((tm, d), lambda i: (i, 0)),
            pl.BlockSpec((tm, d), lambda i: (i, 1)),
            pl.BlockSpec((tm, d), lambda i: (i, 2)),
            pl.BlockSpec((V7X_SUBLANES, d), lambda i: (jnp.maximum(i * halo - 1, 0), 1)),
            pl.BlockSpec((V7X_SUBLANES, d), lambda i: (jnp.maximum(i * halo - 1, 0), 2)),
            pl.BlockSpec(conv_w.shape, lambda i: (0, 0)),
            pl.BlockSpec(w.shape, lambda i: (0, 0)),
        ],
        out_specs=pl.BlockSpec((tm, d), lambda i: (i, 0)),
        compiler_params=_params(1),
        name="conv_out",
    )(h, proj, proj, proj, proj, proj, conv_w, w)


def _peer_scores_kernel(h_ref, g_ref, wq_ref, keys_ref, xn_ref, sc_ref):
    xn = _mxu(_rms(h_ref[...], g_ref[...]))
    xn_ref[...] = xn
    n_hp = keys_ref.shape[0]
    per = 512 // PEER_HALF
    for c in range(n_hp // per):
        q = jnp.dot(xn, wq_ref[:, c * 512:(c + 1) * 512], preferred_element_type=F32)
        for u in range(per):
            hp = c * per + u
            qs = _mxu(q[:, u * PEER_HALF:(u + 1) * PEER_HALF])
            sc_ref[hp] = lax.dot_general(keys_ref[hp], qs, NT_DIMS, preferred_element_type=F32)


def peer_scores(h, g, wq, keys):
    t, d = h.shape
    tm = PEER_Q_TM
    n_hp = keys.shape[0]
    return pl.pallas_call(
        _peer_scores_kernel,
        out_shape=(jax.ShapeDtypeStruct((t, d), MXU_DTYPE), jax.ShapeDtypeStruct((n_hp, PEER_NKEYS, t), F32)),
        grid=(t // tm,),
        in_specs=[
            pl.BlockSpec((tm, d), lambda i: (i, 0)),
            pl.BlockSpec((1, d), lambda i: (0, 0)),
            pl.BlockSpec(wq.shape, lambda i: (0, 0)),
            pl.BlockSpec(keys.shape, lambda i: (0, 0, 0)),
        ],
        out_specs=(pl.BlockSpec((tm, d), lambda i: (i, 0)), pl.BlockSpec((n_hp, PEER_NKEYS, tm), lambda i: (0, 0, i))),
        compiler_params=_params(1),
        name="peer_scores",
    )(h, g.reshape(1, d), wq, keys)


def _topk_rows(x, pos, k):
    big = jnp.int32(2**30)
    vals, ids = [], []
    for _ in range(k):
        m = jnp.max(x, axis=0, keepdims=True)
        sel = jnp.min(jnp.where(x == m, pos, big), axis=0, keepdims=True)
        vals.append(m)
        ids.append(sel)
        x = jnp.where(pos == sel, NEG_INF, x)
    return jnp.concatenate(vals, axis=0), jnp.concatenate(ids, axis=0)


def _lookup_rows(table, idx):
    out = jnp.zeros(idx.shape, table.dtype)
    for a in range(table.shape[0]):
        out = jnp.where(idx == a, table[a:a + 1, :], out)
    return out


def _peer_topk_kernel(sc_ref, i1_ref, i2_ref, g_ref, l1_ref, l2_ref, lg_ref):
    k = PEER_TOPK
    tl = sc_ref.shape[2]
    key_pos = lax.broadcasted_iota(jnp.int32, (PEER_NKEYS, tl), 0)
    r16 = lax.broadcasted_iota(jnp.int32, (k, tl), 0)
    r8 = lax.broadcasted_iota(jnp.int32, (V7X_SUBLANES, tl), 0)
    cand_pos = [r16]
    for a in range(1, V7X_SUBLANES):
        cand_pos.append(a * k + r8)
    cand_pos.append((V7X_SUBLANES + r8) * k)
    cand_pos = jnp.concatenate(cand_pos, axis=0)

    def head(h, carry):
        v1, n1 = _topk_rows(sc_ref[2 * h], key_pos, k)
        v2, n2 = _topk_rows(sc_ref[2 * h + 1], key_pos, k)
        cand = [v1[0:1, :] + v2]
        for a in range(1, V7X_SUBLANES):
            piece = v1[a:a + 1, :] + v2[0:V7X_SUBLANES, :]
            cand.append(jnp.where(r8 < k // (a + 1), piece, NEG_INF))
        cand.append(v1[V7X_SUBLANES:, :] + v2[0:1, :])
        top_s, top_pos = _topk_rows(jnp.concatenate(cand, axis=0), cand_pos, k)
        e1 = _lookup_rows(n1, lax.shift_right_logical(top_pos, k.bit_length() - 1))
        e2 = _lookup_rows(n2, top_pos & (k - 1))
        ex = jnp.exp(top_s - top_s[0:1, :])
        gate = ex / jnp.sum(ex, axis=0, keepdims=True)
        base = pl.multiple_of(h * k, k)
        l1_ref[pl.ds(base, k), :] = e1
        l2_ref[pl.ds(base, k), :] = e2
        lg_ref[pl.ds(base, k), :] = gate
        return carry

    lax.fori_loop(0, sc_ref.shape[0] // 2, head, 0)
    for c in range(tl // V7X_LANES):
        cs = slice(c * V7X_LANES, (c + 1) * V7X_LANES)
        i1_ref[cs, :] = l1_ref[:, cs].T
        i2_ref[cs, :] = l2_ref[:, cs].T
        g_ref[cs, :] = lg_ref[:, cs].T


def peer_topk(sc):
    n_hp, nk, t = sc.shape
    tl = PEER_TOPK_TL
    nj = (n_hp // 2) * PEER_TOPK
    out_spec = pl.BlockSpec((tl, nj), lambda i: (i, 0))
    return pl.pallas_call(
        _peer_topk_kernel,
        out_shape=(jax.ShapeDtypeStruct((t, nj), jnp.int32), jax.ShapeDtypeStruct((t, nj), jnp.int32),
                   jax.ShapeDtypeStruct((t, nj), F32)),
        grid=(t // tl,),
        in_specs=[pl.BlockSpec((n_hp, nk, tl), lambda i: (0, 0, i))],
        out_specs=(out_spec, out_spec, out_spec),
        scratch_shapes=[pltpu.VMEM((nj, tl), jnp.int32), pltpu.VMEM((nj, tl), jnp.int32), pltpu.VMEM((nj, tl), F32)],
        compiler_params=_params(1),
        name="peer_topk",
    )(sc)


def _peer_mix_kernel(h_ref, xn_ref, i1_ref, i2_ref, g_ref, ua_ref, ub_ref, va_ref, vb_ref, o_ref, w_ref, ha_ref,
                     hb_ref, acc_ref, *, unroll, n_chunks):
    c = pl.program_id(1)
    last = pl.num_programs(1) - 1
    tm = xn_ref.shape[0]
    nk = PEER_NKEYS
    half = nk // 2

    @pl.when(c == 0)
    def _():
        acc_ref[...] = jnp.zeros_like(acc_ref)
        hb_ref[...] = jnp.zeros_like(hb_ref)
        r = lax.broadcasted_iota(jnp.int32, (nk, nk), 0)
        perm = jnp.where(r < half, 2 * r, 2 * (r - half) + 1)

        def tok(t, carry):
            i1 = i1_ref[pl.ds(t, 1), :]
            i2 = i2_ref[pl.ds(t, 1), :]
            g = g_ref[pl.ds(t, 1), :]
            a = _mxu(jnp.where(perm == i1, g, 0.0))
            b = _mxu(jnp.where(r == i2, 1.0, 0.0))
            w = lax.dot_general(a, b, NT_DIMS, preferred_element_type=F32)
            w = w.astype(jnp.bfloat16).astype(F32)
            hi = lax.bitcast_convert_type(w[:half, :], jnp.uint32) & jnp.uint32(0xFFFF0000)
            lo = lax.bitcast_convert_type(w[half:, :], jnp.uint32) >> 16
            w_ref[pl.ds(pl.multiple_of(t * W_PITCH, V7X_SUBLANES), half), :] = hi | lo
            return carry

        lax.fori_loop(0, tm, tok, 0, unroll=unroll)

    def hidden(u_ref):
        return lax.dot_general(xn_ref[...], u_ref[...], NT_DIMS, preferred_element_type=F32)

    def consume(hid_ref, chunk, v_ref, valid):
        packed = w_ref[pl.ds(jnp.clip(chunk, 0, n_chunks - 1), tm, stride=W_PITCH), :]
        if valid is not None:
            packed = jnp.where(valid, packed, jnp.uint32(0))
        w_even = lax.bitcast_convert_type(packed & jnp.uint32(0xFFFF0000), F32)
        w_odd = lax.bitcast_convert_type(packed << 16, F32)
        hid = hid_ref[...]
        act = 0.5 * hid * (1.0 + lax.erf(hid * (2.0 ** -0.5)))
        p = jnp.concatenate([w_even * act[:, :nk], w_odd * act[:, nk:]], axis=1)
        acc_ref[...] += jnp.dot(_mxu(p), v_ref[...], preferred_element_type=F32)

    ha_ref[...] = hidden(ua_ref)
    consume(hb_ref, 2 * c - 1, va_ref, None)
    hb_ref[...] = hidden(ub_ref)
    consume(ha_ref, 2 * c, vb_ref, c < last)

    @pl.when(c == last)
    def _():
        o_ref[...] = h_ref[...] + acc_ref[...]


def peer_mix(h, xn, i1, i2, g, u, v):
    t, d = h.shape
    tm = min(PEER_TM, t)
    n_chunks = u.shape[0] // PEER_ECHUNK
    steps = n_chunks // 2 + 1
    tok_spec = pl.BlockSpec((tm, d), lambda i, c: (i, 0))
    list_spec = pl.BlockSpec((tm, i1.shape[1]), lambda i, c: (i, 0))

    def exp_spec(offset):
        return pl.BlockSpec((PEER_ECHUNK, d), lambda i, c: (jnp.clip(2 * c + offset, 0, n_chunks - 1), 0))

    return pl.pallas_call(
        functools.partial(_peer_mix_kernel, unroll=16, n_chunks=n_chunks),
        out_shape=jax.ShapeDtypeStruct((t, d), F32),
        grid=(t // tm, steps),
        in_specs=[tok_spec, tok_spec, list_spec, list_spec, list_spec, exp_spec(0), exp_spec(1), exp_spec(-1),
                  exp_spec(0)],
        out_specs=tok_spec,
        scratch_shapes=[
            pltpu.VMEM((tm * W_PITCH, PEER_NKEYS), jnp.uint32),
            pltpu.VMEM((tm, PEER_ECHUNK), F32),
            pltpu.VMEM((tm, PEER_ECHUNK), F32),
            pltpu.VMEM((tm, d), F32),
        ],
        compiler_params=_params(2),
        name="peer_mix",
    )(h, xn, i1, i2, g, u, u, v, v)


def _final_norm_kernel(h_ref, g_ref, o_ref):
    o_ref[...] = _rms(h_ref[...], g_ref[...])


def final_rmsnorm(h, g):
    t, d = h.shape
    tm = OUT_TM
    return pl.pallas_call(
        _final_norm_kernel,
        out_shape=jax.ShapeDtypeStruct((t, d), F32),
        grid=(t // tm,),
        in_specs=[pl.BlockSpec((tm, d), lambda i: (i, 0)), pl.BlockSpec((1, d), lambda i: (0, 0))],
        out_specs=pl.BlockSpec((tm, d), lambda i: (i, 0)),
        compiler_params=_params(1),
        name="final_norm",
    )(h, g.reshape(1, d))


def _rope_tables(seq):
    half = HEAD_DIM // 2
    inv = ROPE_THETA ** (-jnp.arange(half, dtype=F32) / half)
    ang = jnp.arange(seq, dtype=F32)[:, None] * inv[None, :]
    cos, sin = jnp.cos(ang), jnp.sin(ang)
    return jnp.concatenate([cos, cos], axis=1), jnp.concatenate([-sin, sin], axis=1)


def kernel(x, norm_mix, norm_ffn, even_w_in, even_w_out, odd_w_in, odd_conv, odd_w_out, peer_w_q, peer_sub_keys,
           peer_u, peer_v, final_norm):
    batch, seq, d = x.shape
    depth = norm_mix.shape[0]
    cos_t, sin_t = _rope_tables(seq)
    ret_tables = _retention_tables()
    mh, rh = MOBA_HEADS, RET_HEADS
    rope_blocks = tuple(range(0, 2 * mh)) + tuple(range(3 * mh, 3 * mh + 2 * rh))
    ret_k_blocks = tuple(range(3 * mh + rh, 3 * mh + 2 * rh))

    h = x.reshape(batch * seq, d)
    for layer in range(depth):
        i = layer // 2
        if layer % 2 == 0:
            proj = norm_proj(h, norm_mix[layer], _mxu(even_w_in[i]), cos_t, sin_t, seq, rope_blocks, ret_k_blocks,
                             HEAD_DIM ** -0.5)
            mo = moba(proj, batch, seq)
            ro = retention(proj, ret_tables, batch, seq)
            h = even_out(h, mo, ro, _mxu(even_w_out[i]))
        else:
            proj = norm_proj(h, norm_mix[layer], _mxu(odd_w_in[i]), cos_t, sin_t, seq)
            h = conv_out(h, proj, odd_conv[i], _mxu(odd_w_out[i]), seq)
        keys = _mxu(peer_sub_keys[layer].reshape(2 * PEER_HEADS, PEER_NKEYS, PEER_HALF))
        xn, sc = peer_scores(h, norm_ffn[layer], _mxu(peer_w_q[layer]), keys)
        i1, i2, g = peer_topk(sc)
        h = peer_mix(h, xn, i1, i2, g, _mxu(peer_u[layer]), _mxu(peer_v[layer]))
    return final_rmsnorm(h, final_norm).reshape(batch, seq, d)
```

```python
import functools

import jax
import jax.numpy as jnp
import numpy as np
from jax import lax
from jax.experimental import pallas as pl
from jax.experimental.pallas import tpu as pltpu

HEAD_DIM = 128
MOBA_HEADS = 4
RET_HEADS = 4
MOBA_BLOCK = 256
MOBA_TOPK = 3
RET_CHUNK = 128
PEER_HEADS = 8
PEER_NKEYS = 128
PEER_TOPK = 16
PEER_HALF = 128
ROPE_THETA = 10000.0
EPS = 1e-6

V7X_LANES = 128
V7X_SUBLANES = 8
V7X_VMEM_BYTES = 64 * 2**20
VMEM_LIMIT = V7X_VMEM_BYTES - 8 * 2**20

MXU_DTYPE = jnp.bfloat16
F32 = jnp.float32
NEG_INF = float("-inf")
NT_DIMS = (((1,), (1,)), ((), ()))
TN_DIMS = (((0,), (0,)), ((), ()))

PROJ_TM = 256
OUT_TM = 512
CONV_TM = 256
PEER_Q_TM = 256
PEER_TOPK_TL = 256
PEER_TM = 512
PEER_ECHUNK = 2 * PEER_NKEYS
W_PITCH = 72


def _params(n_axes, flags=None):
    return pltpu.CompilerParams(dimension_semantics=("arbitrary",) * n_axes, vmem_limit_bytes=VMEM_LIMIT,
                                flags=flags)


def _mxu(x):
    return x.astype(MXU_DTYPE)


def _rms(x, g):
    return x * lax.rsqrt(jnp.mean(x * x, axis=-1, keepdims=True) + EPS) * g


def _norm_proj_kernel(h_ref, g_ref, w_ref, cos_ref, sin_ref, o_ref, *, rope_blocks, scale_blocks, scale, chunk):
    xn = _mxu(_rms(h_ref[...], g_ref[...]))
    n = o_ref.shape[1]
    for c0 in range(0, n, chunk):
        y = jnp.dot(xn, w_ref[:, c0:c0 + chunk], preferred_element_type=F32)
        for u in range(chunk // HEAD_DIM):
            blk = (c0 // HEAD_DIM) + u
            t = y[:, u * HEAD_DIM:(u + 1) * HEAD_DIM]
            if blk in rope_blocks:
                t = t * cos_ref[...] + pltpu.roll(t, HEAD_DIM // 2, axis=1) * sin_ref[...]
            if blk in scale_blocks:
                t = t * scale
            o_ref[:, blk * HEAD_DIM:(blk + 1) * HEAD_DIM] = t


def norm_proj(h, g, w, cos_t, sin_t, seq, rope_blocks=(), scale_blocks=(), scale=1.0):
    t, d = h.shape
    n = w.shape[1]
    tm = PROJ_TM
    per_seq = seq // tm
    return pl.pallas_call(
        functools.partial(_norm_proj_kernel, rope_blocks=frozenset(rope_blocks), scale_blocks=frozenset(scale_blocks),
                          scale=scale, chunk=512),
        out_shape=jax.ShapeDtypeStruct((t, n), F32),
        grid=(t // tm,),
        in_specs=[
            pl.BlockSpec((tm, d), lambda i: (i, 0)),
            pl.BlockSpec((1, d), lambda i: (0, 0)),
            pl.BlockSpec((d, n), lambda i: (0, 0)),
            pl.BlockSpec((tm, HEAD_DIM), lambda i: (i % per_seq, 0)),
            pl.BlockSpec((tm, HEAD_DIM), lambda i: (i % per_seq, 0)),
        ],
        out_specs=pl.BlockSpec((tm, n), lambda i: (i, 0)),
        compiler_params=_params(1),
        name="norm_proj",
    )(h, g.reshape(1, d), w, cos_t, sin_t)


def _moba_kernel(qa_ref, qb_ref, k_ref, v_ref, oa_ref, ob_ref, kb_ref, vt_ref, km_ref, q2_ref, sel_ref, s_ref,
                 acc_ref, l_ref, *, nb, nbp):
    i = pl.program_id(2)
    blk = MOBA_BLOCK
    own = (i, nb - 1 - i)

    @pl.when(i == 0)
    def _():
        km_ref[...] = jnp.zeros_like(km_ref)
        for j in range(nb):
            rows = slice(j * blk, (j + 1) * blk)
            kb_ref[j] = _mxu(k_ref[rows, :])
            vt_ref[j] = _mxu(v_ref[rows, :].T)
            km_ref[j:j + 1, :] = jnp.sum(k_ref[rows, :], axis=0, keepdims=True) / float(blk)

    sub = lax.broadcasted_iota(jnp.int32, (nbp, blk), 0)
    kpos = lax.broadcasted_iota(jnp.int32, (blk, blk), 0)
    qpos = lax.broadcasted_iota(jnp.int32, (blk, blk), 1)

    def scores_t(j, w):
        return lax.dot_general(kb_ref[j], q2_ref[w], NT_DIMS, preferred_element_type=F32)

    m = []
    for w, q_ref in enumerate((qa_ref, qb_ref)):
        q = q_ref[...]
        q2_ref[w] = _mxu(q)
        gate = lax.dot_general(km_ref[...], q, NT_DIMS, precision=lax.Precision.HIGHEST, preferred_element_type=F32)
        rank = jnp.zeros((nbp, blk), jnp.int32)
        for jp in range(nb):
            other = gate[jp:jp + 1, :]
            beats = jnp.where(other > gate, 1, jnp.where(other == gate, jnp.where(jp < sub, 1, 0), 0))
            rank = rank + jnp.where(jp < own[w], beats, 0)
        sel_ref[w] = jnp.where(sub < own[w], jnp.where(rank < MOBA_TOPK, 1.0, 0.0), 0.0)
        s = jnp.where(kpos <= qpos, scores_t(own[w], w), NEG_INF)
        s_ref[nb - 1 + w] = s
        m.append(jnp.max(s, axis=0, keepdims=True))

    def past_pair(t):
        is_b = t >= i
        return is_b, jnp.where(is_b, 1, 0), jnp.where(is_b, t - i, t)

    for t in range(nb - 1):
        is_b, w, j = past_pair(t)
        s = jnp.where(sel_ref[w, pl.ds(j, 1), :] > 0.0, scores_t(j, w), NEG_INF)
        s_ref[t] = s
        mt = jnp.max(s, axis=0, keepdims=True)
        m = [jnp.where(is_b, m[0], jnp.maximum(m[0], mt)), jnp.where(is_b, jnp.maximum(m[1], mt), m[1])]

    c_exp = (HEAD_DIM ** -0.5) * float(np.log2(np.e))

    def weigh(slot, j, mq):
        p = jnp.exp2((s_ref[slot] - mq) * c_exp)
        return jnp.sum(p, axis=0, keepdims=True), jnp.dot(vt_ref[j], _mxu(p), preferred_element_type=F32)

    for w in range(2):
        l_ref[w], acc_ref[w] = weigh(nb - 1 + w, own[w], m[w])
    for t in range(nb - 1):
        is_b, w, j = past_pair(t)
        lt, at = weigh(t, j, jnp.where(is_b, m[1], m[0]))
        l_ref[w] += lt
        acc_ref[w] += at
    oa_ref[...] = (acc_ref[0] / l_ref[0]).T
    ob_ref[...] = (acc_ref[1] / l_ref[1]).T


def moba(proj, batch, seq):
    t = proj.shape[0]
    nb = seq // MOBA_BLOCK
    nbp = max(nb, V7X_SUBLANES)
    hd = HEAD_DIM
    assert nb % 2 == 0
    nh = nb // 2
    out = jax.ShapeDtypeStruct((t // 2, MOBA_HEADS * hd), F32)
    out_spec = pl.BlockSpec((MOBA_BLOCK, hd), lambda b, h, i: (b * nh + i, h))
    return pl.pallas_call(
        functools.partial(_moba_kernel, nb=nb, nbp=nbp),
        out_shape=(out, out),
        grid=(batch, MOBA_HEADS, nh),
        in_specs=[
            pl.BlockSpec((MOBA_BLOCK, hd), lambda b, h, i: (b * nb + i, h)),
            pl.BlockSpec((MOBA_BLOCK, hd), lambda b, h, i: (b * nb + nb - 1 - i, h)),
            pl.BlockSpec((seq, hd), lambda b, h, i: (b, MOBA_HEADS + h)),
            pl.BlockSpec((seq, hd), lambda b, h, i: (b, 2 * MOBA_HEADS + h)),
        ],
        out_specs=(out_spec, out_spec),
        scratch_shapes=[
            pltpu.VMEM((nb, MOBA_BLOCK, hd), MXU_DTYPE),
            pltpu.VMEM((nb, hd, MOBA_BLOCK), MXU_DTYPE),
            pltpu.VMEM((nbp, hd), F32),
            pltpu.VMEM((2, MOBA_BLOCK, hd), MXU_DTYPE),
            pltpu.VMEM((2, nbp, MOBA_BLOCK), F32),
            pltpu.VMEM((nb + 1, MOBA_BLOCK, MOBA_BLOCK), F32),
            pltpu.VMEM((2, hd, MOBA_BLOCK), F32),
            pltpu.VMEM((2, 1, MOBA_BLOCK), F32),
        ],
        compiler_params=_params(3),
        name="moba",
    )(proj, proj, proj, proj)


def _retention_kernel(q_ref, k_ref, v_ref, g_ref, dm_ref, xi_ref, ze_ref, gc_ref, o_ref, st_ref):
    c = pl.program_id(1)

    @pl.when(c == 0)
    def _():
        st_ref[...] = jnp.zeros_like(st_ref)

    for h in range(RET_HEADS):
        sl = slice(h * HEAD_DIM, (h + 1) * HEAD_DIM)
        q = _mxu(q_ref[:, sl])
        k = k_ref[:, sl]
        v = _mxu(v_ref[:, sl])
        st = st_ref[h]
        inner = lax.dot_general(q, _mxu(k), NT_DIMS, preferred_element_type=F32) * dm_ref[h]
        o = (jnp.dot(_mxu(inner), v, preferred_element_type=F32)
             + jnp.dot(q, _mxu(st), preferred_element_type=F32) * xi_ref[h])
        st_ref[h] = st * gc_ref[h] + lax.dot_general(_mxu(k * ze_ref[h]), v, TN_DIMS, preferred_element_type=F32)
        o = o * lax.rsqrt(jnp.mean(o * o, axis=-1, keepdims=True) + EPS)
        gate = g_ref[:, sl]
        o_ref[:, sl] = o * (gate * jax.nn.sigmoid(gate))


def retention(proj, tables, batch, seq):
    t = proj.shape[0]
    nc = seq // RET_CHUNK
    w = RET_HEADS * HEAD_DIM
    base = 3 * MOBA_HEADS * HEAD_DIM // w
    tab_spec = pl.BlockSpec((RET_HEADS, RET_CHUNK, HEAD_DIM), lambda b, c: (0, 0, 0))
    return pl.pallas_call(
        _retention_kernel,
        out_shape=jax.ShapeDtypeStruct((t, w), F32),
        grid=(batch, nc),
        in_specs=[
            pl.BlockSpec((RET_CHUNK, w), lambda b, c: (b * nc + c, base)),
            pl.BlockSpec((RET_CHUNK, w), lambda b, c: (b * nc + c, base + 1)),
            pl.BlockSpec((RET_CHUNK, w), lambda b, c: (b * nc + c, base + 2)),
            pl.BlockSpec((RET_CHUNK, w), lambda b, c: (b * nc + c, base + 3)),
            tab_spec, tab_spec, tab_spec, tab_spec,
        ],
        out_specs=pl.BlockSpec((RET_CHUNK, w), lambda b, c: (b * nc + c, 0)),
        scratch_shapes=[pltpu.VMEM((RET_HEADS, HEAD_DIM, HEAD_DIM), F32)],
        compiler_params=_params(2),
        name="retention",
    )(proj, proj, proj, proj, *tables)


def _retention_tables():
    c = RET_CHUNK
    log_g = jnp.log(1.0 - 2.0 ** (-5.0 - jnp.arange(RET_HEADS, dtype=F32)))
    idx = jnp.arange(c, dtype=F32)
    diff = idx[:, None] - idx[None, :]
    dmask = jnp.where(diff >= 0, jnp.exp(log_g[:, None, None] * jnp.maximum(diff, 0.0)), 0.0)
    xi = jnp.exp(log_g[:, None] * (idx + 1.0))[..., None]
    zeta = jnp.exp(log_g[:, None] * (c - 1.0 - idx))[..., None]
    gc = jnp.exp(log_g * c)[:, None, None]
    full = (RET_HEADS, c, HEAD_DIM)
    return (dmask, jnp.broadcast_to(xi, full), jnp.broadcast_to(zeta, full), jnp.broadcast_to(gc, full))


def _even_out_kernel(h_ref, lo_ref, hi_ref, r_ref, w_ref, o_ref, *, nb):
    in_lo = (pl.program_id(0) % nb) < nb // 2
    mo = jnp.where(in_lo, lo_ref[...], hi_ref[...])
    y = jnp.concatenate([_mxu(mo), _mxu(r_ref[...])], axis=1)
    o_ref[...] = h_ref[...] + jnp.dot(y, w_ref[...], preferred_element_type=F32)


def even_out(h, mo_lo, mo_hi, ro, w, seq):
    t, d = h.shape
    tm = MOBA_BLOCK
    nb = seq // tm
    nh = nb // 2

    def lo_map(g):
        return ((g // nb) * nh + jnp.minimum(g % nb, nh - 1), 0)

    def hi_map(g):
        return ((g // nb) * nh + (nb - 1 - jnp.maximum(g % nb, nh)), 0)

    return pl.pallas_call(
        functools.partial(_even_out_kernel, nb=nb),
        out_shape=jax.ShapeDtypeStruct((t, d), F32),
        grid=(t // tm,),
        in_specs=[
            pl.BlockSpec((tm, d), lambda g: (g, 0)),
            pl.BlockSpec((tm, mo_lo.shape[1]), lo_map),
            pl.BlockSpec((tm, mo_hi.shape[1]), hi_map),
            pl.BlockSpec((tm, ro.shape[1]), lambda g: (g, 0)),
            pl.BlockSpec(w.shape, lambda g: (0, 0)),
        ],
        out_specs=pl.BlockSpec((tm, d), lambda g: (g, 0)),
        compiler_params=_params(1),
        name="even_out",
    )(h, mo_lo, mo_hi, ro, w)


def _conv_out_kernel(h_ref, bg_ref, cg_ref, hx_ref, cgp_ref, hxp_ref, cw_ref, w_ref, o_ref, *, per_seq):
    i = pl.program_id(0)
    u = cg_ref[...] * hx_ref[...]
    up = jnp.where(i % per_seq == 0, 0.0, cgp_ref[...] * hxp_ref[...])
    row = lax.broadcasted_iota(jnp.int32, u.shape, 0)
    p1 = up[V7X_SUBLANES - 1:V7X_SUBLANES, :]
    p2 = up[V7X_SUBLANES - 2:V7X_SUBLANES - 1, :]
    u1 = jnp.where(row == 0, p1, pltpu.roll(u, 1, axis=0))
    u2 = jnp.where(row == 0, p2, jnp.where(row == 1, p1, pltpu.roll(u, 2, axis=0)))
    y = cw_ref[0:1, :] * u2 + cw_ref[1:2, :] * u1 + cw_ref[2:3, :] * u
    z = _mxu(bg_ref[...] * y)
    o_ref[...] = h_ref[...] + jnp.dot(z, w_ref[...], preferred_element_type=F32)


def conv_out(h, proj, conv_w, w, seq):
    t, d = h.shape
    tm = CONV_TM
    per_seq = seq // tm
    halo = tm // V7X_SUBLANES
    return pl.pallas_call(
        functools.partial(_conv_out_kernel, per_seq=per_seq),
        out_shape=jax.ShapeDtypeStruct((t, d), F32),
        grid=(t // tm,),
        in_specs=[
            pl.BlockSpec((tm, d), lambda i: (i, 0)),
            pl.BlockSpec((tm, d), lambda i: (i, 0)),
            pl.BlockSpec((tm, d), lambda i: (i, 1)),
            pl.BlockSpec((tm, d), lambda i: (i, 2)),
            pl.BlockSpec((V7X_SUBLANES, d), lambda i: (jnp.maximum(i * halo - 1, 0), 1)),
            pl.BlockSpec((V7X_SUBLANES, d), lambda i: (jnp.maximum(i * halo - 1, 0), 2)),
            pl.BlockSpec(conv_w.shape, lambda i: (0, 0)),
            pl.BlockSpec(w.shape, lambda i: (0, 0)),
        ],
        out_specs=pl.BlockSpec((tm, d), lambda i: (i, 0)),
        compiler_params=_params(1),
        name="conv_out",
    )(h, proj, proj, proj, proj, proj, conv_w, w)


def _peer_scores_kernel(h_ref, g_ref, wq_ref, keys_ref, xn_ref, sc_ref):
    xn = _mxu(_rms(h_ref[...], g_ref[...]))
    xn_ref[...] = xn
    n_hp = keys_ref.shape[0]
    per = 512 // PEER_HALF
    for c in range(n_hp // per):
        q = jnp.dot(xn, wq_ref[:, c * 512:(c + 1) * 512], preferred_element_type=F32)
        for u in range(per):
            hp = c * per + u
            qs = _mxu(q[:, u * PEER_HALF:(u + 1) * PEER_HALF])
            sc = lax.dot_general(keys_ref[hp], qs, NT_DIMS, preferred_element_type=F32)
            for grp in range(sc_ref.shape[1]):
                sc_ref[hp, grp] = sc[:, grp * V7X_LANES:(grp + 1) * V7X_LANES]


def peer_scores(h, g, wq, keys):
    t, d = h.shape
    tm = PEER_Q_TM
    n_hp = keys.shape[0]
    grp = tm // V7X_LANES
    return pl.pallas_call(
        _peer_scores_kernel,
        out_shape=(jax.ShapeDtypeStruct((t, d), MXU_DTYPE),
                   jax.ShapeDtypeStruct((n_hp, t // V7X_LANES, PEER_NKEYS, V7X_LANES), F32)),
        grid=(t // tm,),
        in_specs=[
            pl.BlockSpec((tm, d), lambda i: (i, 0)),
            pl.BlockSpec((1, d), lambda i: (0, 0)),
            pl.BlockSpec(wq.shape, lambda i: (0, 0)),
            pl.BlockSpec(keys.shape, lambda i: (0, 0, 0)),
        ],
        out_specs=(pl.BlockSpec((tm, d), lambda i: (i, 0)),
                   pl.BlockSpec((n_hp, grp, PEER_NKEYS, V7X_LANES), lambda i: (0, i, 0, 0))),
        compiler_params=_params(1),
        name="peer_scores",
    )(h, g.reshape(1, d), wq, keys)


def _topk_rows(x, pos, k):
    big = jnp.int32(2**30)
    vals, ids = [], []
    for _ in range(k):
        m = jnp.max(x, axis=0, keepdims=True)
        sel = jnp.min(jnp.where(x == m, pos, big), axis=0, keepdims=True)
        vals.append(m)
        ids.append(sel)
        x = jnp.where(pos == sel, NEG_INF, x)
    return jnp.concatenate(vals, axis=0), jnp.concatenate(ids, axis=0)


def _lookup_rows(table, idx):
    out = jnp.zeros(idx.shape, table.dtype)
    for a in range(table.shape[0]):
        out = jnp.where(idx == a, table[a:a + 1, :], out)
    return out


def _topk_halves(sc_ref, h, grp, hv_ref, hn_ref, halves=(0, 1)):
    key_pos = lax.broadcasted_iota(jnp.int32, (PEER_NKEYS, V7X_LANES), 0)
    for p in halves:
        hv_ref[p], hn_ref[p] = _topk_rows(sc_ref[2 * h + p, grp], key_pos, PEER_TOPK)


def _topk_pairs(hv_ref, hn_ref, h, grp, l1_ref, l2_ref, lg_ref):
    k = PEER_TOPK
    n = V7X_LANES
    r16 = lax.broadcasted_iota(jnp.int32, (k, n), 0)
    r8 = lax.broadcasted_iota(jnp.int32, (V7X_SUBLANES, n), 0)
    cand_pos = [r16]
    for a in range(1, V7X_SUBLANES):
        cand_pos.append(a * k + r8)
    cand_pos.append((V7X_SUBLANES + r8) * k)
    cand_pos = jnp.concatenate(cand_pos, axis=0)

    v1, n1, v2, n2 = hv_ref[0], hn_ref[0], hv_ref[1], hn_ref[1]
    cand = [v1[0:1, :] + v2]
    for a in range(1, V7X_SUBLANES):
        piece = v1[a:a + 1, :] + v2[0:V7X_SUBLANES, :]
        cand.append(jnp.where(r8 < k // (a + 1), piece, NEG_INF))
    cand.append(v1[V7X_SUBLANES:, :] + v2[0:1, :])
    top_s, top_pos = _topk_rows(jnp.concatenate(cand, axis=0), cand_pos, k)
    e1 = _lookup_rows(n1, lax.shift_right_logical(top_pos, k.bit_length() - 1))
    e2 = _lookup_rows(n2, top_pos & (k - 1))
    ex = jnp.exp(top_s - top_s[0:1, :])
    gate = ex / jnp.sum(ex, axis=0, keepdims=True)
    rows = pl.ds(pl.multiple_of(h * k, k), k)
    l1_ref[grp, rows, :] = e1
    l2_ref[grp, rows, :] = e2
    lg_ref[grp, rows, :] = gate


def _transpose_lists(src_refs, dst_refs):
    for src, dst in zip(src_refs, dst_refs):
        for grp in range(src.shape[0]):
            dst[grp * V7X_LANES:(grp + 1) * V7X_LANES, :] = src[grp].T


def _peer_topk_kernel(sc_ref, i1_ref, i2_ref, g_ref, l1_ref, l2_ref, lg_ref, hv_ref, hn_ref):
    n_heads = sc_ref.shape[0] // 2

    def unit(n, carry):
        h, grp = n % n_heads, n // n_heads
        _topk_halves(sc_ref, h, grp, hv_ref, hn_ref)
        _topk_pairs(hv_ref, hn_ref, h, grp, l1_ref, l2_ref, lg_ref)
        return carry

    lax.fori_loop(0, n_heads * sc_ref.shape[1], unit, 0)
    _transpose_lists((l1_ref, l2_ref, lg_ref), (i1_ref, i2_ref, g_ref))


def _topk_scratch(n_groups, nj, slots=()):
    lists = (n_groups, nj, V7X_LANES)
    halves = (*slots, 2, PEER_TOPK, V7X_LANES)
    return [pltpu.VMEM(lists, jnp.int32), pltpu.VMEM(lists, jnp.int32), pltpu.VMEM(lists, F32),
            pltpu.VMEM(halves, F32), pltpu.VMEM(halves, jnp.int32)]


def peer_topk(sc, n_tokens):
    n_hp, _, nk, _ = sc.shape
    grp = n_tokens // V7X_LANES
    nj = (n_hp // 2) * PEER_TOPK
    out_spec = pl.BlockSpec((n_tokens, nj), lambda i: (0, 0))
    return pl.pallas_call(
        _peer_topk_kernel,
        out_shape=(jax.ShapeDtypeStruct((n_tokens, nj), jnp.int32), jax.ShapeDtypeStruct((n_tokens, nj), jnp.int32),
                   jax.ShapeDtypeStruct((n_tokens, nj), F32)),
        grid=(1,),
        in_specs=[pl.BlockSpec((n_hp, grp, nk, V7X_LANES), lambda i: (0, 0, 0, 0))],
        out_specs=(out_spec, out_spec, out_spec),
        scratch_shapes=_topk_scratch(grp, nj),
        compiler_params=_params(1),
        name="peer_topk",
    )(sc)


def _peer_mix_kernel(h_ref, xn_ref, f1_ref, f2_ref, fg_ref, sc_ref, ua_ref, ub_ref, va_ref, vb_ref, o_ref, w_ref,
                     ha_ref, hb_ref, acc_ref, i1_ref, i2_ref, g_ref, l1_ref, l2_ref, lg_ref, hv_ref, hn_ref, *, unroll,
                     n_chunks, units_per_step):
    tile = pl.program_id(0)
    c = pl.program_id(1)
    last = pl.num_programs(1) - 1
    tm = xn_ref.shape[0]
    nk = PEER_NKEYS
    half = nk // 2
    next_lists = (l1_ref, l2_ref, lg_ref)
    lists = (i1_ref, i2_ref, g_ref)

    @pl.when((c == 0) & (tile == 0))
    def _():
        for src, dst in zip((f1_ref, f2_ref, fg_ref), lists):
            dst[...] = src[...]
        hv_ref[...] = jnp.zeros_like(hv_ref)
        hn_ref[...] = jnp.zeros_like(hn_ref)

    @pl.when((c == 0) & (tile > 0))
    def _():
        _transpose_lists(next_lists, lists)

    @pl.when(c == 0)
    def _():
        acc_ref[...] = jnp.zeros_like(acc_ref)
        hb_ref[...] = jnp.zeros_like(hb_ref)
        r = lax.broadcasted_iota(jnp.int32, (nk, nk), 0).astype(F32).astype(jnp.bfloat16)
        one = jnp.ones((nk, nk), jnp.bfloat16)
        zero = jnp.zeros((nk, nk), jnp.bfloat16)

        def tok(t, carry):
            i1 = i1_ref[pl.ds(t, 1), :].astype(F32).astype(jnp.bfloat16)
            i2 = i2_ref[pl.ds(t, 1), :].astype(F32).astype(jnp.bfloat16)
            g = g_ref[pl.ds(t, 1), :].astype(jnp.bfloat16)
            a = jnp.where(r == i1, g, zero)
            b = jnp.where(r == i2, one, zero)
            w = lax.dot_general(a, b, NT_DIMS, preferred_element_type=F32).astype(jnp.bfloat16)
            w_ref[pl.ds(pl.multiple_of(t * W_PITCH, V7X_SUBLANES), half), :] = pltpu.bitcast(w, jnp.uint32)
            return carry

        lax.fori_loop(0, tm, tok, 0, unroll=unroll)

    def hidden(u_ref):
        return lax.dot_general(xn_ref[...], u_ref[...], NT_DIMS, preferred_element_type=F32)

    def consume(hid_ref, chunk, v_ref, valid):
        packed = w_ref[pl.ds(jnp.clip(chunk, 0, n_chunks - 1), tm, stride=W_PITCH), :]
        if valid is not None:
            packed = jnp.where(valid, packed, jnp.uint32(0))
        w_even = lax.bitcast_convert_type(packed << 16, F32)
        w_odd = lax.bitcast_convert_type(packed & jnp.uint32(0xFFFF0000), F32)
        hid = hid_ref[...]
        act = 0.5 * hid * (1.0 + lax.erf(hid * (2.0 ** -0.5)))
        p = jnp.concatenate([w_even * act[:, :nk], w_odd * act[:, nk:]], axis=1)
        acc_ref[...] += jnp.dot(_mxu(p), v_ref[...], preferred_element_type=F32)

    n_heads = sc_ref.shape[0] // 2
    n_units = n_heads * sc_ref.shape[1]

    def topk_pairs():
        for k in range(units_per_step):
            n = jnp.clip((c - 1) * units_per_step + k, 0, n_units - 1)
            _topk_pairs(hv_ref.at[k], hn_ref.at[k], n % n_heads, n // n_heads, *next_lists)

    def topk_half(p):
        for k in range(units_per_step):
            n = jnp.minimum(c * units_per_step + k, n_units - 1)
            _topk_halves(sc_ref, n % n_heads, n // n_heads, hv_ref.at[k], hn_ref.at[k], halves=(p,))

    topk_pairs()
    ha_ref[...] = hidden(ua_ref)
    topk_half(0)
    consume(hb_ref, 2 * c - 1, va_ref, None)
    hb_ref[...] = hidden(ub_ref)
    topk_half(1)
    consume(ha_ref, 2 * c, vb_ref, c < last)

    @pl.when(c == last)
    def _():
        o_ref[...] = h_ref[...] + acc_ref[...]


def peer_mix(h, xn, sc, first_lists, u, v):
    t, d = h.shape
    tm = first_lists[0].shape[0]
    nj = first_lists[0].shape[1]
    n_tiles = t // tm
    grp = tm // V7X_LANES
    n_hp = sc.shape[0]
    n_chunks = u.shape[0] // PEER_ECHUNK
    steps = n_chunks // 2 + 1
    n_units = (n_hp // 2) * grp
    units_per_step = -(-n_units // (steps - 1))
    tok_spec = pl.BlockSpec((tm, d), lambda i, c: (i, 0))
    first_spec = pl.BlockSpec((tm, nj), lambda i, c: (0, 0))
    sc_spec = pl.BlockSpec((n_hp, grp, PEER_NKEYS, V7X_LANES), lambda i, c: (0, jnp.minimum(i + 1, n_tiles - 1), 0, 0))

    def exp_spec(offset):
        return pl.BlockSpec((PEER_ECHUNK, d), lambda i, c: (jnp.clip(2 * c + offset, 0, n_chunks - 1), 0))

    return pl.pallas_call(
        functools.partial(_peer_mix_kernel, unroll=64, n_chunks=n_chunks, units_per_step=units_per_step),
        out_shape=jax.ShapeDtypeStruct((t, d), F32),
        grid=(n_tiles, steps),
        in_specs=[tok_spec, tok_spec, first_spec, first_spec, first_spec, sc_spec, exp_spec(0), exp_spec(1),
                  exp_spec(-1), exp_spec(0)],
        out_specs=tok_spec,
        scratch_shapes=[
            pltpu.VMEM((tm * W_PITCH, PEER_NKEYS), jnp.uint32),
            pltpu.VMEM((tm, PEER_ECHUNK), F32),
            pltpu.VMEM((tm, PEER_ECHUNK), F32),
            pltpu.VMEM((tm, d), F32),
            pltpu.VMEM((tm, nj), jnp.int32),
            pltpu.VMEM((tm, nj), jnp.int32),
            pltpu.VMEM((tm, nj), F32),
            *_topk_scratch(grp, nj, slots=(units_per_step,)),
        ],
        compiler_params=_params(2),
        name="peer_mix",
    )(h, xn, *first_lists, sc, u, u, v, v)


def _final_norm_kernel(h_ref, g_ref, o_ref):
    o_ref[...] = _rms(h_ref[...], g_ref[...])


def final_rmsnorm(h, g):
    t, d = h.shape
    tm = OUT_TM
    return pl.pallas_call(
        _final_norm_kernel,
        out_shape=jax.ShapeDtypeStruct((t, d), F32),
        grid=(t // tm,),
        in_specs=[pl.BlockSpec((tm, d), lambda i: (i, 0)), pl.BlockSpec((1, d), lambda i: (0, 0))],
        out_specs=pl.BlockSpec((tm, d), lambda i: (i, 0)),
        compiler_params=_params(1),
        name="final_norm",
    )(h, g.reshape(1, d))


def _rope_tables(seq):
    half = HEAD_DIM // 2
    inv = ROPE_THETA ** (-jnp.arange(half, dtype=F32) / half)
    ang = jnp.arange(seq, dtype=F32)[:, None] * inv[None, :]
    cos, sin = jnp.cos(ang), jnp.sin(ang)
    return jnp.concatenate([cos, cos], axis=1), jnp.concatenate([-sin, sin], axis=1)


def kernel(x, norm_mix, norm_ffn, even_w_in, even_w_out, odd_w_in, odd_conv, odd_w_out, peer_w_q, peer_sub_keys,
           peer_u, peer_v, final_norm):
    batch, seq, d = x.shape
    depth = norm_mix.shape[0]
    cos_t, sin_t = _rope_tables(seq)
    ret_tables = _retention_tables()
    mh, rh = MOBA_HEADS, RET_HEADS
    rope_blocks = tuple(range(0, 2 * mh)) + tuple(range(3 * mh, 3 * mh + 2 * rh))
    ret_k_blocks = tuple(range(3 * mh + rh, 3 * mh + 2 * rh))

    h = x.reshape(batch * seq, d)
    for layer in range(depth):
        i = layer // 2
        if layer % 2 == 0:
            proj = norm_proj(h, norm_mix[layer], _mxu(even_w_in[i]), cos_t, sin_t, seq, rope_blocks, ret_k_blocks,
                             HEAD_DIM ** -0.5)
            mo_lo, mo_hi = moba(proj, batch, seq)
            ro = retention(proj, ret_tables, batch, seq)
            h = even_out(h, mo_lo, mo_hi, ro, _mxu(even_w_out[i]), seq)
        else:
            proj = norm_proj(h, norm_mix[layer], _mxu(odd_w_in[i]), cos_t, sin_t, seq)
            h = conv_out(h, proj, odd_conv[i], _mxu(odd_w_out[i]), seq)
        keys = _mxu(peer_sub_keys[layer].reshape(2 * PEER_HEADS, PEER_NKEYS, PEER_HALF))
        xn, sc = peer_scores(h, norm_ffn[layer], _mxu(peer_w_q[layer]), keys)
        first_lists = peer_topk(sc, min(PEER_TM, h.shape[0]))
        h = peer_mix(h, xn, sc, first_lists, _mxu(peer_u[layer]), _mxu(peer_v[layer]))
    return final_rmsnorm(h, final_norm).reshape(batch, seq, d)
```

```python
import functools

import jax
import jax.numpy as jnp
import numpy as np
from jax import lax
from jax.experimental import pallas as pl
from jax.experimental.pallas import tpu as pltpu

HEAD_DIM = 128
MOBA_HEADS = 4
RET_HEADS = 4
MOBA_BLOCK = 256
MOBA_TOPK = 3
RET_CHUNK = 128
PEER_HEADS = 8
PEER_NKEYS = 128
PEER_TOPK = 16
PEER_HALF = 128
ROPE_THETA = 10000.0
EPS = 1e-6

V7X_LANES = 128
V7X_SUBLANES = 8
V7X_VMEM_BYTES = 64 * 2**20
VMEM_LIMIT = V7X_VMEM_BYTES - 8 * 2**20

MXU_DTYPE = jnp.bfloat16
F32 = jnp.float32
NEG_INF = float("-inf")
NT_DIMS = (((1,), (1,)), ((), ()))
TN_DIMS = (((0,), (0,)), ((), ()))

PROJ_TM = 256
OUT_TM = 512
CONV_TM = 256
PEER_Q_TM = 256
PEER_TOPK_TL = 256
PEER_TM = 512
PEER_ECHUNK = 2 * PEER_NKEYS
W_PITCH = 72


def _params(n_axes, flags=None):
    return pltpu.CompilerParams(dimension_semantics=("arbitrary",) * n_axes, vmem_limit_bytes=VMEM_LIMIT,
                                flags=flags)


def _mxu(x):
    return x.astype(MXU_DTYPE)


def _rms(x, g):
    return x * lax.rsqrt(jnp.mean(x * x, axis=-1, keepdims=True) + EPS) * g


def _norm_proj_kernel(h_ref, g_ref, w_ref, cos_ref, sin_ref, o_ref, *, rope_blocks, scale_blocks, scale, chunk):
    xn = _mxu(_rms(h_ref[...], g_ref[...]))
    n = o_ref.shape[1]
    for c0 in range(0, n, chunk):
        y = jnp.dot(xn, w_ref[:, c0:c0 + chunk], preferred_element_type=F32)
        for u in range(chunk // HEAD_DIM):
            blk = (c0 // HEAD_DIM) + u
            t = y[:, u * HEAD_DIM:(u + 1) * HEAD_DIM]
            if blk in rope_blocks:
                t = t * cos_ref[...] + pltpu.roll(t, HEAD_DIM // 2, axis=1) * sin_ref[...]
            if blk in scale_blocks:
                t = t * scale
            o_ref[:, blk * HEAD_DIM:(blk + 1) * HEAD_DIM] = t


def norm_proj(h, g, w, cos_t, sin_t, seq, rope_blocks=(), scale_blocks=(), scale=1.0):
    t, d = h.shape
    n = w.shape[1]
    tm = PROJ_TM
    per_seq = seq // tm
    return pl.pallas_call(
        functools.partial(_norm_proj_kernel, rope_blocks=frozenset(rope_blocks), scale_blocks=frozenset(scale_blocks),
                          scale=scale, chunk=512),
        out_shape=jax.ShapeDtypeStruct((t, n), F32),
        grid=(t // tm,),
        in_specs=[
            pl.BlockSpec((tm, d), lambda i: (i, 0)),
            pl.BlockSpec((1, d), lambda i: (0, 0)),
            pl.BlockSpec((d, n), lambda i: (0, 0)),
            pl.BlockSpec((tm, HEAD_DIM), lambda i: (i % per_seq, 0)),
            pl.BlockSpec((tm, HEAD_DIM), lambda i: (i % per_seq, 0)),
        ],
        out_specs=pl.BlockSpec((tm, n), lambda i: (i, 0)),
        compiler_params=_params(1),
        name="norm_proj",
    )(h, g.reshape(1, d), w, cos_t, sin_t)


def _moba_kernel(qa_ref, qb_ref, k_ref, v_ref, oa_ref, ob_ref, kb_ref, vt_ref, km_ref, q2_ref, sel_ref, s_ref,
                 acc_ref, l_ref, *, nb, nbp):
    i = pl.program_id(2)
    blk = MOBA_BLOCK
    own = (i, nb - 1 - i)

    @pl.when(i == 0)
    def _():
        km_ref[...] = jnp.zeros_like(km_ref)
        for j in range(nb):
            rows = slice(j * blk, (j + 1) * blk)
            kb_ref[j] = _mxu(k_ref[rows, :])
            vt_ref[j] = _mxu(v_ref[rows, :].T)
            km_ref[j:j + 1, :] = jnp.sum(k_ref[rows, :], axis=0, keepdims=True) / float(blk)

    sub = lax.broadcasted_iota(jnp.int32, (nbp, blk), 0)
    kpos = lax.broadcasted_iota(jnp.int32, (blk, blk), 0)
    qpos = lax.broadcasted_iota(jnp.int32, (blk, blk), 1)

    def scores_t(j, w):
        return lax.dot_general(kb_ref[j], q2_ref[w], NT_DIMS, preferred_element_type=F32)

    m = []
    for w, q_ref in enumerate((qa_ref, qb_ref)):
        q = q_ref[...]
        q2_ref[w] = _mxu(q)
        gate = lax.dot_general(km_ref[...], q, NT_DIMS, precision=lax.Precision.HIGHEST, preferred_element_type=F32)
        rank = jnp.zeros((nbp, blk), jnp.int32)
        for jp in range(nb):
            other = gate[jp:jp + 1, :]
            beats = jnp.where(other > gate, 1, jnp.where(other == gate, jnp.where(jp < sub, 1, 0), 0))
            rank = rank + jnp.where(jp < own[w], beats, 0)
        sel_ref[w] = jnp.where(sub < own[w], jnp.where(rank < MOBA_TOPK, 1.0, 0.0), 0.0)
        s = jnp.where(kpos <= qpos, scores_t(own[w], w), NEG_INF)
        s_ref[nb - 1 + w] = s
        m.append(jnp.max(s, axis=0, keepdims=True))

    def past_pair(t):
        is_b = t >= i
        return is_b, jnp.where(is_b, 1, 0), jnp.where(is_b, t - i, t)

    for t in range(nb - 1):
        is_b, w, j = past_pair(t)
        s = jnp.where(sel_ref[w, pl.ds(j, 1), :] > 0.0, scores_t(j, w), NEG_INF)
        s_ref[t] = s
        mt = jnp.max(s, axis=0, keepdims=True)
        m = [jnp.where(is_b, m[0], jnp.maximum(m[0], mt)), jnp.where(is_b, jnp.maximum(m[1], mt), m[1])]

    c_exp = (HEAD_DIM ** -0.5) * float(np.log2(np.e))

    def weigh(slot, j, mq):
        p = jnp.exp2((s_ref[slot] - mq) * c_exp)
        return jnp.sum(p, axis=0, keepdims=True), jnp.dot(vt_ref[j], _mxu(p), preferred_element_type=F32)

    for w in range(2):
        l_ref[w], acc_ref[w] = weigh(nb - 1 + w, own[w], m[w])
    for t in range(nb - 1):
        is_b, w, j = past_pair(t)
        lt, at = weigh(t, j, jnp.where(is_b, m[1], m[0]))
        l_ref[w] += lt
        acc_ref[w] += at
    oa_ref[...] = (acc_ref[0] / l_ref[0]).T
    ob_ref[...] = (acc_ref[1] / l_ref[1]).T


def moba(proj, batch, seq):
    t = proj.shape[0]
    nb = seq // MOBA_BLOCK
    nbp = max(nb, V7X_SUBLANES)
    hd = HEAD_DIM
    assert nb % 2 == 0
    nh = nb // 2
    out = jax.ShapeDtypeStruct((t // 2, MOBA_HEADS * hd), F32)
    out_spec = pl.BlockSpec((MOBA_BLOCK, hd), lambda b, h, i: (b * nh + i, h))
    return pl.pallas_call(
        functools.partial(_moba_kernel, nb=nb, nbp=nbp),
        out_shape=(out, out),
        grid=(batch, MOBA_HEADS, nh),
        in_specs=[
            pl.BlockSpec((MOBA_BLOCK, hd), lambda b, h, i: (b * nb + i, h)),
            pl.BlockSpec((MOBA_BLOCK, hd), lambda b, h, i: (b * nb + nb - 1 - i, h)),
            pl.BlockSpec((seq, hd), lambda b, h, i: (b, MOBA_HEADS + h)),
            pl.BlockSpec((seq, hd), lambda b, h, i: (b, 2 * MOBA_HEADS + h)),
        ],
        out_specs=(out_spec, out_spec),
        scratch_shapes=[
            pltpu.VMEM((nb, MOBA_BLOCK, hd), MXU_DTYPE),
            pltpu.VMEM((nb, hd, MOBA_BLOCK), MXU_DTYPE),
            pltpu.VMEM((nbp, hd), F32),
            pltpu.VMEM((2, MOBA_BLOCK, hd), MXU_DTYPE),
            pltpu.VMEM((2, nbp, MOBA_BLOCK), F32),
            pltpu.VMEM((nb + 1, MOBA_BLOCK, MOBA_BLOCK), F32),
            pltpu.VMEM((2, hd, MOBA_BLOCK), F32),
            pltpu.VMEM((2, 1, MOBA_BLOCK), F32),
        ],
        compiler_params=_params(3),
        name="moba",
    )(proj, proj, proj, proj)


def _retention_kernel(q_ref, k_ref, v_ref, g_ref, dm_ref, xi_ref, ze_ref, gc_ref, o_ref, st_ref):
    c = pl.program_id(1)

    @pl.when(c == 0)
    def _():
        st_ref[...] = jnp.zeros_like(st_ref)

    for h in range(RET_HEADS):
        sl = slice(h * HEAD_DIM, (h + 1) * HEAD_DIM)
        q = _mxu(q_ref[:, sl])
        k = k_ref[:, sl]
        v = _mxu(v_ref[:, sl])
        st = st_ref[h]
        inner = lax.dot_general(q, _mxu(k), NT_DIMS, preferred_element_type=F32) * dm_ref[h]
        o = (jnp.dot(_mxu(inner), v, preferred_element_type=F32)
             + jnp.dot(q, _mxu(st), preferred_element_type=F32) * xi_ref[h])
        st_ref[h] = st * gc_ref[h] + lax.dot_general(_mxu(k * ze_ref[h]), v, TN_DIMS, preferred_element_type=F32)
        o = o * lax.rsqrt(jnp.mean(o * o, axis=-1, keepdims=True) + EPS)
        gate = g_ref[:, sl]
        o_ref[:, sl] = o * (gate * jax.nn.sigmoid(gate))


def retention(proj, tables, batch, seq):
    t = proj.shape[0]
    nc = seq // RET_CHUNK
    w = RET_HEADS * HEAD_DIM
    base = 3 * MOBA_HEADS * HEAD_DIM // w
    tab_spec = pl.BlockSpec((RET_HEADS, RET_CHUNK, HEAD_DIM), lambda b, c: (0, 0, 0))
    return pl.pallas_call(
        _retention_kernel,
        out_shape=jax.ShapeDtypeStruct((t, w), F32),
        grid=(batch, nc),
        in_specs=[
            pl.BlockSpec((RET_CHUNK, w), lambda b, c: (b * nc + c, base)),
            pl.BlockSpec((RET_CHUNK, w), lambda b, c: (b * nc + c, base + 1)),
            pl.BlockSpec((RET_CHUNK, w), lambda b, c: (b * nc + c, base + 2)),
            pl.BlockSpec((RET_CHUNK, w), lambda b, c: (b * nc + c, base + 3)),
            tab_spec, tab_spec, tab_spec, tab_spec,
        ],
        out_specs=pl.BlockSpec((RET_CHUNK, w), lambda b, c: (b * nc + c, 0)),
        scratch_shapes=[pltpu.VMEM((RET_HEADS, HEAD_DIM, HEAD_DIM), F32)],
        compiler_params=_params(2),
        name="retention",
    )(proj, proj, proj, proj, *tables)


def _retention_tables():
    c = RET_CHUNK
    log_g = jnp.log(1.0 - 2.0 ** (-5.0 - jnp.arange(RET_HEADS, dtype=F32)))
    idx = jnp.arange(c, dtype=F32)
    diff = idx[:, None] - idx[None, :]
    dmask = jnp.where(diff >= 0, jnp.exp(log_g[:, None, None] * jnp.maximum(diff, 0.0)), 0.0)
    xi = jnp.exp(log_g[:, None] * (idx + 1.0))[..., None]
    zeta = jnp.exp(log_g[:, None] * (c - 1.0 - idx))[..., None]
    gc = jnp.exp(log_g * c)[:, None, None]
    full = (RET_HEADS, c, HEAD_DIM)
    return (dmask, jnp.broadcast_to(xi, full), jnp.broadcast_to(zeta, full), jnp.broadcast_to(gc, full))


def _even_out_kernel(h_ref, lo_ref, hi_ref, r_ref, w_ref, o_ref, *, nb):
    in_lo = (pl.program_id(0) % nb) < nb // 2
    mo = jnp.where(in_lo, lo_ref[...], hi_ref[...])
    y = jnp.concatenate([_mxu(mo), _mxu(r_ref[...])], axis=1)
    o_ref[...] = h_ref[...] + jnp.dot(y, w_ref[...], preferred_element_type=F32)


def even_out(h, mo_lo, mo_hi, ro, w, seq):
    t, d = h.shape
    tm = MOBA_BLOCK
    nb = seq // tm
    nh = nb // 2

    def lo_map(g):
        return ((g // nb) * nh + jnp.minimum(g % nb, nh - 1), 0)

    def hi_map(g):
        return ((g // nb) * nh + (nb - 1 - jnp.maximum(g % nb, nh)), 0)

    return pl.pallas_call(
        functools.partial(_even_out_kernel, nb=nb),
        out_shape=jax.ShapeDtypeStruct((t, d), F32),
        grid=(t // tm,),
        in_specs=[
            pl.BlockSpec((tm, d), lambda g: (g, 0)),
            pl.BlockSpec((tm, mo_lo.shape[1]), lo_map),
            pl.BlockSpec((tm, mo_hi.shape[1]), hi_map),
            pl.BlockSpec((tm, ro.shape[1]), lambda g: (g, 0)),
            pl.BlockSpec(w.shape, lambda g: (0, 0)),
        ],
        out_specs=pl.BlockSpec((tm, d), lambda g: (g, 0)),
        compiler_params=_params(1),
        name="even_out",
    )(h, mo_lo, mo_hi, ro, w)


def _conv_out_kernel(h_ref, bg_ref, cg_ref, hx_ref, cgp_ref, hxp_ref, cw_ref, w_ref, o_ref, *, per_seq):
    i = pl.program_id(0)
    u = cg_ref[...] * hx_ref[...]
    up = jnp.where(i % per_seq == 0, 0.0, cgp_ref[...] * hxp_ref[...])
    row = lax.broadcasted_iota(jnp.int32, u.shape, 0)
    p1 = up[V7X_SUBLANES - 1:V7X_SUBLANES, :]
    p2 = up[V7X_SUBLANES - 2:V7X_SUBLANES - 1, :]
    u1 = jnp.where(row == 0, p1, pltpu.roll(u, 1, axis=0))
    u2 = jnp.where(row == 0, p2, jnp.where(row == 1, p1, pltpu.roll(u, 2, axis=0)))
    y = cw_ref[0:1, :] * u2 + cw_ref[1:2, :] * u1 + cw_ref[2:3, :] * u
    z = _mxu(bg_ref[...] * y)
    o_ref[...] = h_ref[...] + jnp.dot(z, w_ref[...], preferred_element_type=F32)


def conv_out(h, proj, conv_w, w, seq):
    t, d = h.shape
    tm = CONV_TM
    per_seq = seq // tm
    halo = tm // V7X_SUBLANES
    return pl.pallas_call(
        functools.partial(_conv_out_kernel, per_seq=per_seq),
        out_shape=jax.ShapeDtypeStruct((t, d), F32),
        grid=(t // tm,),
        in_specs=[
            pl.BlockSpec((tm, d), lambda i: (i, 0)),
            pl.BlockSpec((tm, d), lambda i: (i, 0)),
            pl.BlockSpec((tm, d), lambda i: (i, 1)),
            pl.BlockSpec((tm, d), lambda i: (i, 2)),
            pl.BlockSpec((V7X_SUBLANES, d), lambda i: (jnp.maximum(i * halo - 1, 0), 1)),
            pl.BlockSpec((V7X_SUBLANES, d), lambda i: (jnp.maximum(i * halo - 1, 0), 2)),
            pl.BlockSpec(conv_w.shape, lambda i: (0, 0)),
            pl.BlockSpec(w.shape, lambda i: (0, 0)),
        ],
        out_specs=pl.BlockSpec((tm, d), lambda i: (i, 0)),
        compiler_params=_params(1),
        name="conv_out",
    )(h, proj, proj, proj, proj, proj, conv_w, w)


def _peer_scores_kernel(h_ref, g_ref, wq_ref, keys_ref, xn_ref, sc_ref):
    xn = _mxu(_rms(h_ref[...], g_ref[...]))
    xn_ref[...] = xn
    n_hp = keys_ref.shape[0]
    per = 512 // PEER_HALF
    for c in range(n_hp // per):
        q = jnp.dot(xn, wq_ref[:, c * 512:(c + 1) * 512], preferred_element_type=F32)
        for u in range(per):
            hp = c * per + u
            qs = _mxu(q[:, u * PEER_HALF:(u + 1) * PEER_HALF])
            sc = lax.dot_general(keys_ref[hp], qs, NT_DIMS, preferred_element_type=F32)
            for grp in range(sc_ref.shape[1]):
                sc_ref[hp, grp] = sc[:, grp * V7X_LANES:(grp + 1) * V7X_LANES]


def peer_scores(h, g, wq, keys):
    t, d = h.shape
    tm = PEER_Q_TM
    n_hp = keys.shape[0]
    grp = tm // V7X_LANES
    return pl.pallas_call(
        _peer_scores_kernel,
        out_shape=(jax.ShapeDtypeStruct((t, d), MXU_DTYPE),
                   jax.ShapeDtypeStruct((n_hp, t // V7X_LANES, PEER_NKEYS, V7X_LANES), F32)),
        grid=(t // tm,),
        in_specs=[
            pl.BlockSpec((tm, d), lambda i: (i, 0)),
            pl.BlockSpec((1, d), lambda i: (0, 0)),
            pl.BlockSpec(wq.shape, lambda i: (0, 0)),
            pl.BlockSpec(keys.shape, lambda i: (0, 0, 0)),
        ],
        out_specs=(pl.BlockSpec((tm, d), lambda i: (i, 0)),
                   pl.BlockSpec((n_hp, grp, PEER_NKEYS, V7X_LANES), lambda i: (0, i, 0, 0))),
        compiler_params=_params(1),
        name="peer_scores",
    )(h, g.reshape(1, d), wq, keys)


def _extract_max(x, pos):
    m = jnp.max(x, axis=-2, keepdims=True)
    sel = jnp.min(jnp.where(x == m, pos, float(2**30)), axis=-2, keepdims=True)
    return m, sel, jnp.where(pos == sel, NEG_INF, x)


def _topk_rows(x, pos, k):
    vals, ids = [], []
    for _ in range(k):
        m, sel, x = _extract_max(x, pos)
        vals.append(m)
        ids.append(sel)
    return jnp.concatenate(vals, axis=-2), jnp.concatenate(ids, axis=-2)


def _lookup_rows(table, idx):
    out = jnp.zeros(idx.shape, table.dtype)
    for a in range(table.shape[0]):
        out = jnp.where(idx == a, table[a:a + 1, :], out)
    return out


def _key_positions():
    return lax.broadcasted_iota(jnp.int32, (PEER_NKEYS, V7X_LANES), 0).astype(F32)


def _pair_candidates(v1, v2, rows):
    k = PEER_TOPK
    n = V7X_LANES
    r16 = lax.broadcasted_iota(jnp.int32, (k, n), 0)
    r8 = lax.broadcasted_iota(jnp.int32, (V7X_SUBLANES, n), 0)
    cand, pos = [v1[0:1, :] + v2], [r16]
    for a in range(1, V7X_SUBLANES):
        piece = v1[a:a + 1, :] + v2[0:V7X_SUBLANES, :]
        cand.append(jnp.where(r8 < k // (a + 1), piece, NEG_INF))
        pos.append(a * k + r8)
    cand.append(v1[V7X_SUBLANES:, :] + v2[0:1, :])
    pos.append((V7X_SUBLANES + r8) * k)
    pad = rows - sum(c.shape[0] for c in cand)
    if pad > 0:
        cand.append(jnp.full((pad, n), NEG_INF, F32))
        pos.append(jnp.zeros((pad, n), jnp.int32))
    return jnp.concatenate(cand, axis=0), jnp.concatenate(pos, axis=0).astype(F32)


def _emit_lists(top_s, top_pos, n1, n2, h, grp, l1_ref, l2_ref, lg_ref):
    k = PEER_TOPK
    top_pos = top_pos.astype(jnp.int32)
    e1 = _lookup_rows(n1, lax.shift_right_logical(top_pos, k.bit_length() - 1))
    e2 = _lookup_rows(n2, top_pos & (k - 1))
    ex = jnp.exp(top_s - top_s[0:1, :])
    gate = ex / jnp.sum(ex, axis=0, keepdims=True)
    rows = pl.ds(pl.multiple_of(h * k, k), k)
    l1_ref[grp, rows, :] = e1.astype(jnp.int32)
    l2_ref[grp, rows, :] = e2.astype(jnp.int32)
    lg_ref[grp, rows, :] = gate


def _transpose_lists(src_refs, dst_refs):
    for src, dst in zip(src_refs, dst_refs):
        for grp in range(src.shape[0]):
            dst[grp * V7X_LANES:(grp + 1) * V7X_LANES, :] = src[grp].T


def _peer_topk_kernel(sc_ref, i1_ref, i2_ref, g_ref, l1_ref, l2_ref, lg_ref, hv_ref, hn_ref):
    n_heads = sc_ref.shape[0] // 2

    def unit(n, carry):
        h, grp = n % n_heads, n // n_heads
        x = jnp.stack([sc_ref[2 * h, grp], sc_ref[2 * h + 1, grp]])
        hv_ref[...], hn_ref[...] = _topk_rows(x, _key_positions(), PEER_TOPK)
        cand, cand_pos = _pair_candidates(hv_ref[0], hv_ref[1], 0)
        top_s, top_pos = _topk_rows(cand, cand_pos, PEER_TOPK)
        _emit_lists(top_s, top_pos, hn_ref[0], hn_ref[1], h, grp, l1_ref, l2_ref, lg_ref)
        return carry

    lax.fori_loop(0, n_heads * sc_ref.shape[1], unit, 0)
    _transpose_lists((l1_ref, l2_ref, lg_ref), (i1_ref, i2_ref, g_ref))


def _topk_scratch(n_groups, nj, slots=()):
    lists = (n_groups, nj, V7X_LANES)
    halves = (*slots, 2, PEER_TOPK, V7X_LANES)
    return [pltpu.VMEM(lists, jnp.int32), pltpu.VMEM(lists, jnp.int32), pltpu.VMEM(lists, F32),
            pltpu.VMEM(halves, F32), pltpu.VMEM(halves, F32)]


def peer_topk(sc, n_tokens):
    n_hp, _, nk, _ = sc.shape
    grp = n_tokens // V7X_LANES
    nj = (n_hp // 2) * PEER_TOPK
    out_spec = pl.BlockSpec((n_tokens, nj), lambda i: (0, 0))
    return pl.pallas_call(
        _peer_topk_kernel,
        out_shape=(jax.ShapeDtypeStruct((n_tokens, nj), jnp.int32), jax.ShapeDtypeStruct((n_tokens, nj), jnp.int32),
                   jax.ShapeDtypeStruct((n_tokens, nj), F32)),
        grid=(1,),
        in_specs=[pl.BlockSpec((n_hp, grp, nk, V7X_LANES), lambda i: (0, 0, 0, 0))],
        out_specs=(out_spec, out_spec, out_spec),
        scratch_shapes=_topk_scratch(grp, nj),
        compiler_params=_params(1),
        name="peer_topk",
    )(sc)


def _peer_mix_kernel(h_ref, xn_ref, f1_ref, f2_ref, fg_ref, sc_ref, ua_ref, ub_ref, va_ref, vb_ref, o_ref, w_ref,
                     ha_ref, hb_ref, acc_ref, i1_ref, i2_ref, g_ref, l1_ref, l2_ref, lg_ref, hv_ref, hn_ref, *, unroll,
                     n_chunks, units_per_step):
    tile = pl.program_id(0)
    c = pl.program_id(1)
    last = pl.num_programs(1) - 1
    tm = xn_ref.shape[0]
    nk = PEER_NKEYS
    half = nk // 2
    next_lists = (l1_ref, l2_ref, lg_ref)
    lists = (i1_ref, i2_ref, g_ref)

    @pl.when((c == 0) & (tile == 0))
    def _():
        for src, dst in zip((f1_ref, f2_ref, fg_ref), lists):
            dst[...] = src[...]
        hv_ref[...] = jnp.zeros_like(hv_ref)
        hn_ref[...] = jnp.zeros_like(hn_ref)

    @pl.when((c == 0) & (tile > 0))
    def _():
        _transpose_lists(next_lists, lists)

    @pl.when(c == 0)
    def _():
        acc_ref[...] = jnp.zeros_like(acc_ref)
        hb_ref[...] = jnp.zeros_like(hb_ref)
        r = lax.broadcasted_iota(jnp.int32, (nk, nk), 0).astype(F32).astype(jnp.bfloat16)
        one = jnp.ones((nk, nk), jnp.bfloat16)
        zero = jnp.zeros((nk, nk), jnp.bfloat16)

        def tok(t, carry):
            i1 = i1_ref[pl.ds(t, 1), :].astype(F32).astype(jnp.bfloat16)
            i2 = i2_ref[pl.ds(t, 1), :].astype(F32).astype(jnp.bfloat16)
            g = (0.5 * g_ref[pl.ds(t, 1), :]).astype(jnp.bfloat16)
            a = jnp.where(r == i1, g, zero)
            b = jnp.where(r == i2, one, zero)
            w = lax.dot_general(a, b, NT_DIMS, preferred_element_type=F32).astype(jnp.bfloat16)
            w_ref[pl.ds(pl.multiple_of(t * W_PITCH, V7X_SUBLANES), half), :] = pltpu.bitcast(w, jnp.uint32)
            return carry

        lax.fori_loop(0, tm, tok, 0, unroll=unroll)

    def hidden(u_ref):
        return jnp.dot(xn_ref[...], u_ref[...], preferred_element_type=F32)

    def consume(hid_ref, chunk, v_ref, valid):
        packed = w_ref[pl.ds(jnp.clip(chunk, 0, n_chunks - 1), tm, stride=W_PITCH), :]
        if valid is not None:
            packed = jnp.where(valid, packed, jnp.uint32(0))
        w_even = lax.bitcast_convert_type(packed << 16, F32)
        w_odd = lax.bitcast_convert_type(packed & jnp.uint32(0xFFFF0000), F32)
        hid = hid_ref[...]
        act = hid * (1.0 + lax.erf(hid * (2.0 ** -0.5)))
        p = jnp.concatenate([w_even * act[:, :nk], w_odd * act[:, nk:]], axis=1)
        acc_ref[...] += jnp.dot(_mxu(p), v_ref[...], preferred_element_type=F32)

    n_heads = sc_ref.shape[0] // 2
    n_units = n_heads * sc_ref.shape[1]
    xs, poss, prev = [], [], []
    for s in range(units_per_step):
        n = jnp.minimum(c * units_per_step + s, n_units - 1)
        h, grp = n % n_heads, n // n_heads
        cand, cand_pos = _pair_candidates(hv_ref[s, 0], hv_ref[s, 1], PEER_NKEYS)
        xs += [sc_ref[2 * h, grp], sc_ref[2 * h + 1, grp], cand]
        poss += [_key_positions(), _key_positions(), cand_pos]
        n = jnp.clip((c - 1) * units_per_step + s, 0, n_units - 1)
        prev.append((n % n_heads, n // n_heads, hn_ref[s, 0], hn_ref[s, 1]))
    x, pos = jnp.stack(xs), jnp.stack(poss)
    vals, ids = [], []

    def select(n_steps):
        nonlocal x
        for _ in range(n_steps):
            m, sel, x = _extract_max(x, pos)
            vals.append(m)
            ids.append(sel)

    quarter = PEER_TOPK // 4
    select(quarter)
    ha_ref[...] = hidden(ua_ref)
    select(quarter)
    consume(hb_ref, 2 * c - 1, va_ref, None)
    select(quarter)
    hb_ref[...] = hidden(ub_ref)
    select(PEER_TOPK - 3 * quarter)
    consume(ha_ref, 2 * c, vb_ref, c < last)

    vals, ids = jnp.concatenate(vals, axis=-2), jnp.concatenate(ids, axis=-2)
    for s, (h, grp, n1, n2) in enumerate(prev):
        hv_ref[s] = vals[3 * s:3 * s + 2]
        hn_ref[s] = ids[3 * s:3 * s + 2]
        _emit_lists(vals[3 * s + 2], ids[3 * s + 2], n1, n2, h, grp, *next_lists)

    @pl.when(c == last)
    def _():
        o_ref[...] = h_ref[...] + acc_ref[...]


def peer_mix(h, xn, sc, first_lists, u, v):
    t, d = h.shape
    tm = first_lists[0].shape[0]
    nj = first_lists[0].shape[1]
    n_tiles = t // tm
    grp = tm // V7X_LANES
    n_hp = sc.shape[0]
    n_chunks = v.shape[0] // PEER_ECHUNK
    steps = n_chunks // 2 + 1
    n_units = (n_hp // 2) * grp
    units_per_step = -(-n_units // (steps - 1))
    tok_spec = pl.BlockSpec((tm, d), lambda i, c: (i, 0))
    first_spec = pl.BlockSpec((tm, nj), lambda i, c: (0, 0))
    sc_spec = pl.BlockSpec((n_hp, grp, PEER_NKEYS, V7X_LANES), lambda i, c: (0, jnp.minimum(i + 1, n_tiles - 1), 0, 0))

    def chunk(c, offset):
        return jnp.clip(2 * c + offset, 0, n_chunks - 1)

    def u_spec(offset):
        return pl.BlockSpec((d, PEER_ECHUNK), lambda i, c: (0, chunk(c, offset)))

    def v_spec(offset):
        return pl.BlockSpec((PEER_ECHUNK, d), lambda i, c: (chunk(c, offset), 0))

    return pl.pallas_call(
        functools.partial(_peer_mix_kernel, unroll=64, n_chunks=n_chunks, units_per_step=units_per_step),
        out_shape=jax.ShapeDtypeStruct((t, d), F32),
        grid=(n_tiles, steps),
        in_specs=[tok_spec, tok_spec, first_spec, first_spec, first_spec, sc_spec, u_spec(0), u_spec(1),
                  v_spec(-1), v_spec(0)],
        out_specs=tok_spec,
        scratch_shapes=[
            pltpu.VMEM((tm * W_PITCH, PEER_NKEYS), jnp.uint32),
            pltpu.VMEM((tm, PEER_ECHUNK), F32),
            pltpu.VMEM((tm, PEER_ECHUNK), F32),
            pltpu.VMEM((tm, d), F32),
            pltpu.VMEM((tm, nj), jnp.int32),
            pltpu.VMEM((tm, nj), jnp.int32),
            pltpu.VMEM((tm, nj), F32),
            *_topk_scratch(grp, nj, slots=(units_per_step,)),
        ],
        compiler_params=_params(2),
        name="peer_mix",
    )(h, xn, *first_lists, sc, u, u, v, v)


def _final_norm_kernel(h_ref, g_ref, o_ref):
    o_ref[...] = _rms(h_ref[...], g_ref[...])


def final_rmsnorm(h, g):
    t, d = h.shape
    tm = OUT_TM
    return pl.pallas_call(
        _final_norm_kernel,
        out_shape=jax.ShapeDtypeStruct((t, d), F32),
        grid=(t // tm,),
        in_specs=[pl.BlockSpec((tm, d), lambda i: (i, 0)), pl.BlockSpec((1, d), lambda i: (0, 0))],
        out_specs=pl.BlockSpec((tm, d), lambda i: (i, 0)),
        compiler_params=_params(1),
        name="final_norm",
    )(h, g.reshape(1, d))


def _rope_tables(seq):
    half = HEAD_DIM // 2
    inv = ROPE_THETA ** (-jnp.arange(half, dtype=F32) / half)
    ang = jnp.arange(seq, dtype=F32)[:, None] * inv[None, :]
    cos, sin = jnp.cos(ang), jnp.sin(ang)
    return jnp.concatenate([cos, cos], axis=1), jnp.concatenate([-sin, sin], axis=1)


def kernel(x, norm_mix, norm_ffn, even_w_in, even_w_out, odd_w_in, odd_conv, odd_w_out, peer_w_q, peer_sub_keys,
           peer_u, peer_v, final_norm):
    batch, seq, d = x.shape
    depth = norm_mix.shape[0]
    cos_t, sin_t = _rope_tables(seq)
    ret_tables = _retention_tables()
    mh, rh = MOBA_HEADS, RET_HEADS
    rope_blocks = tuple(range(0, 2 * mh)) + tuple(range(3 * mh, 3 * mh + 2 * rh))
    ret_k_blocks = tuple(range(3 * mh + rh, 3 * mh + 2 * rh))

    h = x.reshape(batch * seq, d)
    for layer in range(depth):
        i = layer // 2
        if layer % 2 == 0:
            proj = norm_proj(h, norm_mix[layer], _mxu(even_w_in[i]), cos_t, sin_t, seq, rope_blocks, ret_k_blocks,
                             HEAD_DIM ** -0.5)
            mo_lo, mo_hi = moba(proj, batch, seq)
            ro = retention(proj, ret_tables, batch, seq)
            h = even_out(h, mo_lo, mo_hi, ro, _mxu(even_w_out[i]), seq)
        else:
            proj = norm_proj(h, norm_mix[layer], _mxu(odd_w_in[i]), cos_t, sin_t, seq)
            h = conv_out(h, proj, odd_conv[i], _mxu(odd_w_out[i]), seq)
        keys = _mxu(peer_sub_keys[layer].reshape(2 * PEER_HEADS, PEER_NKEYS, PEER_HALF))
        xn, sc = peer_scores(h, norm_ffn[layer], _mxu(peer_w_q[layer]), keys)
        first_lists = peer_topk(sc, min(PEER_TM, h.shape[0]))
        h = peer_mix(h, xn, sc, first_lists, _mxu(peer_u[layer].T), _mxu(peer_v[layer]))
    return final_rmsnorm(h, final_norm).reshape(batch, seq, d)
```

```python
import functools

import jax
import jax.numpy as jnp
import numpy as np
from jax import lax
from jax.experimental import pallas as pl
from jax.experimental.pallas import tpu as pltpu

HEAD_DIM = 128
MOBA_HEADS = 4
RET_HEADS = 4
MOBA_BLOCK = 256
MOBA_TOPK = 3
RET_CHUNK = 128
PEER_HEADS = 8
PEER_NKEYS = 128
PEER_TOPK = 16
PEER_HALF = 128
ROPE_THETA = 10000.0
EPS = 1e-6

V7X_LANES = 128
V7X_SUBLANES = 8
V7X_VMEM_BYTES = 64 * 2**20
VMEM_LIMIT = V7X_VMEM_BYTES - 8 * 2**20

MXU_DTYPE = jnp.bfloat16
F32 = jnp.float32
NEG_INF = float("-inf")
NT_DIMS = (((1,), (1,)), ((), ()))
TN_DIMS = (((0,), (0,)), ((), ()))

CAST_ROWS = 512
PROJ_TM = 256
OUT_TM = 512
CONV_TM = 256
PEER_Q_TM = 256
PEER_TM = 512
PEER_ECHUNK = 2 * PEER_NKEYS
W_PITCH = 72


def _params(n_axes, flags=None):
    return pltpu.CompilerParams(dimension_semantics=("arbitrary",) * n_axes, vmem_limit_bytes=VMEM_LIMIT,
                                flags=flags)


def _mxu(x):
    return x.astype(MXU_DTYPE)


def _rms(x, g):
    return x * lax.rsqrt(jnp.mean(x * x, axis=-1, keepdims=True) + EPS) * g


def _layer_spec(stack, layer):
    return pl.BlockSpec((None, *stack.shape[1:]), lambda *_: (layer, 0, 0))


def _cast_kernel(w_ref, o_ref, *, transpose):
    w = w_ref[...]
    o_ref[...] = _mxu(w.T if transpose else w)


def cast_weights(w, transpose=False):
    n_layers, rows, cols = w.shape
    tr = min(rows, CAST_ROWS)
    out_shape = (n_layers, cols, rows) if transpose else w.shape
    out_block = (None, cols, tr) if transpose else (None, tr, cols)
    out_map = (lambda l, r: (l, 0, r)) if transpose else (lambda l, r: (l, r, 0))
    return pl.pallas_call(
        functools.partial(_cast_kernel, transpose=transpose),
        out_shape=jax.ShapeDtypeStruct(out_shape, MXU_DTYPE),
        grid=(n_layers, rows // tr),
        in_specs=[pl.BlockSpec((None, tr, cols), lambda l, r: (l, r, 0))],
        out_specs=pl.BlockSpec(out_block, out_map),
        compiler_params=_params(2),
        name="cast_weights",
    )(w)


def _norm_proj_kernel(h_ref, g_ref, w_ref, cos_ref, sin_ref, o_ref, *, rope_blocks, scale_blocks, scale, chunk):
    xn = _mxu(_rms(h_ref[...], g_ref[...]))
    n = o_ref.shape[1]
    for c0 in range(0, n, chunk):
        y = jnp.dot(xn, w_ref[:, c0:c0 + chunk], preferred_element_type=F32)
        for u in range(chunk // HEAD_DIM):
            blk = (c0 // HEAD_DIM) + u
            t = y[:, u * HEAD_DIM:(u + 1) * HEAD_DIM]
            if blk in rope_blocks:
                t = t * cos_ref[...] + pltpu.roll(t, HEAD_DIM // 2, axis=1) * sin_ref[...]
            if blk in scale_blocks:
                t = t * scale
            o_ref[:, blk * HEAD_DIM:(blk + 1) * HEAD_DIM] = t


def norm_proj(h, g, w, layer, cos_t, sin_t, seq, rope_blocks=(), scale_blocks=(), scale=1.0):
    t, d = h.shape
    n = w.shape[2]
    tm = PROJ_TM
    per_seq = seq // tm
    return pl.pallas_call(
        functools.partial(_norm_proj_kernel, rope_blocks=frozenset(rope_blocks), scale_blocks=frozenset(scale_blocks),
                          scale=scale, chunk=512),
        out_shape=jax.ShapeDtypeStruct((t, n), F32),
        grid=(t // tm,),
        in_specs=[
            pl.BlockSpec((tm, d), lambda i: (i, 0)),
            pl.BlockSpec((1, d), lambda i: (0, 0)),
            _layer_spec(w, layer),
            pl.BlockSpec((tm, HEAD_DIM), lambda i: (i % per_seq, 0)),
            pl.BlockSpec((tm, HEAD_DIM), lambda i: (i % per_seq, 0)),
        ],
        out_specs=pl.BlockSpec((tm, n), lambda i: (i, 0)),
        compiler_params=_params(1),
        name="norm_proj",
    )(h, g.reshape(1, d), w, cos_t, sin_t)


def _moba_kernel(qa_ref, qb_ref, k_ref, v_ref, oa_ref, ob_ref, kb_ref, vt_ref, km_ref, q2_ref, sel_ref, s_ref,
                 acc_ref, l_ref, *, nb, nbp):
    i = pl.program_id(2)
    blk = MOBA_BLOCK
    own = (i, nb - 1 - i)

    @pl.when(i == 0)
    def _():
        km_ref[...] = jnp.zeros_like(km_ref)
        for j in range(nb):
            rows = slice(j * blk, (j + 1) * blk)
            kb_ref[j] = _mxu(k_ref[rows, :])
            vt_ref[j] = _mxu(v_ref[rows, :].T)
            km_ref[j:j + 1, :] = jnp.sum(k_ref[rows, :], axis=0, keepdims=True) / float(blk)

    sub = lax.broadcasted_iota(jnp.int32, (nbp, blk), 0)
    kpos = lax.broadcasted_iota(jnp.int32, (blk, blk), 0)
    qpos = lax.broadcasted_iota(jnp.int32, (blk, blk), 1)

    def scores_t(j, w):
        return lax.dot_general(kb_ref[j], q2_ref[w], NT_DIMS, preferred_element_type=F32)

    m = []
    for w, q_ref in enumerate((qa_ref, qb_ref)):
        q = q_ref[...]
        q2_ref[w] = _mxu(q)
        gate = lax.dot_general(km_ref[...], q, NT_DIMS, precision=lax.Precision.HIGHEST, preferred_element_type=F32)
        rank = jnp.zeros((nbp, blk), jnp.int32)
        for jp in range(nb):
            other = gate[jp:jp + 1, :]
            beats = jnp.where(other > gate, 1, jnp.where(other == gate, jnp.where(jp < sub, 1, 0), 0))
            rank = rank + jnp.where(jp < own[w], beats, 0)
        sel_ref[w] = jnp.where(sub < own[w], jnp.where(rank < MOBA_TOPK, 1.0, 0.0), 0.0)
        s = jnp.where(kpos <= qpos, scores_t(own[w], w), NEG_INF)
        s_ref[nb - 1 + w] = s
        m.append(jnp.max(s, axis=0, keepdims=True))

    def past_pair(t):
        is_b = t >= i
        return is_b, jnp.where(is_b, 1, 0), jnp.where(is_b, t - i, t)

    for t in range(nb - 1):
        is_b, w, j = past_pair(t)
        s = jnp.where(sel_ref[w, pl.ds(j, 1), :] > 0.0, scores_t(j, w), NEG_INF)
        s_ref[t] = s
        mt = jnp.max(s, axis=0, keepdims=True)
        m = [jnp.where(is_b, m[0], jnp.maximum(m[0], mt)), jnp.where(is_b, jnp.maximum(m[1], mt), m[1])]

    c_exp = (HEAD_DIM ** -0.5) * float(np.log2(np.e))

    def weigh(slot, j, mq):
        p = jnp.exp2((s_ref[slot] - mq) * c_exp)
        return jnp.sum(p, axis=0, keepdims=True), jnp.dot(vt_ref[j], _mxu(p), preferred_element_type=F32)

    for w in range(2):
        l_ref[w], acc_ref[w] = weigh(nb - 1 + w, own[w], m[w])
    for t in range(nb - 1):
        is_b, w, j = past_pair(t)
        lt, at = weigh(t, j, jnp.where(is_b, m[1], m[0]))
        l_ref[w] += lt
        acc_ref[w] += at
    oa_ref[...] = (acc_ref[0] / l_ref[0]).T
    ob_ref[...] = (acc_ref[1] / l_ref[1]).T


def moba(proj, batch, seq):
    t = proj.shape[0]
    nb = seq // MOBA_BLOCK
    nbp = max(nb, V7X_SUBLANES)
    hd = HEAD_DIM
    assert nb % 2 == 0
    nh = nb // 2
    out = jax.ShapeDtypeStruct((t // 2, MOBA_HEADS * hd), F32)
    out_spec = pl.BlockSpec((MOBA_BLOCK, hd), lambda b, h, i: (b * nh + i, h))
    return pl.pallas_call(
        functools.partial(_moba_kernel, nb=nb, nbp=nbp),
        out_shape=(out, out),
        grid=(batch, MOBA_HEADS, nh),
        in_specs=[
            pl.BlockSpec((MOBA_BLOCK, hd), lambda b, h, i: (b * nb + i, h)),
            pl.BlockSpec((MOBA_BLOCK, hd), lambda b, h, i: (b * nb + nb - 1 - i, h)),
            pl.BlockSpec((seq, hd), lambda b, h, i: (b, MOBA_HEADS + h)),
            pl.BlockSpec((seq, hd), lambda b, h, i: (b, 2 * MOBA_HEADS + h)),
        ],
        out_specs=(out_spec, out_spec),
        scratch_shapes=[
            pltpu.VMEM((nb, MOBA_BLOCK, hd), MXU_DTYPE),
            pltpu.VMEM((nb, hd, MOBA_BLOCK), MXU_DTYPE),
            pltpu.VMEM((nbp, hd), F32),
            pltpu.VMEM((2, MOBA_BLOCK, hd), MXU_DTYPE),
            pltpu.VMEM((2, nbp, MOBA_BLOCK), F32),
            pltpu.VMEM((nb + 1, MOBA_BLOCK, MOBA_BLOCK), F32),
            pltpu.VMEM((2, hd, MOBA_BLOCK), F32),
            pltpu.VMEM((2, 1, MOBA_BLOCK), F32),
        ],
        compiler_params=_params(3),
        name="moba",
    )(proj, proj, proj, proj)


def _retention_kernel(q_ref, k_ref, v_ref, g_ref, dm_ref, xi_ref, ze_ref, gc_ref, o_ref, st_ref):
    @pl.when(pl.program_id(0) == 0)
    def _():
        st_ref[...] = jnp.zeros_like(st_ref)

    for b in range(q_ref.shape[0]):
        for h in range(RET_HEADS):
            sl = slice(h * HEAD_DIM, (h + 1) * HEAD_DIM)
            q = _mxu(q_ref[b, :, sl])
            k = k_ref[b, :, sl]
            v = _mxu(v_ref[b, :, sl])
            st = st_ref[b, h]
            inner = lax.dot_general(q, _mxu(k), NT_DIMS, preferred_element_type=F32) * dm_ref[h]
            o = (jnp.dot(_mxu(inner), v, preferred_element_type=F32)
                 + jnp.dot(q, _mxu(st), preferred_element_type=F32) * xi_ref[h])
            st_ref[b, h] = (st * gc_ref[h]
                            + lax.dot_general(_mxu(k * ze_ref[h]), v, TN_DIMS, preferred_element_type=F32))
            o = o * lax.rsqrt(jnp.mean(o * o, axis=-1, keepdims=True) + EPS)
            gate = g_ref[b, :, sl]
            o_ref[b, :, sl] = o * (gate * jax.nn.sigmoid(gate))


def retention(proj, tables, batch, seq):
    t, n = proj.shape
    nc = seq // RET_CHUNK
    w = RET_HEADS * HEAD_DIM
    base = 3 * MOBA_HEADS * HEAD_DIM // w
    tab_spec = pl.BlockSpec((RET_HEADS, RET_CHUNK, HEAD_DIM), lambda c: (0, 0, 0))
    proj3 = proj.reshape(batch, seq, n)

    def col_spec(block):
        return pl.BlockSpec((batch, RET_CHUNK, w), lambda c: (0, c, block))

    out = pl.pallas_call(
        _retention_kernel,
        out_shape=jax.ShapeDtypeStruct((batch, seq, w), F32),
        grid=(nc,),
        in_specs=[col_spec(base), col_spec(base + 1), col_spec(base + 2), col_spec(base + 3),
                  tab_spec, tab_spec, tab_spec, tab_spec],
        out_specs=col_spec(0),
        scratch_shapes=[pltpu.VMEM((batch, RET_HEADS, HEAD_DIM, HEAD_DIM), F32)],
        compiler_params=_params(1),
        name="retention",
    )(proj3, proj3, proj3, proj3, *tables)
    return out.reshape(t, w)


def _retention_tables():
    c = RET_CHUNK
    log_g = jnp.log(1.0 - 2.0 ** (-5.0 - jnp.arange(RET_HEADS, dtype=F32)))
    idx = jnp.arange(c, dtype=F32)
    diff = idx[:, None] - idx[None, :]
    dmask = jnp.where(diff >= 0, jnp.exp(log_g[:, None, None] * jnp.maximum(diff, 0.0)), 0.0)
    xi = jnp.exp(log_g[:, None] * (idx + 1.0))[..., None]
    zeta = jnp.exp(log_g[:, None] * (c - 1.0 - idx))[..., None]
    gc = jnp.exp(log_g * c)[:, None, None]
    full = (RET_HEADS, c, HEAD_DIM)
    return (dmask, jnp.broadcast_to(xi, full), jnp.broadcast_to(zeta, full), jnp.broadcast_to(gc, full))


def _even_out_kernel(h_ref, lo_ref, hi_ref, r_ref, w_ref, o_ref, *, nb):
    in_lo = (pl.program_id(0) % nb) < nb // 2
    mo = jnp.where(in_lo, lo_ref[...], hi_ref[...])
    y = jnp.concatenate([_mxu(mo), _mxu(r_ref[...])], axis=1)
    o_ref[...] = h_ref[...] + jnp.dot(y, w_ref[...], preferred_element_type=F32)


def even_out(h, mo_lo, mo_hi, ro, w, layer, seq):
    t, d = h.shape
    tm = MOBA_BLOCK
    nb = seq // tm
    nh = nb // 2

    def lo_map(g):
        return ((g // nb) * nh + jnp.minimum(g % nb, nh - 1), 0)

    def hi_map(g):
        return ((g // nb) * nh + (nb - 1 - jnp.maximum(g % nb, nh)), 0)

    return pl.pallas_call(
        functools.partial(_even_out_kernel, nb=nb),
        out_shape=jax.ShapeDtypeStruct((t, d), F32),
        grid=(t // tm,),
        in_specs=[
            pl.BlockSpec((tm, d), lambda g: (g, 0)),
            pl.BlockSpec((tm, mo_lo.shape[1]), lo_map),
            pl.BlockSpec((tm, mo_hi.shape[1]), hi_map),
            pl.BlockSpec((tm, ro.shape[1]), lambda g: (g, 0)),
            _layer_spec(w, layer),
        ],
        out_specs=pl.BlockSpec((tm, d), lambda g: (g, 0)),
        compiler_params=_params(1),
        name="even_out",
    )(h, mo_lo, mo_hi, ro, w)


def _conv_out_kernel(h_ref, bg_ref, cg_ref, hx_ref, cgp_ref, hxp_ref, cw_ref, w_ref, o_ref, *, per_seq):
    i = pl.program_id(0)
    u = cg_ref[...] * hx_ref[...]
    up = jnp.where(i % per_seq == 0, 0.0, cgp_ref[...] * hxp_ref[...])
    row = lax.broadcasted_iota(jnp.int32, u.shape, 0)
    p1 = up[V7X_SUBLANES - 1:V7X_SUBLANES, :]
    p2 = up[V7X_SUBLANES - 2:V7X_SUBLANES - 1, :]
    u1 = jnp.where(row == 0, p1, pltpu.roll(u, 1, axis=0))
    u2 = jnp.where(row == 0, p2, jnp.where(row == 1, p1, pltpu.roll(u, 2, axis=0)))
    y = cw_ref[0:1, :] * u2 + cw_ref[1:2, :] * u1 + cw_ref[2:3, :] * u
    z = _mxu(bg_ref[...] * y)
    o_ref[...] = h_ref[...] + jnp.dot(z, w_ref[...], preferred_element_type=F32)


def conv_out(h, proj, conv_w, w, layer, seq):
    t, d = h.shape
    tm = CONV_TM
    per_seq = seq // tm
    halo = tm // V7X_SUBLANES
    return pl.pallas_call(
        functools.partial(_conv_out_kernel, per_seq=per_seq),
        out_shape=jax.ShapeDtypeStruct((t, d), F32),
        grid=(t // tm,),
        in_specs=[
            pl.BlockSpec((tm, d), lambda i: (i, 0)),
            pl.BlockSpec((tm, d), lambda i: (i, 0)),
            pl.BlockSpec((tm, d), lambda i: (i, 1)),
            pl.BlockSpec((tm, d), lambda i: (i, 2)),
            pl.BlockSpec((V7X_SUBLANES, d), lambda i: (jnp.maximum(i * halo - 1, 0), 1)),
            pl.BlockSpec((V7X_SUBLANES, d), lambda i: (jnp.maximum(i * halo - 1, 0), 2)),
            pl.BlockSpec(conv_w.shape, lambda i: (0, 0)),
            _layer_spec(w, layer),
        ],
        out_specs=pl.BlockSpec((tm, d), lambda i: (i, 0)),
        compiler_params=_params(1),
        name="conv_out",
    )(h, proj, proj, proj, proj, proj, conv_w, w)


def _peer_scores_kernel(h_ref, g_ref, wq_ref, keys_ref, xn_ref, sc_ref):
    xn = _mxu(_rms(h_ref[...], g_ref[...]))
    xn_ref[...] = xn
    n_hp = keys_ref.shape[0]
    per = 512 // PEER_HALF
    for c in range(n_hp // per):
        q = jnp.dot(xn, wq_ref[:, c * 512:(c + 1) * 512], preferred_element_type=F32)
        for u in range(per):
            hp = c * per + u
            qs = _mxu(q[:, u * PEER_HALF:(u + 1) * PEER_HALF])
            sc = lax.dot_general(keys_ref[hp], qs, NT_DIMS, preferred_element_type=F32)
            for grp in range(sc_ref.shape[1]):
                sc_ref[hp, grp] = sc[:, grp * V7X_LANES:(grp + 1) * V7X_LANES]


def peer_scores(h, g, wq, layer, keys):
    t, d = h.shape
    tm = PEER_Q_TM
    n_hp = keys.shape[0]
    grp = tm // V7X_LANES
    return pl.pallas_call(
        _peer_scores_kernel,
        out_shape=(jax.ShapeDtypeStruct((t, d), MXU_DTYPE),
                   jax.ShapeDtypeStruct((n_hp, t // V7X_LANES, PEER_NKEYS, V7X_LANES), F32)),
        grid=(t // tm,),
        in_specs=[
            pl.BlockSpec((tm, d), lambda i: (i, 0)),
            pl.BlockSpec((1, d), lambda i: (0, 0)),
            _layer_spec(wq, layer),
            pl.BlockSpec(keys.shape, lambda i: (0, 0, 0)),
        ],
        out_specs=(pl.BlockSpec((tm, d), lambda i: (i, 0)),
                   pl.BlockSpec((n_hp, grp, PEER_NKEYS, V7X_LANES), lambda i: (0, i, 0, 0))),
        compiler_params=_params(1),
        name="peer_scores",
    )(h, g.reshape(1, d), wq, keys)


def _extract_max(x, pos):
    m = jnp.max(x, axis=-2, keepdims=True)
    sel = jnp.min(jnp.where(x == m, pos, float(2**30)), axis=-2, keepdims=True)
    return m, sel, jnp.where(pos == sel, NEG_INF, x)


def _topk_rows(x, pos, k):
    vals, ids = [], []
    for _ in range(k):
        m, sel, x = _extract_max(x, pos)
        vals.append(m)
        ids.append(sel)
    return jnp.concatenate(vals, axis=-2), jnp.concatenate(ids, axis=-2)


def _lookup_rows(table, idx):
    out = jnp.zeros(idx.shape, table.dtype)
    for a in range(table.shape[0]):
        out = jnp.where(idx == a, table[a:a + 1, :], out)
    return out


def _key_positions():
    return lax.broadcasted_iota(jnp.int32, (PEER_NKEYS, V7X_LANES), 0).astype(F32)


def _pair_candidates(v1, v2, rows):
    k = PEER_TOPK
    n = V7X_LANES
    r16 = lax.broadcasted_iota(jnp.int32, (k, n), 0)
    r8 = lax.broadcasted_iota(jnp.int32, (V7X_SUBLANES, n), 0)
    cand, pos = [v1[0:1, :] + v2], [r16]
    for a in range(1, V7X_SUBLANES):
        piece = v1[a:a + 1, :] + v2[0:V7X_SUBLANES, :]
        cand.append(jnp.where(r8 < k // (a + 1), piece, NEG_INF))
        pos.append(a * k + r8)
    cand.append(v1[V7X_SUBLANES:, :] + v2[0:1, :])
    pos.append((V7X_SUBLANES + r8) * k)
    pad = rows - sum(c.shape[0] for c in cand)
    if pad > 0:
        cand.append(jnp.full((pad, n), NEG_INF, F32))
        pos.append(jnp.zeros((pad, n), jnp.int32))
    return jnp.concatenate(cand, axis=0), jnp.concatenate(pos, axis=0).astype(F32)


def _emit_lists(top_s, top_pos, n1, n2, h, grp, l1_ref, l2_ref, lg_ref):
    k = PEER_TOPK
    top_pos = top_pos.astype(jnp.int32)
    e1 = _lookup_rows(n1, lax.shift_right_logical(top_pos, k.bit_length() - 1))
    e2 = _lookup_rows(n2, top_pos & (k - 1))
    ex = jnp.exp(top_s - top_s[0:1, :])
    gate = ex / jnp.sum(ex, axis=0, keepdims=True)
    rows = pl.ds(pl.multiple_of(h * k, k), k)
    l1_ref[grp, rows, :] = e1.astype(jnp.int32)
    l2_ref[grp, rows, :] = e2.astype(jnp.int32)
    lg_ref[grp, rows, :] = gate


def _transpose_lists(src_refs, dst_refs):
    for src, dst in zip(src_refs, dst_refs):
        for grp in range(src.shape[0]):
            dst[grp * V7X_LANES:(grp + 1) * V7X_LANES, :] = src[grp].T


def _peer_topk_kernel(sc_ref, i1_ref, i2_ref, g_ref, l1_ref, l2_ref, lg_ref, hv_ref, hn_ref):
    n_heads = sc_ref.shape[0] // 2

    def unit(n, carry):
        h, grp = n % n_heads, n // n_heads
        x = jnp.stack([sc_ref[2 * h, grp], sc_ref[2 * h + 1, grp]])
        hv_ref[...], hn_ref[...] = _topk_rows(x, _key_positions(), PEER_TOPK)
        cand, cand_pos = _pair_candidates(hv_ref[0], hv_ref[1], 0)
        top_s, top_pos = _topk_rows(cand, cand_pos, PEER_TOPK)
        _emit_lists(top_s, top_pos, hn_ref[0], hn_ref[1], h, grp, l1_ref, l2_ref, lg_ref)
        return carry

    lax.fori_loop(0, n_heads * sc_ref.shape[1], unit, 0)
    _transpose_lists((l1_ref, l2_ref, lg_ref), (i1_ref, i2_ref, g_ref))


def _topk_scratch(n_groups, nj, slots=()):
    lists = (n_groups, nj, V7X_LANES)
    halves = (*slots, 2, PEER_TOPK, V7X_LANES)
    return [pltpu.VMEM(lists, jnp.int32), pltpu.VMEM(lists, jnp.int32), pltpu.VMEM(lists, F32),
            pltpu.VMEM(halves, F32), pltpu.VMEM(halves, F32)]


def peer_topk(sc, n_tokens):
    n_hp, _, nk, _ = sc.shape
    grp = n_tokens // V7X_LANES
    nj = (n_hp // 2) * PEER_TOPK
    out_spec = pl.BlockSpec((n_tokens, nj), lambda i: (0, 0))
    return pl.pallas_call(
        _peer_topk_kernel,
        out_shape=(jax.ShapeDtypeStruct((n_tokens, nj), jnp.int32), jax.ShapeDtypeStruct((n_tokens, nj), jnp.int32),
                   jax.ShapeDtypeStruct((n_tokens, nj), F32)),
        grid=(1,),
        in_specs=[pl.BlockSpec((n_hp, grp, nk, V7X_LANES), lambda i: (0, 0, 0, 0))],
        out_specs=(out_spec, out_spec, out_spec),
        scratch_shapes=_topk_scratch(grp, nj),
        compiler_params=_params(1),
        name="peer_topk",
    )(sc)


def _peer_mix_kernel(h_ref, xn_ref, f1_ref, f2_ref, fg_ref, sc_ref, ua_ref, ub_ref, va_ref, vb_ref, o_ref, w_ref,
                     ha_ref, hb_ref, acc_ref, i1_ref, i2_ref, g_ref, l1_ref, l2_ref, lg_ref, hv_ref, hn_ref, *, unroll,
                     n_chunks, units_per_step):
    tile = pl.program_id(0)
    c = pl.program_id(1)
    last = pl.num_programs(1) - 1
    tm = xn_ref.shape[0]
    nk = PEER_NKEYS
    half = nk // 2
    next_lists = (l1_ref, l2_ref, lg_ref)
    lists = (i1_ref, i2_ref, g_ref)

    @pl.when((c == 0) & (tile == 0))
    def _():
        for src, dst in zip((f1_ref, f2_ref, fg_ref), lists):
            dst[...] = src[...]
        hv_ref[...] = jnp.zeros_like(hv_ref)
        hn_ref[...] = jnp.zeros_like(hn_ref)

    @pl.when((c == 0) & (tile > 0))
    def _():
        _transpose_lists(next_lists, lists)

    @pl.when(c == 0)
    def _():
        acc_ref[...] = jnp.zeros_like(acc_ref)
        hb_ref[...] = jnp.zeros_like(hb_ref)
        r = lax.broadcasted_iota(jnp.int32, (nk, nk), 0).astype(F32).astype(jnp.bfloat16)
        one = jnp.ones((nk, nk), jnp.bfloat16)
        zero = jnp.zeros((nk, nk), jnp.bfloat16)

        def tok(t, carry):
            i1 = i1_ref[pl.ds(t, 1), :].astype(F32).astype(jnp.bfloat16)
            i2 = i2_ref[pl.ds(t, 1), :].astype(F32).astype(jnp.bfloat16)
            g = (0.5 * g_ref[pl.ds(t, 1), :]).astype(jnp.bfloat16)
            a = jnp.where(r == i1, g, zero)
            b = jnp.where(r == i2, one, zero)
            w = lax.dot_general(a, b, NT_DIMS, preferred_element_type=F32).astype(jnp.bfloat16)
            w_ref[pl.ds(pl.multiple_of(t * W_PITCH, V7X_SUBLANES), half), :] = pltpu.bitcast(w, jnp.uint32)
            return carry

        lax.fori_loop(0, tm, tok, 0, unroll=unroll)

    def hidden(u_ref):
        return jnp.dot(xn_ref[...], u_ref[...], preferred_element_type=F32)

    def consume(hid_ref, chunk, v_ref, valid):
        packed = w_ref[pl.ds(jnp.clip(chunk, 0, n_chunks - 1), tm, stride=W_PITCH), :]
        if valid is not None:
            packed = jnp.where(valid, packed, jnp.uint32(0))
        w_even = lax.bitcast_convert_type(packed << 16, F32)
        w_odd = lax.bitcast_convert_type(packed & jnp.uint32(0xFFFF0000), F32)
        hid = hid_ref[...]
        act = hid * (1.0 + lax.erf(hid * (2.0 ** -0.5)))
        p = jnp.concatenate([w_even * act[:, :nk], w_odd * act[:, nk:]], axis=1)
        acc_ref[...] += jnp.dot(_mxu(p), v_ref[...], preferred_element_type=F32)

    n_heads = sc_ref.shape[0] // 2
    n_units = n_heads * sc_ref.shape[1]
    xs, poss, prev = [], [], []
    for s in range(units_per_step):
        n = jnp.minimum(c * units_per_step + s, n_units - 1)
        h, grp = n % n_heads, n // n_heads
        cand, cand_pos = _pair_candidates(hv_ref[s, 0], hv_ref[s, 1], PEER_NKEYS)
        xs += [sc_ref[2 * h, grp], sc_ref[2 * h + 1, grp], cand]
        poss += [_key_positions(), _key_positions(), cand_pos]
        n = jnp.clip((c - 1) * units_per_step + s, 0, n_units - 1)
        prev.append((n % n_heads, n // n_heads, hn_ref[s, 0], hn_ref[s, 1]))
    x, pos = jnp.stack(xs), jnp.stack(poss)
    vals, ids = [], []

    def select(n_steps):
        nonlocal x
        for _ in range(n_steps):
            m, sel, x = _extract_max(x, pos)
            vals.append(m)
            ids.append(sel)

    quarter = PEER_TOPK // 4
    select(quarter)
    ha_ref[...] = hidden(ua_ref)
    select(quarter)
    consume(hb_ref, 2 * c - 1, va_ref, None)
    select(quarter)
    hb_ref[...] = hidden(ub_ref)
    select(PEER_TOPK - 3 * quarter)
    consume(ha_ref, 2 * c, vb_ref, c < last)

    vals, ids = jnp.concatenate(vals, axis=-2), jnp.concatenate(ids, axis=-2)
    for s, (h, grp, n1, n2) in enumerate(prev):
        hv_ref[s] = vals[3 * s:3 * s + 2]
        hn_ref[s] = ids[3 * s:3 * s + 2]
        _emit_lists(vals[3 * s + 2], ids[3 * s + 2], n1, n2, h, grp, *next_lists)

    @pl.when(c == last)
    def _():
        o_ref[...] = h_ref[...] + acc_ref[...]


def peer_mix(h, xn, sc, first_lists, u, v, layer):
    t, d = h.shape
    tm = first_lists[0].shape[0]
    nj = first_lists[0].shape[1]
    n_tiles = t // tm
    grp = tm // V7X_LANES
    n_hp = sc.shape[0]
    n_chunks = v.shape[1] // PEER_ECHUNK
    steps = n_chunks // 2 + 1
    n_units = (n_hp // 2) * grp
    units_per_step = -(-n_units // (steps - 1))
    tok_spec = pl.BlockSpec((tm, d), lambda i, c: (i, 0))
    first_spec = pl.BlockSpec((tm, nj), lambda i, c: (0, 0))
    sc_spec = pl.BlockSpec((n_hp, grp, PEER_NKEYS, V7X_LANES), lambda i, c: (0, jnp.minimum(i + 1, n_tiles - 1), 0, 0))

    def chunk(c, offset):
        return jnp.clip(2 * c + offset, 0, n_chunks - 1)

    def u_spec(offset):
        return pl.BlockSpec((None, d, PEER_ECHUNK), lambda i, c: (layer, 0, chunk(c, offset)))

    def v_spec(offset):
        return pl.BlockSpec((None, PEER_ECHUNK, d), lambda i, c: (layer, chunk(c, offset), 0))

    return pl.pallas_call(
        functools.partial(_peer_mix_kernel, unroll=64, n_chunks=n_chunks, units_per_step=units_per_step),
        out_shape=jax.ShapeDtypeStruct((t, d), F32),
        grid=(n_tiles, steps),
        in_specs=[tok_spec, tok_spec, first_spec, first_spec, first_spec, sc_spec, u_spec(0), u_spec(1),
                  v_spec(-1), v_spec(0)],
        out_specs=tok_spec,
        scratch_shapes=[
            pltpu.VMEM((tm * W_PITCH, PEER_NKEYS), jnp.uint32),
            pltpu.VMEM((tm, PEER_ECHUNK), F32),
            pltpu.VMEM((tm, PEER_ECHUNK), F32),
            pltpu.VMEM((tm, d), F32),
            pltpu.VMEM((tm, nj), jnp.int32),
            pltpu.VMEM((tm, nj), jnp.int32),
            pltpu.VMEM((tm, nj), F32),
            *_topk_scratch(grp, nj, slots=(units_per_step,)),
        ],
        compiler_params=_params(2),
        name="peer_mix",
    )(h, xn, *first_lists, sc, u, u, v, v)


def _final_norm_kernel(h_ref, g_ref, o_ref):
    o_ref[...] = _rms(h_ref[...], g_ref[...])


def final_rmsnorm(h, g):
    t, d = h.shape
    tm = OUT_TM
    return pl.pallas_call(
        _final_norm_kernel,
        out_shape=jax.ShapeDtypeStruct((t, d), F32),
        grid=(t // tm,),
        in_specs=[pl.BlockSpec((tm, d), lambda i: (i, 0)), pl.BlockSpec((1, d), lambda i: (0, 0))],
        out_specs=pl.BlockSpec((tm, d), lambda i: (i, 0)),
        compiler_params=_params(1),
        name="final_norm",
    )(h, g.reshape(1, d))


def _rope_tables(seq):
    half = HEAD_DIM // 2
    inv = ROPE_THETA ** (-jnp.arange(half, dtype=F32) / half)
    ang = jnp.arange(seq, dtype=F32)[:, None] * inv[None, :]
    cos, sin = jnp.cos(ang), jnp.sin(ang)
    return jnp.concatenate([cos, cos], axis=1), jnp.concatenate([-sin, sin], axis=1)


def kernel(x, norm_mix, norm_ffn, even_w_in, even_w_out, odd_w_in, odd_conv, odd_w_out, peer_w_q, peer_sub_keys,
           peer_u, peer_v, final_norm):
    batch, seq, d = x.shape
    depth = norm_mix.shape[0]
    cos_t, sin_t = _rope_tables(seq)
    ret_tables = _retention_tables()
    mh, rh = MOBA_HEADS, RET_HEADS
    rope_blocks = tuple(range(0, 2 * mh)) + tuple(range(3 * mh, 3 * mh + 2 * rh))
    ret_k_blocks = tuple(range(3 * mh + rh, 3 * mh + 2 * rh))

    even_in, even_out_w = cast_weights(even_w_in), cast_weights(even_w_out)
    odd_in, odd_out_w = cast_weights(odd_w_in), cast_weights(odd_w_out)
    w_q, u_t, v = cast_weights(peer_w_q), cast_weights(peer_u, transpose=True), cast_weights(peer_v)
    keys = _mxu(peer_sub_keys.reshape(depth, 2 * PEER_HEADS, PEER_NKEYS, PEER_HALF))

    h = x.reshape(batch * seq, d)
    for layer in range(depth):
        i = layer // 2
        if layer % 2 == 0:
            proj = norm_proj(h, norm_mix[layer], even_in, i, cos_t, sin_t, seq, rope_blocks, ret_k_blocks,
                             HEAD_DIM ** -0.5)
            mo_lo, mo_hi = moba(proj, batch, seq)
            ro = retention(proj, ret_tables, batch, seq)
            h = even_out(h, mo_lo, mo_hi, ro, even_out_w, i, seq)
        else:
            proj = norm_proj(h, norm_mix[layer], odd_in, i, cos_t, sin_t, seq)
            h = conv_out(h, proj, odd_conv[i], odd_out_w, i, seq)
        xn, sc = peer_scores(h, norm_ffn[layer], w_q, layer, keys[layer])
        first_lists = peer_topk(sc, min(PEER_TM, h.shape[0]))
        h = peer_mix(h, xn, sc, first_lists, u_t, v, layer)
    return final_rmsnorm(h, final_norm).reshape(batch, seq, d)
```

```python
import functools

import jax
import jax.numpy as jnp
import numpy as np
from jax import lax
from jax.experimental import pallas as pl
from jax.experimental.pallas import tpu as pltpu

HEAD_DIM = 128
MOBA_HEADS = 4
RET_HEADS = 4
MOBA_BLOCK = 256
MOBA_TOPK = 3
RET_CHUNK = 128
PEER_HEADS = 8
PEER_NKEYS = 128
PEER_TOPK = 16
PEER_HALF = 128
ROPE_THETA = 10000.0
EPS = 1e-6

V7X_LANES = 128
V7X_SUBLANES = 8
V7X_VMEM_BYTES = 64 * 2**20
VMEM_LIMIT = V7X_VMEM_BYTES - 8 * 2**20

MXU_DTYPE = jnp.bfloat16
F32 = jnp.float32
NEG_INF = float("-inf")
NT_DIMS = (((1,), (1,)), ((), ()))
TN_DIMS = (((0,), (0,)), ((), ()))

CAST_ROWS = 512
PROJ_TM = 256
OUT_TM = 512
CONV_TM = 256
PEER_Q_TM = 256
PEER_TM = 512
PEER_ECHUNK = 2 * PEER_NKEYS
W_PITCH = 72


def _params(n_axes, flags=None):
    return pltpu.CompilerParams(dimension_semantics=("arbitrary",) * n_axes, vmem_limit_bytes=VMEM_LIMIT,
                                flags=flags)


def _mxu(x):
    return x.astype(MXU_DTYPE)


def _rms(x, g):
    return x * lax.rsqrt(jnp.mean(x * x, axis=-1, keepdims=True) + EPS) * g


def _layer_spec(stack, layer):
    return pl.BlockSpec((None, *stack.shape[1:]), lambda *_: (layer, 0, 0))


def _cast_kernel(w_ref, o_ref, *, slab):
    if slab is None:
        o_ref[...] = _mxu(w_ref[...])
    else:
        for q in range(o_ref.shape[0]):
            o_ref[q] = _mxu(w_ref[q * slab:(q + 1) * slab, :].T)


def cast_weights(w, slab=None):
    n_layers, rows, cols = w.shape
    tr = min(rows, CAST_ROWS)
    if slab is None:
        out_shape, out_block, out_map = w.shape, (None, tr, cols), lambda l, r: (l, r, 0)
    else:
        out_shape = (n_layers, rows // slab, cols, slab)
        out_block, out_map = (None, tr // slab, cols, slab), lambda l, r: (l, r, 0, 0)
    return pl.pallas_call(
        functools.partial(_cast_kernel, slab=slab),
        out_shape=jax.ShapeDtypeStruct(out_shape, MXU_DTYPE),
        grid=(n_layers, rows // tr),
        in_specs=[pl.BlockSpec((None, tr, cols), lambda l, r: (l, r, 0))],
        out_specs=pl.BlockSpec(out_block, out_map),
        compiler_params=_params(2),
        name="cast_weights",
    )(w)


def _norm_proj_kernel(h_ref, g_ref, w_ref, cos_ref, sin_ref, o_ref, *, rope_blocks, scale_blocks, scale, chunk):
    xn = _mxu(_rms(h_ref[...], g_ref[...]))
    n = o_ref.shape[1]
    for c0 in range(0, n, chunk):
        y = jnp.dot(xn, w_ref[:, c0:c0 + chunk], preferred_element_type=F32)
        for u in range(chunk // HEAD_DIM):
            blk = (c0 // HEAD_DIM) + u
            t = y[:, u * HEAD_DIM:(u + 1) * HEAD_DIM]
            if blk in rope_blocks:
                t = t * cos_ref[...] + pltpu.roll(t, HEAD_DIM // 2, axis=1) * sin_ref[...]
            if blk in scale_blocks:
                t = t * scale
            o_ref[:, blk * HEAD_DIM:(blk + 1) * HEAD_DIM] = t


def norm_proj(h, g, w, layer, cos_t, sin_t, seq, rope_blocks=(), scale_blocks=(), scale=1.0):
    t, d = h.shape
    n = w.shape[2]
    tm = PROJ_TM
    per_seq = seq // tm
    return pl.pallas_call(
        functools.partial(_norm_proj_kernel, rope_blocks=frozenset(rope_blocks), scale_blocks=frozenset(scale_blocks),
                          scale=scale, chunk=512),
        out_shape=jax.ShapeDtypeStruct((t, n), F32),
        grid=(t // tm,),
        in_specs=[
            pl.BlockSpec((tm, d), lambda i: (i, 0)),
            pl.BlockSpec((1, d), lambda i: (0, 0)),
            _layer_spec(w, layer),
            pl.BlockSpec((tm, HEAD_DIM), lambda i: (i % per_seq, 0)),
            pl.BlockSpec((tm, HEAD_DIM), lambda i: (i % per_seq, 0)),
        ],
        out_specs=pl.BlockSpec((tm, n), lambda i: (i, 0)),
        compiler_params=_params(1),
        name="norm_proj",
    )(h, g.reshape(1, d), w, cos_t, sin_t)


def _moba_kernel(qa_ref, qb_ref, k_ref, v_ref, oa_ref, ob_ref, kb_ref, vt_ref, km_ref, q2_ref, sel_ref, s_ref,
                 acc_ref, l_ref, *, nb, nbp):
    i = pl.program_id(2)
    blk = MOBA_BLOCK
    own = (i, nb - 1 - i)

    @pl.when(i == 0)
    def _():
        km_ref[...] = jnp.zeros_like(km_ref)
        for j in range(nb):
            rows = slice(j * blk, (j + 1) * blk)
            kb_ref[j] = _mxu(k_ref[rows, :])
            vt_ref[j] = _mxu(v_ref[rows, :].T)
            km_ref[j:j + 1, :] = jnp.sum(k_ref[rows, :], axis=0, keepdims=True) / float(blk)

    sub = lax.broadcasted_iota(jnp.int32, (nbp, blk), 0)
    kpos = lax.broadcasted_iota(jnp.int32, (blk, blk), 0)
    qpos = lax.broadcasted_iota(jnp.int32, (blk, blk), 1)

    def scores_t(j, w):
        return lax.dot_general(kb_ref[j], q2_ref[w], NT_DIMS, preferred_element_type=F32)

    m = []
    for w, q_ref in enumerate((qa_ref, qb_ref)):
        q = q_ref[...]
        q2_ref[w] = _mxu(q)
        gate = lax.dot_general(km_ref[...], q, NT_DIMS, precision=lax.Precision.HIGHEST, preferred_element_type=F32)
        rank = jnp.zeros((nbp, blk), jnp.int32)
        for jp in range(nb):
            other = gate[jp:jp + 1, :]
            beats = jnp.where(other > gate, 1, jnp.where(other == gate, jnp.where(jp < sub, 1, 0), 0))
            rank = rank + jnp.where(jp < own[w], beats, 0)
        sel_ref[w] = jnp.where(sub < own[w], jnp.where(rank < MOBA_TOPK, 1.0, 0.0), 0.0)
        s = jnp.where(kpos <= qpos, scores_t(own[w], w), NEG_INF)
        s_ref[nb - 1 + w] = s
        m.append(jnp.max(s, axis=0, keepdims=True))

    def past_pair(t):
        is_b = t >= i
        return is_b, jnp.where(is_b, 1, 0), jnp.where(is_b, t - i, t)

    for t in range(nb - 1):
        is_b, w, j = past_pair(t)
        s = jnp.where(sel_ref[w, pl.ds(j, 1), :] > 0.0, scores_t(j, w), NEG_INF)
        s_ref[t] = s
        mt = jnp.max(s, axis=0, keepdims=True)
        m = [jnp.where(is_b, m[0], jnp.maximum(m[0], mt)), jnp.where(is_b, jnp.maximum(m[1], mt), m[1])]

    c_exp = (HEAD_DIM ** -0.5) * float(np.log2(np.e))

    def weigh(slot, j, mq):
        p = jnp.exp2((s_ref[slot] - mq) * c_exp)
        return jnp.sum(p, axis=0, keepdims=True), jnp.dot(vt_ref[j], _mxu(p), preferred_element_type=F32)

    for w in range(2):
        l_ref[w], acc_ref[w] = weigh(nb - 1 + w, own[w], m[w])
    for t in range(nb - 1):
        is_b, w, j = past_pair(t)
        lt, at = weigh(t, j, jnp.where(is_b, m[1], m[0]))
        l_ref[w] += lt
        acc_ref[w] += at
    oa_ref[...] = (acc_ref[0] / l_ref[0]).T
    ob_ref[...] = (acc_ref[1] / l_ref[1]).T


def moba(proj, batch, seq):
    t = proj.shape[0]
    nb = seq // MOBA_BLOCK
    nbp = max(nb, V7X_SUBLANES)
    hd = HEAD_DIM
    assert nb % 2 == 0
    nh = nb // 2
    out = jax.ShapeDtypeStruct((t // 2, MOBA_HEADS * hd), F32)
    out_spec = pl.BlockSpec((MOBA_BLOCK, hd), lambda b, h, i: (b * nh + i, h))
    return pl.pallas_call(
        functools.partial(_moba_kernel, nb=nb, nbp=nbp),
        out_shape=(out, out),
        grid=(batch, MOBA_HEADS, nh),
        in_specs=[
            pl.BlockSpec((MOBA_BLOCK, hd), lambda b, h, i: (b * nb + i, h)),
            pl.BlockSpec((MOBA_BLOCK, hd), lambda b, h, i: (b * nb + nb - 1 - i, h)),
            pl.BlockSpec((seq, hd), lambda b, h, i: (b, MOBA_HEADS + h)),
            pl.BlockSpec((seq, hd), lambda b, h, i: (b, 2 * MOBA_HEADS + h)),
        ],
        out_specs=(out_spec, out_spec),
        scratch_shapes=[
            pltpu.VMEM((nb, MOBA_BLOCK, hd), MXU_DTYPE),
            pltpu.VMEM((nb, hd, MOBA_BLOCK), MXU_DTYPE),
            pltpu.VMEM((nbp, hd), F32),
            pltpu.VMEM((2, MOBA_BLOCK, hd), MXU_DTYPE),
            pltpu.VMEM((2, nbp, MOBA_BLOCK), F32),
            pltpu.VMEM((nb + 1, MOBA_BLOCK, MOBA_BLOCK), F32),
            pltpu.VMEM((2, hd, MOBA_BLOCK), F32),
            pltpu.VMEM((2, 1, MOBA_BLOCK), F32),
        ],
        compiler_params=_params(3),
        name="moba",
    )(proj, proj, proj, proj)


def _retention_kernel(q_ref, k_ref, v_ref, g_ref, dm_ref, xi_ref, ze_ref, gc_ref, o_ref, st_ref):
    @pl.when(pl.program_id(0) == 0)
    def _():
        st_ref[...] = jnp.zeros_like(st_ref)

    for b in range(q_ref.shape[0]):
        for h in range(RET_HEADS):
            sl = slice(h * HEAD_DIM, (h + 1) * HEAD_DIM)
            q = _mxu(q_ref[b, :, sl])
            k = k_ref[b, :, sl]
            v = _mxu(v_ref[b, :, sl])
            st = st_ref[b, h]
            inner = lax.dot_general(q, _mxu(k), NT_DIMS, preferred_element_type=F32) * dm_ref[h]
            o = (jnp.dot(_mxu(inner), v, preferred_element_type=F32)
                 + jnp.dot(q, _mxu(st), preferred_element_type=F32) * xi_ref[h])
            st_ref[b, h] = (st * gc_ref[h]
                            + lax.dot_general(_mxu(k * ze_ref[h]), v, TN_DIMS, preferred_element_type=F32))
            o = o * lax.rsqrt(jnp.mean(o * o, axis=-1, keepdims=True) + EPS)
            gate = g_ref[b, :, sl]
            o_ref[b, :, sl] = o * (gate * jax.nn.sigmoid(gate))


def retention(proj, tables, batch, seq):
    t, n = proj.shape
    nc = seq // RET_CHUNK
    w = RET_HEADS * HEAD_DIM
    base = 3 * MOBA_HEADS * HEAD_DIM // w
    tab_spec = pl.BlockSpec((RET_HEADS, RET_CHUNK, HEAD_DIM), lambda c: (0, 0, 0))
    proj3 = proj.reshape(batch, seq, n)

    def col_spec(block):
        return pl.BlockSpec((batch, RET_CHUNK, w), lambda c: (0, c, block))

    out = pl.pallas_call(
        _retention_kernel,
        out_shape=jax.ShapeDtypeStruct((batch, seq, w), F32),
        grid=(nc,),
        in_specs=[col_spec(base), col_spec(base + 1), col_spec(base + 2), col_spec(base + 3),
                  tab_spec, tab_spec, tab_spec, tab_spec],
        out_specs=col_spec(0),
        scratch_shapes=[pltpu.VMEM((batch, RET_HEADS, HEAD_DIM, HEAD_DIM), F32)],
        compiler_params=_params(1),
        name="retention",
    )(proj3, proj3, proj3, proj3, *tables)
    return out.reshape(t, w)


def _retention_tables():
    c = RET_CHUNK
    log_g = jnp.log(1.0 - 2.0 ** (-5.0 - jnp.arange(RET_HEADS, dtype=F32)))
    idx = jnp.arange(c, dtype=F32)
    diff = idx[:, None] - idx[None, :]
    dmask = jnp.where(diff >= 0, jnp.exp(log_g[:, None, None] * jnp.maximum(diff, 0.0)), 0.0)
    xi = jnp.exp(log_g[:, None] * (idx + 1.0))[..., None]
    zeta = jnp.exp(log_g[:, None] * (c - 1.0 - idx))[..., None]
    gc = jnp.exp(log_g * c)[:, None, None]
    full = (RET_HEADS, c, HEAD_DIM)
    return (dmask, jnp.broadcast_to(xi, full), jnp.broadcast_to(zeta, full), jnp.broadcast_to(gc, full))


def _even_out_kernel(h_ref, lo_ref, hi_ref, r_ref, w_ref, o_ref, *, nb):
    in_lo = (pl.program_id(0) % nb) < nb // 2
    mo = jnp.where(in_lo, lo_ref[...], hi_ref[...])
    y = jnp.concatenate([_mxu(mo), _mxu(r_ref[...])], axis=1)
    o_ref[...] = h_ref[...] + jnp.dot(y, w_ref[...], preferred_element_type=F32)


def even_out(h, mo_lo, mo_hi, ro, w, layer, seq):
    t, d = h.shape
    tm = MOBA_BLOCK
    nb = seq // tm
    nh = nb // 2

    def lo_map(g):
        return ((g // nb) * nh + jnp.minimum(g % nb, nh - 1), 0)

    def hi_map(g):
        return ((g // nb) * nh + (nb - 1 - jnp.maximum(g % nb, nh)), 0)

    return pl.pallas_call(
        functools.partial(_even_out_kernel, nb=nb),
        out_shape=jax.ShapeDtypeStruct((t, d), F32),
        grid=(t // tm,),
        in_specs=[
            pl.BlockSpec((tm, d), lambda g: (g, 0)),
            pl.BlockSpec((tm, mo_lo.shape[1]), lo_map),
            pl.BlockSpec((tm, mo_hi.shape[1]), hi_map),
            pl.BlockSpec((tm, ro.shape[1]), lambda g: (g, 0)),
            _layer_spec(w, layer),
        ],
        out_specs=pl.BlockSpec((tm, d), lambda g: (g, 0)),
        compiler_params=_params(1),
        name="even_out",
    )(h, mo_lo, mo_hi, ro, w)


def _conv_out_kernel(h_ref, bg_ref, cg_ref, hx_ref, cgp_ref, hxp_ref, cw_ref, w_ref, o_ref, *, per_seq):
    i = pl.program_id(0)
    u = cg_ref[...] * hx_ref[...]
    up = jnp.where(i % per_seq == 0, 0.0, cgp_ref[...] * hxp_ref[...])
    row = lax.broadcasted_iota(jnp.int32, u.shape, 0)
    p1 = up[V7X_SUBLANES - 1:V7X_SUBLANES, :]
    p2 = up[V7X_SUBLANES - 2:V7X_SUBLANES - 1, :]
    u1 = jnp.where(row == 0, p1, pltpu.roll(u, 1, axis=0))
    u2 = jnp.where(row == 0, p2, jnp.where(row == 1, p1, pltpu.roll(u, 2, axis=0)))
    y = cw_ref[0:1, :] * u2 + cw_ref[1:2, :] * u1 + cw_ref[2:3, :] * u
    z = _mxu(bg_ref[...] * y)
    o_ref[...] = h_ref[...] + jnp.dot(z, w_ref[...], preferred_element_type=F32)


def conv_out(h, proj, conv_w, w, layer, seq):
    t, d = h.shape
    tm = CONV_TM
    per_seq = seq // tm
    halo = tm // V7X_SUBLANES
    return pl.pallas_call(
        functools.partial(_conv_out_kernel, per_seq=per_seq),
        out_shape=jax.ShapeDtypeStruct((t, d), F32),
        grid=(t // tm,),
        in_specs=[
            pl.BlockSpec((tm, d), lambda i: (i, 0)),
            pl.BlockSpec((tm, d), lambda i: (i, 0)),
            pl.BlockSpec((tm, d), lambda i: (i, 1)),
            pl.BlockSpec((tm, d), lambda i: (i, 2)),
            pl.BlockSpec((V7X_SUBLANES, d), lambda i: (jnp.maximum(i * halo - 1, 0), 1)),
            pl.BlockSpec((V7X_SUBLANES, d), lambda i: (jnp.maximum(i * halo - 1, 0), 2)),
            pl.BlockSpec(conv_w.shape, lambda i: (0, 0)),
            _layer_spec(w, layer),
        ],
        out_specs=pl.BlockSpec((tm, d), lambda i: (i, 0)),
        compiler_params=_params(1),
        name="conv_out",
    )(h, proj, proj, proj, proj, proj, conv_w, w)


def _peer_scores_kernel(h_ref, g_ref, wq_ref, keys_ref, xn_ref, sc_ref):
    xn = _mxu(_rms(h_ref[...], g_ref[...]))
    xn_ref[...] = xn
    n_hp = keys_ref.shape[0]
    per = 512 // PEER_HALF
    for c in range(n_hp // per):
        q = jnp.dot(xn, wq_ref[:, c * 512:(c + 1) * 512], preferred_element_type=F32)
        for u in range(per):
            hp = c * per + u
            qs = _mxu(q[:, u * PEER_HALF:(u + 1) * PEER_HALF])
            sc = lax.dot_general(keys_ref[hp], qs, NT_DIMS, preferred_element_type=F32)
            for grp in range(sc_ref.shape[1]):
                sc_ref[hp, grp] = sc[:, grp * V7X_LANES:(grp + 1) * V7X_LANES]


def peer_scores(h, g, wq, layer, keys):
    t, d = h.shape
    tm = PEER_Q_TM
    n_hp = keys.shape[0]
    grp = tm // V7X_LANES
    return pl.pallas_call(
        _peer_scores_kernel,
        out_shape=(jax.ShapeDtypeStruct((t, d), MXU_DTYPE),
                   jax.ShapeDtypeStruct((n_hp, t // V7X_LANES, PEER_NKEYS, V7X_LANES), F32)),
        grid=(t // tm,),
        in_specs=[
            pl.BlockSpec((tm, d), lambda i: (i, 0)),
            pl.BlockSpec((1, d), lambda i: (0, 0)),
            _layer_spec(wq, layer),
            pl.BlockSpec(keys.shape, lambda i: (0, 0, 0)),
        ],
        out_specs=(pl.BlockSpec((tm, d), lambda i: (i, 0)),
                   pl.BlockSpec((n_hp, grp, PEER_NKEYS, V7X_LANES), lambda i: (0, i, 0, 0))),
        compiler_params=_params(1),
        name="peer_scores",
    )(h, g.reshape(1, d), wq, keys)


def _extract_max(x, pos):
    m = jnp.max(x, axis=-2, keepdims=True)
    sel = jnp.min(jnp.where(x == m, pos, float(2**30)), axis=-2, keepdims=True)
    return m, sel, jnp.where(pos == sel, NEG_INF, x)


def _topk_rows(x, pos, k):
    vals, ids = [], []
    for _ in range(k):
        m, sel, x = _extract_max(x, pos)
        vals.append(m)
        ids.append(sel)
    return jnp.concatenate(vals, axis=-2), jnp.concatenate(ids, axis=-2)


def _lookup_rows(table, idx):
    out = jnp.zeros(idx.shape, table.dtype)
    for a in range(table.shape[0]):
        out = jnp.where(idx == a, table[a:a + 1, :], out)
    return out


def _key_positions():
    return lax.broadcasted_iota(jnp.int32, (PEER_NKEYS, V7X_LANES), 0).astype(F32)


def _pair_candidates(v1, v2, rows):
    k = PEER_TOPK
    n = V7X_LANES
    r16 = lax.broadcasted_iota(jnp.int32, (k, n), 0)
    r8 = lax.broadcasted_iota(jnp.int32, (V7X_SUBLANES, n), 0)
    cand, pos = [v1[0:1, :] + v2], [r16]
    for a in range(1, V7X_SUBLANES):
        piece = v1[a:a + 1, :] + v2[0:V7X_SUBLANES, :]
        cand.append(jnp.where(r8 < k // (a + 1), piece, NEG_INF))
        pos.append(a * k + r8)
    cand.append(v1[V7X_SUBLANES:, :] + v2[0:1, :])
    pos.append((V7X_SUBLANES + r8) * k)
    pad = rows - sum(c.shape[0] for c in cand)
    if pad > 0:
        cand.append(jnp.full((pad, n), NEG_INF, F32))
        pos.append(jnp.zeros((pad, n), jnp.int32))
    return jnp.concatenate(cand, axis=0), jnp.concatenate(pos, axis=0).astype(F32)


def _emit_lists(top_s, top_pos, n1, n2, h, grp, l1_ref, l2_ref, lg_ref):
    k = PEER_TOPK
    top_pos = top_pos.astype(jnp.int32)
    e1 = _lookup_rows(n1, lax.shift_right_logical(top_pos, k.bit_length() - 1))
    e2 = _lookup_rows(n2, top_pos & (k - 1))
    ex = jnp.exp(top_s - top_s[0:1, :])
    gate = ex / jnp.sum(ex, axis=0, keepdims=True)
    rows = pl.ds(pl.multiple_of(h * k, k), k)
    l1_ref[grp, rows, :] = e1.astype(jnp.int32)
    l2_ref[grp, rows, :] = e2.astype(jnp.int32)
    lg_ref[grp, rows, :] = gate


def _transpose_lists(src_refs, dst_refs):
    for src, dst in zip(src_refs, dst_refs):
        for grp in range(src.shape[0]):
            dst[grp * V7X_LANES:(grp + 1) * V7X_LANES, :] = src[grp].T


def _peer_topk_kernel(sc_ref, i1_ref, i2_ref, g_ref, l1_ref, l2_ref, lg_ref, hv_ref, hn_ref):
    n_heads = sc_ref.shape[0] // 2

    def unit(n, carry):
        h, grp = n % n_heads, n // n_heads
        x = jnp.stack([sc_ref[2 * h, grp], sc_ref[2 * h + 1, grp]])
        hv_ref[...], hn_ref[...] = _topk_rows(x, _key_positions(), PEER_TOPK)
        cand, cand_pos = _pair_candidates(hv_ref[0], hv_ref[1], 0)
        top_s, top_pos = _topk_rows(cand, cand_pos, PEER_TOPK)
        _emit_lists(top_s, top_pos, hn_ref[0], hn_ref[1], h, grp, l1_ref, l2_ref, lg_ref)
        return carry

    lax.fori_loop(0, n_heads * sc_ref.shape[1], unit, 0)
    _transpose_lists((l1_ref, l2_ref, lg_ref), (i1_ref, i2_ref, g_ref))


def _topk_scratch(n_groups, nj, slots=()):
    lists = (n_groups, nj, V7X_LANES)
    halves = (*slots, 2, PEER_TOPK, V7X_LANES)
    return [pltpu.VMEM(lists, jnp.int32), pltpu.VMEM(lists, jnp.int32), pltpu.VMEM(lists, F32),
            pltpu.VMEM(halves, F32), pltpu.VMEM(halves, F32)]


def peer_topk(sc, n_tokens):
    n_hp, _, nk, _ = sc.shape
    grp = n_tokens // V7X_LANES
    nj = (n_hp // 2) * PEER_TOPK
    out_spec = pl.BlockSpec((n_tokens, nj), lambda i: (0, 0))
    return pl.pallas_call(
        _peer_topk_kernel,
        out_shape=(jax.ShapeDtypeStruct((n_tokens, nj), jnp.int32), jax.ShapeDtypeStruct((n_tokens, nj), jnp.int32),
                   jax.ShapeDtypeStruct((n_tokens, nj), F32)),
        grid=(1,),
        in_specs=[pl.BlockSpec((n_hp, grp, nk, V7X_LANES), lambda i: (0, 0, 0, 0))],
        out_specs=(out_spec, out_spec, out_spec),
        scratch_shapes=_topk_scratch(grp, nj),
        compiler_params=_params(1),
        name="peer_topk",
    )(sc)


def _peer_mix_kernel(h_ref, xn_ref, f1_ref, f2_ref, fg_ref, sc_ref, ua_ref, ub_ref, va_ref, vb_ref, o_ref, w_ref,
                     ha_ref, hb_ref, acc_ref, i1_ref, i2_ref, g_ref, l1_ref, l2_ref, lg_ref, hv_ref, hn_ref, *, unroll,
                     n_chunks, units_per_step):
    tile = pl.program_id(0)
    c = pl.program_id(1)
    last = pl.num_programs(1) - 1
    tm = xn_ref.shape[0]
    nk = PEER_NKEYS
    half = nk // 2
    next_lists = (l1_ref, l2_ref, lg_ref)
    lists = (i1_ref, i2_ref, g_ref)

    @pl.when((c == 0) & (tile == 0))
    def _():
        for src, dst in zip((f1_ref, f2_ref, fg_ref), lists):
            dst[...] = src[...]
        hv_ref[...] = jnp.zeros_like(hv_ref)
        hn_ref[...] = jnp.zeros_like(hn_ref)

    @pl.when((c == 0) & (tile > 0))
    def _():
        _transpose_lists(next_lists, lists)

    @pl.when(c == 0)
    def _():
        acc_ref[...] = jnp.zeros_like(acc_ref)
        hb_ref[...] = jnp.zeros_like(hb_ref)
        r = lax.broadcasted_iota(jnp.int32, (nk, nk), 0).astype(F32).astype(jnp.bfloat16)
        one = jnp.ones((nk, nk), jnp.bfloat16)
        zero = jnp.zeros((nk, nk), jnp.bfloat16)

        def tok(t, carry):
            i1 = i1_ref[pl.ds(t, 1), :].astype(F32).astype(jnp.bfloat16)
            i2 = i2_ref[pl.ds(t, 1), :].astype(F32).astype(jnp.bfloat16)
            g = (0.5 * g_ref[pl.ds(t, 1), :]).astype(jnp.bfloat16)
            a = jnp.where(r == i1, g, zero)
            b = jnp.where(r == i2, one, zero)
            w = lax.dot_general(a, b, NT_DIMS, preferred_element_type=F32).astype(jnp.bfloat16)
            w_ref[pl.ds(pl.multiple_of(t * W_PITCH, V7X_SUBLANES), half), :] = pltpu.bitcast(w, jnp.uint32)
            return carry

        lax.fori_loop(0, tm, tok, 0, unroll=unroll)

    def hidden(u_ref):
        return jnp.dot(xn_ref[...], u_ref[...], preferred_element_type=F32)

    def weighted(hid_ref, chunk, valid):
        pairs = PEER_ECHUNK // (2 * nk)
        first = jnp.clip(chunk, 0, n_chunks - 1) * pairs
        hid = hid_ref[...]
        act = hid * (1.0 + lax.erf(hid * (2.0 ** -0.5)))
        cols = []
        for q in range(pairs):
            packed = w_ref[pl.ds(first + q, tm, stride=W_PITCH), :]
            if valid is not None:
                packed = jnp.where(valid, packed, jnp.uint32(0))
            w_even = lax.bitcast_convert_type(packed << 16, F32)
            w_odd = lax.bitcast_convert_type(packed & jnp.uint32(0xFFFF0000), F32)
            cols += [w_even * act[:, 2 * q * nk:(2 * q + 1) * nk], w_odd * act[:, (2 * q + 1) * nk:(2 * q + 2) * nk]]
        return _mxu(jnp.concatenate(cols, axis=1))

    def accumulate(p, v_ref):
        acc_ref[...] += jnp.dot(p, v_ref[...], preferred_element_type=F32)

    n_heads = sc_ref.shape[0] // 2
    n_units = n_heads * sc_ref.shape[1]
    xs, cands, cand_poss, prev = [], [], [], []
    for s in range(units_per_step):
        n = jnp.minimum(c * units_per_step + s, n_units - 1)
        h, grp = n % n_heads, n // n_heads
        xs += [sc_ref[2 * h, grp], sc_ref[2 * h + 1, grp]]
        cand, cand_pos = _pair_candidates(hv_ref[s, 0], hv_ref[s, 1], 0)
        cands.append(cand)
        cand_poss.append(cand_pos)
        n = jnp.clip((c - 1) * units_per_step + s, 0, n_units - 1)
        prev.append((n % n_heads, n // n_heads, hn_ref[s, 0], hn_ref[s, 1]))
    state = [[jnp.stack(xs), _key_positions(), [], []], [jnp.stack(cands), jnp.stack(cand_poss), [], []]]

    def select(n_steps):
        for _ in range(n_steps):
            for st in state:
                m, sel, st[0] = _extract_max(st[0], st[1])
                st[2].append(m)
                st[3].append(sel)

    part = PEER_TOPK // 8
    select(part)
    ha_ref[...] = hidden(ua_ref)
    select(part)
    p_prev = weighted(hb_ref, 2 * c - 1, None)
    select(part)
    accumulate(p_prev, va_ref)
    select(part)
    hb_ref[...] = hidden(ub_ref)
    select(part)
    p_cur = weighted(ha_ref, 2 * c, c < last)
    select(part)
    accumulate(p_cur, vb_ref)
    select(PEER_TOPK - 6 * part)

    (half_v, half_n), (pair_v, pair_pos) = [
        (jnp.concatenate(st[2], axis=-2), jnp.concatenate(st[3], axis=-2)) for st in state]
    for s, (h, grp, n1, n2) in enumerate(prev):
        hv_ref[s] = half_v[2 * s:2 * s + 2]
        hn_ref[s] = half_n[2 * s:2 * s + 2]
        _emit_lists(pair_v[s], pair_pos[s], n1, n2, h, grp, *next_lists)

    @pl.when(c == last)
    def _():
        o_ref[...] = h_ref[...] + acc_ref[...]


def peer_mix(h, xn, sc, first_lists, u, v, layer):
    t, d = h.shape
    tm = first_lists[0].shape[0]
    nj = first_lists[0].shape[1]
    n_tiles = t // tm
    grp = tm // V7X_LANES
    n_hp = sc.shape[0]
    n_chunks = v.shape[1] // PEER_ECHUNK
    steps = n_chunks // 2 + 1
    n_units = (n_hp // 2) * grp
    units_per_step = -(-n_units // (steps - 1))
    tok_spec = pl.BlockSpec((tm, d), lambda i, c: (i, 0))
    first_spec = pl.BlockSpec((tm, nj), lambda i, c: (0, 0))
    sc_spec = pl.BlockSpec((n_hp, grp, PEER_NKEYS, V7X_LANES), lambda i, c: (0, jnp.minimum(i + 1, n_tiles - 1), 0, 0))

    def chunk(c, offset):
        return jnp.clip(2 * c + offset, 0, n_chunks - 1)

    def u_spec(offset):
        return pl.BlockSpec((None, None, d, PEER_ECHUNK), lambda i, c: (layer, chunk(c, offset), 0, 0))

    def v_spec(offset):
        return pl.BlockSpec((None, PEER_ECHUNK, d), lambda i, c: (layer, chunk(c, offset), 0))

    return pl.pallas_call(
        functools.partial(_peer_mix_kernel, unroll=64, n_chunks=n_chunks, units_per_step=units_per_step),
        out_shape=jax.ShapeDtypeStruct((t, d), F32),
        grid=(n_tiles, steps),
        in_specs=[tok_spec, tok_spec, first_spec, first_spec, first_spec, sc_spec, u_spec(0), u_spec(1),
                  v_spec(-1), v_spec(0)],
        out_specs=tok_spec,
        scratch_shapes=[
            pltpu.VMEM((tm * W_PITCH, PEER_NKEYS), jnp.uint32),
            pltpu.VMEM((tm, PEER_ECHUNK), F32),
            pltpu.VMEM((tm, PEER_ECHUNK), F32),
            pltpu.VMEM((tm, d), F32),
            pltpu.VMEM((tm, nj), jnp.int32),
            pltpu.VMEM((tm, nj), jnp.int32),
            pltpu.VMEM((tm, nj), F32),
            *_topk_scratch(grp, nj, slots=(units_per_step,)),
        ],
        compiler_params=_params(2),
        name="peer_mix",
    )(h, xn, *first_lists, sc, u, u, v, v)


def _final_norm_kernel(h_ref, g_ref, o_ref):
    o_ref[...] = _rms(h_ref[...], g_ref[...])


def final_rmsnorm(h, g):
    t, d = h.shape
    tm = OUT_TM
    return pl.pallas_call(
        _final_norm_kernel,
        out_shape=jax.ShapeDtypeStruct((t, d), F32),
        grid=(t // tm,),
        in_specs=[pl.BlockSpec((tm, d), lambda i: (i, 0)), pl.BlockSpec((1, d), lambda i: (0, 0))],
        out_specs=pl.BlockSpec((tm, d), lambda i: (i, 0)),
        compiler_params=_params(1),
        name="final_norm",
    )(h, g.reshape(1, d))


def _rope_tables(seq):
    half = HEAD_DIM // 2
    inv = ROPE_THETA ** (-jnp.arange(half, dtype=F32) / half)
    ang = jnp.arange(seq, dtype=F32)[:, None] * inv[None, :]
    cos, sin = jnp.cos(ang), jnp.sin(ang)
    return jnp.concatenate([cos, cos], axis=1), jnp.concatenate([-sin, sin], axis=1)


def kernel(x, norm_mix, norm_ffn, even_w_in, even_w_out, odd_w_in, odd_conv, odd_w_out, peer_w_q, peer_sub_keys,
           peer_u, peer_v, final_norm):
    batch, seq, d = x.shape
    depth = norm_mix.shape[0]
    cos_t, sin_t = _rope_tables(seq)
    ret_tables = _retention_tables()
    mh, rh = MOBA_HEADS, RET_HEADS
    rope_blocks = tuple(range(0, 2 * mh)) + tuple(range(3 * mh, 3 * mh + 2 * rh))
    ret_k_blocks = tuple(range(3 * mh + rh, 3 * mh + 2 * rh))

    even_in, even_out_w = cast_weights(even_w_in), cast_weights(even_w_out)
    odd_in, odd_out_w = cast_weights(odd_w_in), cast_weights(odd_w_out)
    w_q, u_t, v = cast_weights(peer_w_q), cast_weights(peer_u, slab=PEER_ECHUNK), cast_weights(peer_v)
    keys = _mxu(peer_sub_keys.reshape(depth, 2 * PEER_HEADS, PEER_NKEYS, PEER_HALF))

    h = x.reshape(batch * seq, d)
    for layer in range(depth):
        i = layer // 2
        if layer % 2 == 0:
            proj = norm_proj(h, norm_mix[layer], even_in, i, cos_t, sin_t, seq, rope_blocks, ret_k_blocks,
                             HEAD_DIM ** -0.5)
            mo_lo, mo_hi = moba(proj, batch, seq)
            ro = retention(proj, ret_tables, batch, seq)
            h = even_out(h, mo_lo, mo_hi, ro, even_out_w, i, seq)
        else:
            proj = norm_proj(h, norm_mix[layer], odd_in, i, cos_t, sin_t, seq)
            h = conv_out(h, proj, odd_conv[i], odd_out_w, i, seq)
        xn, sc = peer_scores(h, norm_ffn[layer], w_q, layer, keys[layer])
        first_lists = peer_topk(sc, min(PEER_TM, h.shape[0]))
        h = peer_mix(h, xn, sc, first_lists, u_t, v, layer)
    return final_rmsnorm(h, final_norm).reshape(batch, seq, d)
```

```python
import functools

import jax
import jax.numpy as jnp
import numpy as np
from jax import lax
from jax.experimental import pallas as pl
from jax.experimental.pallas import tpu as pltpu

HEAD_DIM = 128
MOBA_HEADS = 4
RET_HEADS = 4
MOBA_BLOCK = 256
MOBA_TOPK = 3
RET_CHUNK = 128
PEER_HEADS = 8
PEER_NKEYS = 128
PEER_TOPK = 16
PEER_HALF = 128
ROPE_THETA = 10000.0
EPS = 1e-6

V7X_LANES = 128
V7X_SUBLANES = 8
V7X_VMEM_BYTES = 64 * 2**20
VMEM_LIMIT = V7X_VMEM_BYTES - 8 * 2**20

MXU_DTYPE = jnp.bfloat16
F32 = jnp.float32
NEG_INF = float("-inf")
NT_DIMS = (((1,), (1,)), ((), ()))
TN_DIMS = (((0,), (0,)), ((), ()))

CAST_ROWS = 512
PROJ_TM = 256
PROJ_COLS = 512
OUT_TM = 512
CONV_TM = 256
PEER_Q_TM = 256
PEER_TM = 512
PEER_ECHUNK = 2 * PEER_NKEYS
SCATTER_UNROLL = 64
W_PITCH = 72


def _params(n_axes):
    return pltpu.CompilerParams(dimension_semantics=("arbitrary",) * n_axes, vmem_limit_bytes=VMEM_LIMIT)


def _mxu(x):
    return x.astype(MXU_DTYPE)


def _rms(x, g):
    return x * lax.rsqrt(jnp.mean(x * x, axis=-1, keepdims=True) + EPS) * g


def _layer_spec(stack, layer):
    return pl.BlockSpec((None, *stack.shape[1:]), lambda *_: (layer, 0, 0))


def _cast_kernel(w_ref, o_ref, *, slab):
    if slab is None:
        o_ref[...] = _mxu(w_ref[...])
    else:
        for q in range(o_ref.shape[0]):
            o_ref[q] = _mxu(w_ref[q * slab:(q + 1) * slab, :].T)


def cast_weights(w, slab=None):
    n_layers, rows, cols = w.shape
    tr = min(rows, CAST_ROWS)
    if slab is None:
        out_shape, out_block, out_map = w.shape, (None, tr, cols), lambda l, r: (l, r, 0)
    else:
        out_shape = (n_layers, rows // slab, cols, slab)
        out_block, out_map = (None, tr // slab, cols, slab), lambda l, r: (l, r, 0, 0)
    return pl.pallas_call(
        functools.partial(_cast_kernel, slab=slab),
        out_shape=jax.ShapeDtypeStruct(out_shape, MXU_DTYPE),
        grid=(n_layers, rows // tr),
        in_specs=[pl.BlockSpec((None, tr, cols), lambda l, r: (l, r, 0))],
        out_specs=pl.BlockSpec(out_block, out_map),
        compiler_params=_params(2),
        name="cast_weights",
    )(w)


def _norm_proj_kernel(h_ref, g_ref, w_ref, cos_ref, sin_ref, o_ref, *, rope_blocks, scale_blocks, scale):
    xn = _mxu(_rms(h_ref[...], g_ref[...]))
    n = o_ref.shape[1]
    for c0 in range(0, n, PROJ_COLS):
        y = jnp.dot(xn, w_ref[:, c0:c0 + PROJ_COLS], preferred_element_type=F32)
        for u in range(PROJ_COLS // HEAD_DIM):
            blk = (c0 // HEAD_DIM) + u
            t = y[:, u * HEAD_DIM:(u + 1) * HEAD_DIM]
            if blk in rope_blocks:
                t = t * cos_ref[...] + pltpu.roll(t, HEAD_DIM // 2, axis=1) * sin_ref[...]
            if blk in scale_blocks:
                t = t * scale
            o_ref[:, blk * HEAD_DIM:(blk + 1) * HEAD_DIM] = t


def norm_proj(h, g, w, layer, cos_t, sin_t, seq, rope_blocks=(), scale_blocks=(), scale=1.0):
    t, d = h.shape
    n = w.shape[2]
    tm = PROJ_TM
    per_seq = seq // tm
    return pl.pallas_call(
        functools.partial(_norm_proj_kernel, rope_blocks=frozenset(rope_blocks), scale_blocks=frozenset(scale_blocks),
                          scale=scale),
        out_shape=jax.ShapeDtypeStruct((t, n), F32),
        grid=(t // tm,),
        in_specs=[
            pl.BlockSpec((tm, d), lambda i: (i, 0)),
            pl.BlockSpec((1, d), lambda i: (0, 0)),
            _layer_spec(w, layer),
            pl.BlockSpec((tm, HEAD_DIM), lambda i: (i % per_seq, 0)),
            pl.BlockSpec((tm, HEAD_DIM), lambda i: (i % per_seq, 0)),
        ],
        out_specs=pl.BlockSpec((tm, n), lambda i: (i, 0)),
        compiler_params=_params(1),
        name="norm_proj",
    )(h, g.reshape(1, d), w, cos_t, sin_t)


def _moba_kernel(qa_ref, qb_ref, k_ref, v_ref, oa_ref, ob_ref, kb_ref, vt_ref, km_ref, q2_ref, sel_ref, s_ref,
                 acc_ref, l_ref, *, nb, nbp):
    i = pl.program_id(2)
    blk = MOBA_BLOCK
    own = (i, nb - 1 - i)

    @pl.when(i == 0)
    def _():
        km_ref[...] = jnp.zeros_like(km_ref)
        for j in range(nb):
            rows = slice(j * blk, (j + 1) * blk)
            kb_ref[j] = _mxu(k_ref[rows, :])
            vt_ref[j] = _mxu(v_ref[rows, :].T)
            km_ref[j:j + 1, :] = jnp.sum(k_ref[rows, :], axis=0, keepdims=True) / float(blk)

    sub = lax.broadcasted_iota(jnp.int32, (nbp, blk), 0)
    kpos = lax.broadcasted_iota(jnp.int32, (blk, blk), 0)
    qpos = lax.broadcasted_iota(jnp.int32, (blk, blk), 1)

    def scores_t(j, w):
        return lax.dot_general(kb_ref[j], q2_ref[w], NT_DIMS, preferred_element_type=F32)

    m = []
    for w, q_ref in enumerate((qa_ref, qb_ref)):
        q = q_ref[...]
        q2_ref[w] = _mxu(q)
        gate = lax.dot_general(km_ref[...], q, NT_DIMS, precision=lax.Precision.HIGHEST, preferred_element_type=F32)
        rank = jnp.zeros((nbp, blk), jnp.int32)
        for jp in range(nb):
            other = gate[jp:jp + 1, :]
            beats = jnp.where(other > gate, 1, jnp.where(other == gate, jnp.where(jp < sub, 1, 0), 0))
            rank = rank + jnp.where(jp < own[w], beats, 0)
        sel_ref[w] = jnp.where(sub < own[w], jnp.where(rank < MOBA_TOPK, 1.0, 0.0), 0.0)
        s = jnp.where(kpos <= qpos, scores_t(own[w], w), NEG_INF)
        s_ref[nb - 1 + w] = s
        m.append(jnp.max(s, axis=0, keepdims=True))

    def past_pair(t):
        is_b = t >= i
        return is_b, jnp.where(is_b, 1, 0), jnp.where(is_b, t - i, t)

    for t in range(nb - 1):
        is_b, w, j = past_pair(t)
        s = jnp.where(sel_ref[w, pl.ds(j, 1), :] > 0.0, scores_t(j, w), NEG_INF)
        s_ref[t] = s
        mt = jnp.max(s, axis=0, keepdims=True)
        m = [jnp.where(is_b, m[0], jnp.maximum(m[0], mt)), jnp.where(is_b, jnp.maximum(m[1], mt), m[1])]

    c_exp = (HEAD_DIM ** -0.5) * float(np.log2(np.e))

    def weigh(slot, j, mq):
        p = jnp.exp2((s_ref[slot] - mq) * c_exp)
        return jnp.sum(p, axis=0, keepdims=True), jnp.dot(vt_ref[j], _mxu(p), preferred_element_type=F32)

    for w in range(2):
        l_ref[w], acc_ref[w] = weigh(nb - 1 + w, own[w], m[w])
    for t in range(nb - 1):
        is_b, w, j = past_pair(t)
        lt, at = weigh(t, j, jnp.where(is_b, m[1], m[0]))
        l_ref[w] += lt
        acc_ref[w] += at
    oa_ref[...] = (acc_ref[0] / l_ref[0]).T
    ob_ref[...] = (acc_ref[1] / l_ref[1]).T


def moba(proj, batch, seq):
    t = proj.shape[0]
    nb = seq // MOBA_BLOCK
    nbp = max(nb, V7X_SUBLANES)
    hd = HEAD_DIM
    assert nb % 2 == 0
    nh = nb // 2
    out = jax.ShapeDtypeStruct((t // 2, MOBA_HEADS * hd), F32)
    out_spec = pl.BlockSpec((MOBA_BLOCK, hd), lambda b, h, i: (b * nh + i, h))
    return pl.pallas_call(
        functools.partial(_moba_kernel, nb=nb, nbp=nbp),
        out_shape=(out, out),
        grid=(batch, MOBA_HEADS, nh),
        in_specs=[
            pl.BlockSpec((MOBA_BLOCK, hd), lambda b, h, i: (b * nb + i, h)),
            pl.BlockSpec((MOBA_BLOCK, hd), lambda b, h, i: (b * nb + nb - 1 - i, h)),
            pl.BlockSpec((seq, hd), lambda b, h, i: (b, MOBA_HEADS + h)),
            pl.BlockSpec((seq, hd), lambda b, h, i: (b, 2 * MOBA_HEADS + h)),
        ],
        out_specs=(out_spec, out_spec),
        scratch_shapes=[
            pltpu.VMEM((nb, MOBA_BLOCK, hd), MXU_DTYPE),
            pltpu.VMEM((nb, hd, MOBA_BLOCK), MXU_DTYPE),
            pltpu.VMEM((nbp, hd), F32),
            pltpu.VMEM((2, MOBA_BLOCK, hd), MXU_DTYPE),
            pltpu.VMEM((2, nbp, MOBA_BLOCK), F32),
            pltpu.VMEM((nb + 1, MOBA_BLOCK, MOBA_BLOCK), F32),
            pltpu.VMEM((2, hd, MOBA_BLOCK), F32),
            pltpu.VMEM((2, 1, MOBA_BLOCK), F32),
        ],
        compiler_params=_params(3),
        name="moba",
    )(proj, proj, proj, proj)


def _retention_kernel(q_ref, k_ref, v_ref, g_ref, dm_ref, xi_ref, ze_ref, gc_ref, o_ref, st_ref):
    @pl.when(pl.program_id(0) == 0)
    def _():
        st_ref[...] = jnp.zeros_like(st_ref)

    for b in range(q_ref.shape[0]):
        for h in range(RET_HEADS):
            sl = slice(h * HEAD_DIM, (h + 1) * HEAD_DIM)
            q = _mxu(q_ref[b, :, sl])
            k = k_ref[b, :, sl]
            v = _mxu(v_ref[b, :, sl])
            st = st_ref[b, h]
            inner = lax.dot_general(q, _mxu(k), NT_DIMS, preferred_element_type=F32) * dm_ref[h]
            o = (jnp.dot(_mxu(inner), v, preferred_element_type=F32)
                 + jnp.dot(q, _mxu(st), preferred_element_type=F32) * xi_ref[h])
            st_ref[b, h] = (st * gc_ref[h]
                            + lax.dot_general(_mxu(k * ze_ref[h]), v, TN_DIMS, preferred_element_type=F32))
            o = o * lax.rsqrt(jnp.mean(o * o, axis=-1, keepdims=True) + EPS)
            gate = g_ref[b, :, sl]
            o_ref[b, :, sl] = o * (gate * jax.nn.sigmoid(gate))


def retention(proj, tables, batch, seq):
    t, n = proj.shape
    nc = seq // RET_CHUNK
    w = RET_HEADS * HEAD_DIM
    base = 3 * MOBA_HEADS * HEAD_DIM // w
    tab_spec = pl.BlockSpec((RET_HEADS, RET_CHUNK, HEAD_DIM), lambda c: (0, 0, 0))
    proj3 = proj.reshape(batch, seq, n)

    def col_spec(block):
        return pl.BlockSpec((batch, RET_CHUNK, w), lambda c: (0, c, block))

    out = pl.pallas_call(
        _retention_kernel,
        out_shape=jax.ShapeDtypeStruct((batch, seq, w), F32),
        grid=(nc,),
        in_specs=[col_spec(base), col_spec(base + 1), col_spec(base + 2), col_spec(base + 3),
                  tab_spec, tab_spec, tab_spec, tab_spec],
        out_specs=col_spec(0),
        scratch_shapes=[pltpu.VMEM((batch, RET_HEADS, HEAD_DIM, HEAD_DIM), F32)],
        compiler_params=_params(1),
        name="retention",
    )(proj3, proj3, proj3, proj3, *tables)
    return out.reshape(t, w)


def _retention_tables():
    c = RET_CHUNK
    log_g = jnp.log(1.0 - 2.0 ** (-5.0 - jnp.arange(RET_HEADS, dtype=F32)))
    idx = jnp.arange(c, dtype=F32)
    diff = idx[:, None] - idx[None, :]
    dmask = jnp.where(diff >= 0, jnp.exp(log_g[:, None, None] * jnp.maximum(diff, 0.0)), 0.0)
    xi = jnp.exp(log_g[:, None] * (idx + 1.0))[..., None]
    zeta = jnp.exp(log_g[:, None] * (c - 1.0 - idx))[..., None]
    gc = jnp.exp(log_g * c)[:, None, None]
    full = (RET_HEADS, c, HEAD_DIM)
    return (dmask, jnp.broadcast_to(xi, full), jnp.broadcast_to(zeta, full), jnp.broadcast_to(gc, full))


def _even_out_kernel(h_ref, lo_ref, hi_ref, r_ref, w_ref, o_ref, *, nb):
    in_lo = (pl.program_id(0) % nb) < nb // 2
    mo = jnp.where(in_lo, lo_ref[...], hi_ref[...])
    y = jnp.concatenate([_mxu(mo), _mxu(r_ref[...])], axis=1)
    o_ref[...] = h_ref[...] + jnp.dot(y, w_ref[...], preferred_element_type=F32)


def even_out(h, mo_lo, mo_hi, ro, w, layer, seq):
    t, d = h.shape
    tm = MOBA_BLOCK
    nb = seq // tm
    nh = nb // 2

    def lo_map(g):
        return ((g // nb) * nh + jnp.minimum(g % nb, nh - 1), 0)

    def hi_map(g):
        return ((g // nb) * nh + (nb - 1 - jnp.maximum(g % nb, nh)), 0)

    return pl.pallas_call(
        functools.partial(_even_out_kernel, nb=nb),
        out_shape=jax.ShapeDtypeStruct((t, d), F32),
        grid=(t // tm,),
        in_specs=[
            pl.BlockSpec((tm, d), lambda g: (g, 0)),
            pl.BlockSpec((tm, mo_lo.shape[1]), lo_map),
            pl.BlockSpec((tm, mo_hi.shape[1]), hi_map),
            pl.BlockSpec((tm, ro.shape[1]), lambda g: (g, 0)),
            _layer_spec(w, layer),
        ],
        out_specs=pl.BlockSpec((tm, d), lambda g: (g, 0)),
        compiler_params=_params(1),
        name="even_out",
    )(h, mo_lo, mo_hi, ro, w)


def _conv_out_kernel(h_ref, bg_ref, cg_ref, hx_ref, cgp_ref, hxp_ref, cw_ref, w_ref, o_ref, *, per_seq):
    i = pl.program_id(0)
    u = cg_ref[...] * hx_ref[...]
    up = jnp.where(i % per_seq == 0, 0.0, cgp_ref[...] * hxp_ref[...])
    row = lax.broadcasted_iota(jnp.int32, u.shape, 0)
    p1 = up[V7X_SUBLANES - 1:V7X_SUBLANES, :]
    p2 = up[V7X_SUBLANES - 2:V7X_SUBLANES - 1, :]
    u1 = jnp.where(row == 0, p1, pltpu.roll(u, 1, axis=0))
    u2 = jnp.where(row == 0, p2, jnp.where(row == 1, p1, pltpu.roll(u, 2, axis=0)))
    y = cw_ref[0:1, :] * u2 + cw_ref[1:2, :] * u1 + cw_ref[2:3, :] * u
    z = _mxu(bg_ref[...] * y)
    o_ref[...] = h_ref[...] + jnp.dot(z, w_ref[...], preferred_element_type=F32)


def conv_out(h, proj, conv_w, w, layer, seq):
    t, d = h.shape
    tm = CONV_TM
    per_seq = seq // tm
    halo = tm // V7X_SUBLANES
    return pl.pallas_call(
        functools.partial(_conv_out_kernel, per_seq=per_seq),
        out_shape=jax.ShapeDtypeStruct((t, d), F32),
        grid=(t // tm,),
        in_specs=[
            pl.BlockSpec((tm, d), lambda i: (i, 0)),
            pl.BlockSpec((tm, d), lambda i: (i, 0)),
            pl.BlockSpec((tm, d), lambda i: (i, 1)),
            pl.BlockSpec((tm, d), lambda i: (i, 2)),
            pl.BlockSpec((V7X_SUBLANES, d), lambda i: (jnp.maximum(i * halo - 1, 0), 1)),
            pl.BlockSpec((V7X_SUBLANES, d), lambda i: (jnp.maximum(i * halo - 1, 0), 2)),
            pl.BlockSpec(conv_w.shape, lambda i: (0, 0)),
            _layer_spec(w, layer),
        ],
        out_specs=pl.BlockSpec((tm, d), lambda i: (i, 0)),
        compiler_params=_params(1),
        name="conv_out",
    )(h, proj, proj, proj, proj, proj, conv_w, w)


def _peer_scores_kernel(h_ref, g_ref, wq_ref, keys_ref, xn_ref, sc_ref):
    xn = _mxu(_rms(h_ref[...], g_ref[...]))
    xn_ref[...] = xn
    n_hp = keys_ref.shape[0]
    per = PROJ_COLS // PEER_HALF
    for c in range(n_hp // per):
        q = jnp.dot(xn, wq_ref[:, c * PROJ_COLS:(c + 1) * PROJ_COLS], preferred_element_type=F32)
        for u in range(per):
            hp = c * per + u
            qs = _mxu(q[:, u * PEER_HALF:(u + 1) * PEER_HALF])
            sc = lax.dot_general(keys_ref[hp], qs, NT_DIMS, preferred_element_type=F32)
            for grp in range(sc_ref.shape[1]):
                sc_ref[hp, grp] = sc[:, grp * V7X_LANES:(grp + 1) * V7X_LANES]


def peer_scores(h, g, wq, layer, keys):
    t, d = h.shape
    tm = PEER_Q_TM
    n_hp = keys.shape[0]
    grp = tm // V7X_LANES
    return pl.pallas_call(
        _peer_scores_kernel,
        out_shape=(jax.ShapeDtypeStruct((t, d), MXU_DTYPE),
                   jax.ShapeDtypeStruct((n_hp, t // V7X_LANES, PEER_NKEYS, V7X_LANES), F32)),
        grid=(t // tm,),
        in_specs=[
            pl.BlockSpec((tm, d), lambda i: (i, 0)),
            pl.BlockSpec((1, d), lambda i: (0, 0)),
            _layer_spec(wq, layer),
            pl.BlockSpec(keys.shape, lambda i: (0, 0, 0)),
        ],
        out_specs=(pl.BlockSpec((tm, d), lambda i: (i, 0)),
                   pl.BlockSpec((n_hp, grp, PEER_NKEYS, V7X_LANES), lambda i: (0, i, 0, 0))),
        compiler_params=_params(1),
        name="peer_scores",
    )(h, g.reshape(1, d), wq, keys)


def _extract_max(x, pos):
    m = jnp.max(x, axis=-2, keepdims=True)
    sel = jnp.min(jnp.where(x == m, pos, float(2**30)), axis=-2, keepdims=True)
    return m, sel, jnp.where(pos == sel, NEG_INF, x)


def _topk_rows(x, pos, k):
    vals, ids = [], []
    for _ in range(k):
        m, sel, x = _extract_max(x, pos)
        vals.append(m)
        ids.append(sel)
    return jnp.concatenate(vals, axis=-2), jnp.concatenate(ids, axis=-2)


def _lookup_rows(table, idx):
    out = jnp.zeros(idx.shape, table.dtype)
    for a in range(table.shape[0]):
        out = jnp.where(idx == a, table[a:a + 1, :], out)
    return out


def _key_positions():
    return lax.broadcasted_iota(jnp.int32, (PEER_NKEYS, V7X_LANES), 0).astype(F32)


def _pair_candidates(v1, v2):
    k = PEER_TOPK
    n = V7X_LANES
    r16 = lax.broadcasted_iota(jnp.int32, (k, n), 0)
    r8 = lax.broadcasted_iota(jnp.int32, (V7X_SUBLANES, n), 0)
    cand, pos = [v1[0:1, :] + v2], [r16]
    for a in range(1, V7X_SUBLANES):
        piece = v1[a:a + 1, :] + v2[0:V7X_SUBLANES, :]
        cand.append(jnp.where(r8 < k // (a + 1), piece, NEG_INF))
        pos.append(a * k + r8)
    cand.append(v1[V7X_SUBLANES:, :] + v2[0:1, :])
    pos.append((V7X_SUBLANES + r8) * k)
    return jnp.concatenate(cand, axis=0), jnp.concatenate(pos, axis=0).astype(F32)


def _emit_lists(top_s, top_pos, n1, n2, h, grp, l1_ref, l2_ref, lg_ref):
    k = PEER_TOPK
    top_pos = top_pos.astype(jnp.int32)
    e1 = _lookup_rows(n1, lax.shift_right_logical(top_pos, k.bit_length() - 1))
    e2 = _lookup_rows(n2, top_pos & (k - 1))
    ex = jnp.exp(top_s - top_s[0:1, :])
    gate = ex / jnp.sum(ex, axis=0, keepdims=True)
    rows = pl.ds(pl.multiple_of(h * k, k), k)
    l1_ref[grp, rows, :] = e1.astype(jnp.int32)
    l2_ref[grp, rows, :] = e2.astype(jnp.int32)
    lg_ref[grp, rows, :] = gate


def _transpose_lists(src_refs, dst_refs):
    for src, dst in zip(src_refs, dst_refs):
        for grp in range(src.shape[0]):
            dst[grp * V7X_LANES:(grp + 1) * V7X_LANES, :] = src[grp].T


def _peer_topk_kernel(sc_ref, i1_ref, i2_ref, g_ref, l1_ref, l2_ref, lg_ref, hv_ref, hn_ref):
    n_heads = sc_ref.shape[0] // 2

    def unit(n, carry):
        h, grp = n % n_heads, n // n_heads
        x = jnp.stack([sc_ref[2 * h, grp], sc_ref[2 * h + 1, grp]])
        hv_ref[...], hn_ref[...] = _topk_rows(x, _key_positions(), PEER_TOPK)
        cand, cand_pos = _pair_candidates(hv_ref[0], hv_ref[1])
        top_s, top_pos = _topk_rows(cand, cand_pos, PEER_TOPK)
        _emit_lists(top_s, top_pos, hn_ref[0], hn_ref[1], h, grp, l1_ref, l2_ref, lg_ref)
        return carry

    lax.fori_loop(0, n_heads * sc_ref.shape[1], unit, 0)
    _transpose_lists((l1_ref, l2_ref, lg_ref), (i1_ref, i2_ref, g_ref))


def _topk_scratch(n_groups, nj, slots=()):
    lists = (n_groups, nj, V7X_LANES)
    halves = (*slots, 2, PEER_TOPK, V7X_LANES)
    return [pltpu.VMEM(lists, jnp.int32), pltpu.VMEM(lists, jnp.int32), pltpu.VMEM(lists, F32),
            pltpu.VMEM(halves, F32), pltpu.VMEM(halves, F32)]


def peer_topk(sc, n_tokens):
    n_hp, _, nk, _ = sc.shape
    grp = n_tokens // V7X_LANES
    nj = (n_hp // 2) * PEER_TOPK
    out_spec = pl.BlockSpec((n_tokens, nj), lambda i: (0, 0))
    return pl.pallas_call(
        _peer_topk_kernel,
        out_shape=(jax.ShapeDtypeStruct((n_tokens, nj), jnp.int32), jax.ShapeDtypeStruct((n_tokens, nj), jnp.int32),
                   jax.ShapeDtypeStruct((n_tokens, nj), F32)),
        grid=(1,),
        in_specs=[pl.BlockSpec((n_hp, grp, nk, V7X_LANES), lambda i: (0, 0, 0, 0))],
        out_specs=(out_spec, out_spec, out_spec),
        scratch_shapes=_topk_scratch(grp, nj),
        compiler_params=_params(1),
        name="peer_topk",
    )(sc)


def _peer_mix_kernel(h_ref, xn_ref, f1_ref, f2_ref, fg_ref, sc_ref, ua_ref, ub_ref, va_ref, vb_ref, o_ref, w_ref,
                     ha_ref, hb_ref, acc_ref, i1_ref, i2_ref, g_ref, l1_ref, l2_ref, lg_ref, hv_ref, hn_ref, *,
                     n_chunks, units_per_step):
    tile = pl.program_id(0)
    c = pl.program_id(1)
    last = pl.num_programs(1) - 1
    tm = xn_ref.shape[0]
    nk = PEER_NKEYS
    half = nk // 2
    next_lists = (l1_ref, l2_ref, lg_ref)
    lists = (i1_ref, i2_ref, g_ref)

    @pl.when((c == 0) & (tile == 0))
    def _():
        for src, dst in zip((f1_ref, f2_ref, fg_ref), lists):
            dst[...] = src[...]
        hv_ref[...] = jnp.zeros_like(hv_ref)
        hn_ref[...] = jnp.zeros_like(hn_ref)

    @pl.when((c == 0) & (tile > 0))
    def _():
        _transpose_lists(next_lists, lists)

    @pl.when(c == 0)
    def _():
        acc_ref[...] = jnp.zeros_like(acc_ref)
        hb_ref[...] = jnp.zeros_like(hb_ref)
        r = lax.broadcasted_iota(jnp.int32, (nk, nk), 0).astype(F32).astype(jnp.bfloat16)
        one = jnp.ones((nk, nk), jnp.bfloat16)
        zero = jnp.zeros((nk, nk), jnp.bfloat16)

        def one_hots(t):
            i1 = i1_ref[pl.ds(t, 1), :].astype(F32).astype(jnp.bfloat16)
            i2 = i2_ref[pl.ds(t, 1), :].astype(F32).astype(jnp.bfloat16)
            g = (0.5 * g_ref[pl.ds(t, 1), :]).astype(jnp.bfloat16)
            return jnp.where(r == i1, g, zero), jnp.where(r == i2, one, zero)

        def tok_pair(n, carry):
            (a0, b0), (a1, b1) = one_hots(2 * n), one_hots(2 * n + 1)
            w = lax.dot_general(jnp.concatenate([a0, a1], axis=0), jnp.concatenate([b0, b1], axis=0), NT_DIMS,
                                preferred_element_type=F32)
            for k in range(2):
                wk = w[k * nk:(k + 1) * nk, k * nk:(k + 1) * nk].astype(jnp.bfloat16)
                dst = pl.ds(pl.multiple_of((2 * n + k) * W_PITCH, V7X_SUBLANES), half)
                w_ref[dst, :] = pltpu.bitcast(wk, jnp.uint32)
            return carry

        lax.fori_loop(0, tm // 2, tok_pair, 0, unroll=SCATTER_UNROLL // 2)

    def hidden(u_ref):
        return jnp.dot(xn_ref[...], u_ref[...], preferred_element_type=F32)

    def weighted(hid_ref, chunk, valid):
        pairs = PEER_ECHUNK // (2 * nk)
        first = jnp.clip(chunk, 0, n_chunks - 1) * pairs
        hid = hid_ref[...]
        act = hid * (1.0 + lax.erf(hid * (2.0 ** -0.5)))
        cols = []
        for q in range(pairs):
            packed = w_ref[pl.ds(first + q, tm, stride=W_PITCH), :]
            if valid is not None:
                packed = jnp.where(valid, packed, jnp.uint32(0))
            w_even = lax.bitcast_convert_type(packed << 16, F32)
            w_odd = lax.bitcast_convert_type(packed & jnp.uint32(0xFFFF0000), F32)
            cols += [w_even * act[:, 2 * q * nk:(2 * q + 1) * nk], w_odd * act[:, (2 * q + 1) * nk:(2 * q + 2) * nk]]
        return _mxu(jnp.concatenate(cols, axis=1))

    def accumulate(p, v_ref):
        acc_ref[...] += jnp.dot(p, v_ref[...], preferred_element_type=F32)

    n_heads = sc_ref.shape[0] // 2
    n_units = n_heads * sc_ref.shape[1]
    xs, cands, cand_poss, prev = [], [], [], []
    for s in range(units_per_step):
        n = jnp.minimum(c * units_per_step + s, n_units - 1)
        h, grp = n % n_heads, n // n_heads
        xs += [sc_ref[2 * h, grp], sc_ref[2 * h + 1, grp]]
        cand, cand_pos = _pair_candidates(hv_ref[s, 0], hv_ref[s, 1])
        cands.append(cand)
        cand_poss.append(cand_pos)
        n = jnp.clip((c - 1) * units_per_step + s, 0, n_units - 1)
        prev.append((n % n_heads, n // n_heads, hn_ref[s, 0], hn_ref[s, 1]))
    state = [[jnp.stack(xs), _key_positions(), [], []], [jnp.stack(cands), jnp.stack(cand_poss), [], []]]

    def select(n_steps):
        for _ in range(n_steps):
            for st in state:
                m, sel, st[0] = _extract_max(st[0], st[1])
                st[2].append(m)
                st[3].append(sel)

    part = PEER_TOPK // 8
    select(part)
    ha_ref[...] = hidden(ua_ref)
    select(part)
    p_prev = weighted(hb_ref, 2 * c - 1, None)
    select(part)
    accumulate(p_prev, va_ref)
    select(part)
    hb_ref[...] = hidden(ub_ref)
    select(part)
    p_cur = weighted(ha_ref, 2 * c, c < last)
    select(part)
    accumulate(p_cur, vb_ref)
    select(PEER_TOPK - 6 * part)

    (half_v, half_n), (pair_v, pair_pos) = [
        (jnp.concatenate(st[2], axis=-2), jnp.concatenate(st[3], axis=-2)) for st in state]
    for s, (h, grp, n1, n2) in enumerate(prev):
        hv_ref[s] = half_v[2 * s:2 * s + 2]
        hn_ref[s] = half_n[2 * s:2 * s + 2]
        _emit_lists(pair_v[s], pair_pos[s], n1, n2, h, grp, *next_lists)

    @pl.when(c == last)
    def _():
        o_ref[...] = h_ref[...] + acc_ref[...]


def peer_mix(h, xn, sc, first_lists, u, v, layer):
    t, d = h.shape
    tm = first_lists[0].shape[0]
    nj = first_lists[0].shape[1]
    n_tiles = t // tm
    grp = tm // V7X_LANES
    n_hp = sc.shape[0]
    n_chunks = v.shape[1] // PEER_ECHUNK
    steps = n_chunks // 2 + 1
    n_units = (n_hp // 2) * grp
    units_per_step = -(-n_units // (steps - 1))
    tok_spec = pl.BlockSpec((tm, d), lambda i, c: (i, 0))
    first_spec = pl.BlockSpec((tm, nj), lambda i, c: (0, 0))
    sc_spec = pl.BlockSpec((n_hp, grp, PEER_NKEYS, V7X_LANES), lambda i, c: (0, jnp.minimum(i + 1, n_tiles - 1), 0, 0))

    def chunk(c, offset):
        return jnp.clip(2 * c + offset, 0, n_chunks - 1)

    def u_spec(offset):
        return pl.BlockSpec((None, None, d, PEER_ECHUNK), lambda i, c: (layer, chunk(c, offset), 0, 0))

    def v_spec(offset):
        return pl.BlockSpec((None, PEER_ECHUNK, d), lambda i, c: (layer, chunk(c, offset), 0))

    return pl.pallas_call(
        functools.partial(_peer_mix_kernel, n_chunks=n_chunks, units_per_step=units_per_step),
        out_shape=jax.ShapeDtypeStruct((t, d), F32),
        grid=(n_tiles, steps),
        in_specs=[tok_spec, tok_spec, first_spec, first_spec, first_spec, sc_spec, u_spec(0), u_spec(1),
                  v_spec(-1), v_spec(0)],
        out_specs=tok_spec,
        scratch_shapes=[
            pltpu.VMEM((tm * W_PITCH, PEER_NKEYS), jnp.uint32),
            pltpu.VMEM((tm, PEER_ECHUNK), F32),
            pltpu.VMEM((tm, PEER_ECHUNK), F32),
            pltpu.VMEM((tm, d), F32),
            pltpu.VMEM((tm, nj), jnp.int32),
            pltpu.VMEM((tm, nj), jnp.int32),
            pltpu.VMEM((tm, nj), F32),
            *_topk_scratch(grp, nj, slots=(units_per_step,)),
        ],
        compiler_params=_params(2),
        name="peer_mix",
    )(h, xn, *first_lists, sc, u, u, v, v)


def _final_norm_kernel(h_ref, g_ref, o_ref):
    o_ref[...] = _rms(h_ref[...], g_ref[...])


def final_rmsnorm(h, g):
    t, d = h.shape
    tm = OUT_TM
    return pl.pallas_call(
        _final_norm_kernel,
        out_shape=jax.ShapeDtypeStruct((t, d), F32),
        grid=(t // tm,),
        in_specs=[pl.BlockSpec((tm, d), lambda i: (i, 0)), pl.BlockSpec((1, d), lambda i: (0, 0))],
        out_specs=pl.BlockSpec((tm, d), lambda i: (i, 0)),
        compiler_params=_params(1),
        name="final_norm",
    )(h, g.reshape(1, d))


def _rope_tables(seq):
    half = HEAD_DIM // 2
    inv = ROPE_THETA ** (-jnp.arange(half, dtype=F32) / half)
    ang = jnp.arange(seq, dtype=F32)[:, None] * inv[None, :]
    cos, sin = jnp.cos(ang), jnp.sin(ang)
    return jnp.concatenate([cos, cos], axis=1), jnp.concatenate([-sin, sin], axis=1)


def kernel(x, norm_mix, norm_ffn, even_w_in, even_w_out, odd_w_in, odd_conv, odd_w_out, peer_w_q, peer_sub_keys,
           peer_u, peer_v, final_norm):
    batch, seq, d = x.shape
    depth = norm_mix.shape[0]
    cos_t, sin_t = _rope_tables(seq)
    ret_tables = _retention_tables()
    mh, rh = MOBA_HEADS, RET_HEADS
    rope_blocks = tuple(range(0, 2 * mh)) + tuple(range(3 * mh, 3 * mh + 2 * rh))
    ret_k_blocks = tuple(range(3 * mh + rh, 3 * mh + 2 * rh))

    even_in, even_out_w = cast_weights(even_w_in), cast_weights(even_w_out)
    odd_in, odd_out_w = cast_weights(odd_w_in), cast_weights(odd_w_out)
    w_q, u_t, v = cast_weights(peer_w_q), cast_weights(peer_u, slab=PEER_ECHUNK), cast_weights(peer_v)
    keys = _mxu(peer_sub_keys.reshape(depth, 2 * PEER_HEADS, PEER_NKEYS, PEER_HALF))

    h = x.reshape(batch * seq, d)
    for layer in range(depth):
        i = layer // 2
        if layer % 2 == 0:
            proj = norm_proj(h, norm_mix[layer], even_in, i, cos_t, sin_t, seq, rope_blocks, ret_k_blocks,
                             HEAD_DIM ** -0.5)
            mo_lo, mo_hi = moba(proj, batch, seq)
            ro = retention(proj, ret_tables, batch, seq)
            h = even_out(h, mo_lo, mo_hi, ro, even_out_w, i, seq)
        else:
            proj = norm_proj(h, norm_mix[layer], odd_in, i, cos_t, sin_t, seq)
            h = conv_out(h, proj, odd_conv[i], odd_out_w, i, seq)
        xn, sc = peer_scores(h, norm_ffn[layer], w_q, layer, keys[layer])
        first_lists = peer_topk(sc, min(PEER_TM, h.shape[0]))
        h = peer_mix(h, xn, sc, first_lists, u_t, v, layer)
    return final_rmsnorm(h, final_norm).reshape(batch, seq, d)
```

```python
import functools

import jax
import jax.numpy as jnp
import numpy as np
from jax import lax
from jax.experimental import pallas as pl
from jax.experimental.pallas import tpu as pltpu

HEAD_DIM = 128
MOBA_HEADS = 4
RET_HEADS = 4
MOBA_BLOCK = 256
MOBA_TOPK = 3
RET_CHUNK = 128
PEER_HEADS = 8
PEER_NKEYS = 128
PEER_TOPK = 16
PEER_HALF = 128
ROPE_THETA = 10000.0
EPS = 1e-6

V7X_LANES = 128
V7X_SUBLANES = 8
V7X_VMEM_BYTES = 64 * 2**20
VMEM_LIMIT = V7X_VMEM_BYTES - 8 * 2**20

MXU_DTYPE = jnp.bfloat16
F32 = jnp.float32
NEG_INF = float("-inf")
NT_DIMS = (((1,), (1,)), ((), ()))
TN_DIMS = (((0,), (0,)), ((), ()))

CAST_ROWS = 512
PROJ_TM = 256
PROJ_COLS = 512
OUT_TM = 512
CONV_TM = 256
PEER_Q_TM = 256
PEER_TM = 512
PEER_ECHUNK = 2 * PEER_NKEYS
SCATTER_UNROLL = 128
W_PITCH = 72


def _params(n_axes):
    return pltpu.CompilerParams(dimension_semantics=("arbitrary",) * n_axes, vmem_limit_bytes=VMEM_LIMIT)


def _mxu(x):
    return x.astype(MXU_DTYPE)


def _rms(x, g):
    return x * lax.rsqrt(jnp.mean(x * x, axis=-1, keepdims=True) + EPS) * g


def _layer_spec(stack, layer):
    return pl.BlockSpec((None, *stack.shape[1:]), lambda *_: (layer, 0, 0))


def _cast_kernel(w_ref, o_ref, *, slab):
    if slab is None:
        o_ref[...] = _mxu(w_ref[...])
    else:
        for q in range(o_ref.shape[0]):
            o_ref[q] = _mxu(w_ref[q * slab:(q + 1) * slab, :].T)


def cast_weights(w, slab=None):
    n_layers, rows, cols = w.shape
    tr = min(rows, CAST_ROWS)
    if slab is None:
        out_shape, out_block, out_map = w.shape, (None, tr, cols), lambda l, r: (l, r, 0)
    else:
        out_shape = (n_layers, rows // slab, cols, slab)
        out_block, out_map = (None, tr // slab, cols, slab), lambda l, r: (l, r, 0, 0)
    return pl.pallas_call(
        functools.partial(_cast_kernel, slab=slab),
        out_shape=jax.ShapeDtypeStruct(out_shape, MXU_DTYPE),
        grid=(n_layers, rows // tr),
        in_specs=[pl.BlockSpec((None, tr, cols), lambda l, r: (l, r, 0))],
        out_specs=pl.BlockSpec(out_block, out_map),
        compiler_params=_params(2),
        name="cast_weights",
    )(w)


def _norm_proj_kernel(h_ref, g_ref, w_ref, cos_ref, sin_ref, o_ref, *, rope_blocks, scale_blocks, scale):
    xn = _mxu(_rms(h_ref[...], g_ref[...]))
    n = o_ref.shape[1]
    for c0 in range(0, n, PROJ_COLS):
        y = jnp.dot(xn, w_ref[:, c0:c0 + PROJ_COLS], preferred_element_type=F32)
        for u in range(PROJ_COLS // HEAD_DIM):
            blk = (c0 // HEAD_DIM) + u
            t = y[:, u * HEAD_DIM:(u + 1) * HEAD_DIM]
            if blk in rope_blocks:
                t = t * cos_ref[...] + pltpu.roll(t, HEAD_DIM // 2, axis=1) * sin_ref[...]
            if blk in scale_blocks:
                t = t * scale
            o_ref[:, blk * HEAD_DIM:(blk + 1) * HEAD_DIM] = t


def norm_proj(h, g, w, layer, cos_t, sin_t, seq, rope_blocks=(), scale_blocks=(), scale=1.0):
    t, d = h.shape
    n = w.shape[2]
    tm = PROJ_TM
    per_seq = seq // tm
    return pl.pallas_call(
        functools.partial(_norm_proj_kernel, rope_blocks=frozenset(rope_blocks), scale_blocks=frozenset(scale_blocks),
                          scale=scale),
        out_shape=jax.ShapeDtypeStruct((t, n), F32),
        grid=(t // tm,),
        in_specs=[
            pl.BlockSpec((tm, d), lambda i: (i, 0)),
            pl.BlockSpec((1, d), lambda i: (0, 0)),
            _layer_spec(w, layer),
            pl.BlockSpec((tm, HEAD_DIM), lambda i: (i % per_seq, 0)),
            pl.BlockSpec((tm, HEAD_DIM), lambda i: (i % per_seq, 0)),
        ],
        out_specs=pl.BlockSpec((tm, n), lambda i: (i, 0)),
        compiler_params=_params(1),
        name="norm_proj",
    )(h, g.reshape(1, d), w, cos_t, sin_t)


def _moba_kernel(qa_ref, qb_ref, k_ref, v_ref, oa_ref, ob_ref, kb_ref, vt_ref, km_ref, q2_ref, sel_ref, s_ref,
                 acc_ref, l_ref, *, nb, nbp):
    i = pl.program_id(2)
    blk = MOBA_BLOCK
    own = (i, nb - 1 - i)

    @pl.when(i == 0)
    def _():
        km_ref[...] = jnp.zeros_like(km_ref)
        for j in range(nb):
            rows = slice(j * blk, (j + 1) * blk)
            kb_ref[j] = _mxu(k_ref[rows, :])
            vt_ref[j] = _mxu(v_ref[rows, :].T)
            km_ref[j:j + 1, :] = jnp.sum(k_ref[rows, :], axis=0, keepdims=True) / float(blk)

    sub = lax.broadcasted_iota(jnp.int32, (nbp, blk), 0)
    kpos = lax.broadcasted_iota(jnp.int32, (blk, blk), 0)
    qpos = lax.broadcasted_iota(jnp.int32, (blk, blk), 1)

    def scores_t(j, w):
        return lax.dot_general(kb_ref[j], q2_ref[w], NT_DIMS, preferred_element_type=F32)

    m = []
    for w, q_ref in enumerate((qa_ref, qb_ref)):
        q = q_ref[...]
        q2_ref[w] = _mxu(q)
        gate = lax.dot_general(km_ref[...], q, NT_DIMS, precision=lax.Precision.HIGHEST, preferred_element_type=F32)
        rank = jnp.zeros((nbp, blk), jnp.int32)
        for jp in range(nb):
            other = gate[jp:jp + 1, :]
            beats = jnp.where(other > gate, 1, jnp.where(other == gate, jnp.where(jp < sub, 1, 0), 0))
            rank = rank + jnp.where(jp < own[w], beats, 0)
        sel_ref[w] = jnp.where(sub < own[w], jnp.where(rank < MOBA_TOPK, 1.0, 0.0), 0.0)
        s = jnp.where(kpos <= qpos, scores_t(own[w], w), NEG_INF)
        s_ref[nb - 1 + w] = s
        m.append(jnp.max(s, axis=0, keepdims=True))

    def past_pair(t):
        is_b = t >= i
        return is_b, jnp.where(is_b, 1, 0), jnp.where(is_b, t - i, t)

    for t in range(nb - 1):
        is_b, w, j = past_pair(t)
        s = jnp.where(sel_ref[w, pl.ds(j, 1), :] > 0.0, scores_t(j, w), NEG_INF)
        s_ref[t] = s
        mt = jnp.max(s, axis=0, keepdims=True)
        m = [jnp.where(is_b, m[0], jnp.maximum(m[0], mt)), jnp.where(is_b, jnp.maximum(m[1], mt), m[1])]

    c_exp = (HEAD_DIM ** -0.5) * float(np.log2(np.e))

    def weigh(slot, j, mq):
        p = jnp.exp2((s_ref[slot] - mq) * c_exp)
        return jnp.sum(p, axis=0, keepdims=True), jnp.dot(vt_ref[j], _mxu(p), preferred_element_type=F32)

    for w in range(2):
        l_ref[w], acc_ref[w] = weigh(nb - 1 + w, own[w], m[w])
    for t in range(nb - 1):
        is_b, w, j = past_pair(t)
        lt, at = weigh(t, j, jnp.where(is_b, m[1], m[0]))
        l_ref[w] += lt
        acc_ref[w] += at
    oa_ref[...] = (acc_ref[0] / l_ref[0]).T
    ob_ref[...] = (acc_ref[1] / l_ref[1]).T


def moba(proj, batch, seq):
    t = proj.shape[0]
    nb = seq // MOBA_BLOCK
    nbp = max(nb, V7X_SUBLANES)
    hd = HEAD_DIM
    assert nb % 2 == 0
    nh = nb // 2
    out = jax.ShapeDtypeStruct((t // 2, MOBA_HEADS * hd), F32)
    out_spec = pl.BlockSpec((MOBA_BLOCK, hd), lambda b, h, i: (b * nh + i, h))
    return pl.pallas_call(
        functools.partial(_moba_kernel, nb=nb, nbp=nbp),
        out_shape=(out, out),
        grid=(batch, MOBA_HEADS, nh),
        in_specs=[
            pl.BlockSpec((MOBA_BLOCK, hd), lambda b, h, i: (b * nb + i, h)),
            pl.BlockSpec((MOBA_BLOCK, hd), lambda b, h, i: (b * nb + nb - 1 - i, h)),
            pl.BlockSpec((seq, hd), lambda b, h, i: (b, MOBA_HEADS + h)),
            pl.BlockSpec((seq, hd), lambda b, h, i: (b, 2 * MOBA_HEADS + h)),
        ],
        out_specs=(out_spec, out_spec),
        scratch_shapes=[
            pltpu.VMEM((nb, MOBA_BLOCK, hd), MXU_DTYPE),
            pltpu.VMEM((nb, hd, MOBA_BLOCK), MXU_DTYPE),
            pltpu.VMEM((nbp, hd), F32),
            pltpu.VMEM((2, MOBA_BLOCK, hd), MXU_DTYPE),
            pltpu.VMEM((2, nbp, MOBA_BLOCK), F32),
            pltpu.VMEM((nb + 1, MOBA_BLOCK, MOBA_BLOCK), F32),
            pltpu.VMEM((2, hd, MOBA_BLOCK), F32),
            pltpu.VMEM((2, 1, MOBA_BLOCK), F32),
        ],
        compiler_params=_params(3),
        name="moba",
    )(proj, proj, proj, proj)


def _retention_kernel(q_ref, k_ref, v_ref, g_ref, dm_ref, xi_ref, ze_ref, gc_ref, o_ref, st_ref):
    @pl.when(pl.program_id(0) == 0)
    def _():
        st_ref[...] = jnp.zeros_like(st_ref)

    for b in range(q_ref.shape[0]):
        for h in range(RET_HEADS):
            sl = slice(h * HEAD_DIM, (h + 1) * HEAD_DIM)
            q = _mxu(q_ref[b, :, sl])
            k = k_ref[b, :, sl]
            v = _mxu(v_ref[b, :, sl])
            st = st_ref[b, h]
            inner = lax.dot_general(q, _mxu(k), NT_DIMS, preferred_element_type=F32) * dm_ref[h]
            o = (jnp.dot(_mxu(inner), v, preferred_element_type=F32)
                 + jnp.dot(q, _mxu(st), preferred_element_type=F32) * xi_ref[h])
            st_ref[b, h] = (st * gc_ref[h]
                            + lax.dot_general(_mxu(k * ze_ref[h]), v, TN_DIMS, preferred_element_type=F32))
            o = o * lax.rsqrt(jnp.mean(o * o, axis=-1, keepdims=True) + EPS)
            gate = g_ref[b, :, sl]
            o_ref[b, :, sl] = o * (gate * jax.nn.sigmoid(gate))


def retention(proj, tables, batch, seq):
    t, n = proj.shape
    nc = seq // RET_CHUNK
    w = RET_HEADS * HEAD_DIM
    base = 3 * MOBA_HEADS * HEAD_DIM // w
    tab_spec = pl.BlockSpec((RET_HEADS, RET_CHUNK, HEAD_DIM), lambda c: (0, 0, 0))
    proj3 = proj.reshape(batch, seq, n)

    def col_spec(block):
        return pl.BlockSpec((batch, RET_CHUNK, w), lambda c: (0, c, block))

    out = pl.pallas_call(
        _retention_kernel,
        out_shape=jax.ShapeDtypeStruct((batch, seq, w), F32),
        grid=(nc,),
        in_specs=[col_spec(base), col_spec(base + 1), col_spec(base + 2), col_spec(base + 3),
                  tab_spec, tab_spec, tab_spec, tab_spec],
        out_specs=col_spec(0),
        scratch_shapes=[pltpu.VMEM((batch, RET_HEADS, HEAD_DIM, HEAD_DIM), F32)],
        compiler_params=_params(1),
        name="retention",
    )(proj3, proj3, proj3, proj3, *tables)
    return out.reshape(t, w)


def _retention_tables():
    c = RET_CHUNK
    log_g = jnp.log(1.0 - 2.0 ** (-5.0 - jnp.arange(RET_HEADS, dtype=F32)))
    idx = jnp.arange(c, dtype=F32)
    diff = idx[:, None] - idx[None, :]
    dmask = jnp.where(diff >= 0, jnp.exp(log_g[:, None, None] * jnp.maximum(diff, 0.0)), 0.0)
    xi = jnp.exp(log_g[:, None] * (idx + 1.0))[..., None]
    zeta = jnp.exp(log_g[:, None] * (c - 1.0 - idx))[..., None]
    gc = jnp.exp(log_g * c)[:, None, None]
    full = (RET_HEADS, c, HEAD_DIM)
    return (dmask, jnp.broadcast_to(xi, full), jnp.broadcast_to(zeta, full), jnp.broadcast_to(gc, full))


def _even_out_kernel(h_ref, lo_ref, hi_ref, r_ref, w_ref, o_ref, *, nb):
    in_lo = (pl.program_id(0) % nb) < nb // 2
    mo = jnp.where(in_lo, lo_ref[...], hi_ref[...])
    y = jnp.concatenate([_mxu(mo), _mxu(r_ref[...])], axis=1)
    o_ref[...] = h_ref[...] + jnp.dot(y, w_ref[...], preferred_element_type=F32)


def even_out(h, mo_lo, mo_hi, ro, w, layer, seq):
    t, d = h.shape
    tm = MOBA_BLOCK
    nb = seq // tm
    nh = nb // 2

    def lo_map(g):
        return ((g // nb) * nh + jnp.minimum(g % nb, nh - 1), 0)

    def hi_map(g):
        return ((g // nb) * nh + (nb - 1 - jnp.maximum(g % nb, nh)), 0)

    return pl.pallas_call(
        functools.partial(_even_out_kernel, nb=nb),
        out_shape=jax.ShapeDtypeStruct((t, d), F32),
        grid=(t // tm,),
        in_specs=[
            pl.BlockSpec((tm, d), lambda g: (g, 0)),
            pl.BlockSpec((tm, mo_lo.shape[1]), lo_map),
            pl.BlockSpec((tm, mo_hi.shape[1]), hi_map),
            pl.BlockSpec((tm, ro.shape[1]), lambda g: (g, 0)),
            _layer_spec(w, layer),
        ],
        out_specs=pl.BlockSpec((tm, d), lambda g: (g, 0)),
        compiler_params=_params(1),
        name="even_out",
    )(h, mo_lo, mo_hi, ro, w)


def _conv_out_kernel(h_ref, bg_ref, cg_ref, hx_ref, cgp_ref, hxp_ref, cw_ref, w_ref, o_ref, *, per_seq):
    i = pl.program_id(0)
    u = cg_ref[...] * hx_ref[...]
    up = jnp.where(i % per_seq == 0, 0.0, cgp_ref[...] * hxp_ref[...])
    row = lax.broadcasted_iota(jnp.int32, u.shape, 0)
    p1 = up[V7X_SUBLANES - 1:V7X_SUBLANES, :]
    p2 = up[V7X_SUBLANES - 2:V7X_SUBLANES - 1, :]
    u1 = jnp.where(row == 0, p1, pltpu.roll(u, 1, axis=0))
    u2 = jnp.where(row == 0, p2, jnp.where(row == 1, p1, pltpu.roll(u, 2, axis=0)))
    y = cw_ref[0:1, :] * u2 + cw_ref[1:2, :] * u1 + cw_ref[2:3, :] * u
    z = _mxu(bg_ref[...] * y)
    o_ref[...] = h_ref[...] + jnp.dot(z, w_ref[...], preferred_element_type=F32)


def conv_out(h, proj, conv_w, w, layer, seq):
    t, d = h.shape
    tm = CONV_TM
    per_seq = seq // tm
    halo = tm // V7X_SUBLANES
    return pl.pallas_call(
        functools.partial(_conv_out_kernel, per_seq=per_seq),
        out_shape=jax.ShapeDtypeStruct((t, d), F32),
        grid=(t // tm,),
        in_specs=[
            pl.BlockSpec((tm, d), lambda i: (i, 0)),
            pl.BlockSpec((tm, d), lambda i: (i, 0)),
            pl.BlockSpec((tm, d), lambda i: (i, 1)),
            pl.BlockSpec((tm, d), lambda i: (i, 2)),
            pl.BlockSpec((V7X_SUBLANES, d), lambda i: (jnp.maximum(i * halo - 1, 0), 1)),
            pl.BlockSpec((V7X_SUBLANES, d), lambda i: (jnp.maximum(i * halo - 1, 0), 2)),
            pl.BlockSpec(conv_w.shape, lambda i: (0, 0)),
            _layer_spec(w, layer),
        ],
        out_specs=pl.BlockSpec((tm, d), lambda i: (i, 0)),
        compiler_params=_params(1),
        name="conv_out",
    )(h, proj, proj, proj, proj, proj, conv_w, w)


def _peer_scores_kernel(h_ref, g_ref, wq_ref, keys_ref, xn_ref, sc_ref):
    xn = _mxu(_rms(h_ref[...], g_ref[...]))
    xn_ref[...] = xn
    n_hp = keys_ref.shape[0]
    per = PROJ_COLS // PEER_HALF
    for c in range(n_hp // per):
        q = jnp.dot(xn, wq_ref[:, c * PROJ_COLS:(c + 1) * PROJ_COLS], preferred_element_type=F32)
        for u in range(per):
            hp = c * per + u
            qs = _mxu(q[:, u * PEER_HALF:(u + 1) * PEER_HALF])
            sc = lax.dot_general(keys_ref[hp], qs, NT_DIMS, preferred_element_type=F32)
            for grp in range(sc_ref.shape[1]):
                sc_ref[hp, grp] = sc[:, grp * V7X_LANES:(grp + 1) * V7X_LANES]


def peer_scores(h, g, wq, layer, keys):
    t, d = h.shape
    tm = PEER_Q_TM
    n_hp = keys.shape[0]
    grp = tm // V7X_LANES
    return pl.pallas_call(
        _peer_scores_kernel,
        out_shape=(jax.ShapeDtypeStruct((t, d), MXU_DTYPE),
                   jax.ShapeDtypeStruct((n_hp, t // V7X_LANES, PEER_NKEYS, V7X_LANES), F32)),
        grid=(t // tm,),
        in_specs=[
            pl.BlockSpec((tm, d), lambda i: (i, 0)),
            pl.BlockSpec((1, d), lambda i: (0, 0)),
            _layer_spec(wq, layer),
            pl.BlockSpec(keys.shape, lambda i: (0, 0, 0)),
        ],
        out_specs=(pl.BlockSpec((tm, d), lambda i: (i, 0)),
                   pl.BlockSpec((n_hp, grp, PEER_NKEYS, V7X_LANES), lambda i: (0, i, 0, 0))),
        compiler_params=_params(1),
        name="peer_scores",
    )(h, g.reshape(1, d), wq, keys)


def _extract_max(x, pos):
    m = jnp.max(x, axis=-2, keepdims=True)
    sel = jnp.min(jnp.where(x == m, pos, float(2**30)), axis=-2, keepdims=True)
    return m, sel, jnp.where(pos == sel, NEG_INF, x)


def _topk_rows(x, pos, k):
    vals, ids = [], []
    for _ in range(k):
        m, sel, x = _extract_max(x, pos)
        vals.append(m)
        ids.append(sel)
    return jnp.concatenate(vals, axis=-2), jnp.concatenate(ids, axis=-2)


def _lookup_rows(table, idx):
    out = jnp.zeros(idx.shape, table.dtype)
    for a in range(table.shape[0]):
        out = jnp.where(idx == a, table[a:a + 1, :], out)
    return out


def _key_positions():
    return lax.broadcasted_iota(jnp.int32, (PEER_NKEYS, V7X_LANES), 0).astype(F32)


def _pair_candidates(v1, v2):
    k = PEER_TOPK
    n = V7X_LANES
    r16 = lax.broadcasted_iota(jnp.int32, (k, n), 0)
    r8 = lax.broadcasted_iota(jnp.int32, (V7X_SUBLANES, n), 0)
    cand, pos = [v1[0:1, :] + v2], [r16]
    for a in range(1, V7X_SUBLANES):
        piece = v1[a:a + 1, :] + v2[0:V7X_SUBLANES, :]
        cand.append(jnp.where(r8 < k // (a + 1), piece, NEG_INF))
        pos.append(a * k + r8)
    cand.append(v1[V7X_SUBLANES:, :] + v2[0:1, :])
    pos.append((V7X_SUBLANES + r8) * k)
    return jnp.concatenate(cand, axis=0), jnp.concatenate(pos, axis=0).astype(F32)


def _emit_lists(top_s, top_pos, n1, n2, h, grp, l1_ref, l2_ref, lg_ref):
    k = PEER_TOPK
    top_pos = top_pos.astype(jnp.int32)
    e1 = _lookup_rows(n1, lax.shift_right_logical(top_pos, k.bit_length() - 1))
    e2 = _lookup_rows(n2, top_pos & (k - 1))
    ex = jnp.exp(top_s - top_s[0:1, :])
    gate = ex / jnp.sum(ex, axis=0, keepdims=True)
    rows = pl.ds(pl.multiple_of(h * k, k), k)
    l1_ref[grp, rows, :] = e1.astype(jnp.int32)
    l2_ref[grp, rows, :] = e2.astype(jnp.int32)
    lg_ref[grp, rows, :] = gate


def _transpose_lists(src_refs, dst_refs):
    for src, dst in zip(src_refs, dst_refs):
        for grp in range(src.shape[0]):
            dst[grp * V7X_LANES:(grp + 1) * V7X_LANES, :] = src[grp].T


def _peer_topk_kernel(sc_ref, i1_ref, i2_ref, g_ref, l1_ref, l2_ref, lg_ref):
    n_heads = sc_ref.shape[0] // 2
    n_units = n_heads * sc_ref.shape[1]

    def unit_pair(n, carry):
        units = [(u % n_heads, u // n_heads) for u in (n, n + n_units // 2)]
        x = jnp.stack([sc_ref[2 * h + p, grp] for h, grp in units for p in range(2)])
        hv, hn = _topk_rows(x, _key_positions(), PEER_TOPK)
        cands = [_pair_candidates(hv[2 * k], hv[2 * k + 1]) for k in range(2)]
        top_s, top_pos = _topk_rows(jnp.stack([cd[0] for cd in cands]), jnp.stack([cd[1] for cd in cands]), PEER_TOPK)
        for k, (h, grp) in enumerate(units):
            _emit_lists(top_s[k], top_pos[k], hn[2 * k], hn[2 * k + 1], h, grp, l1_ref, l2_ref, lg_ref)
        return carry

    lax.fori_loop(0, n_units // 2, unit_pair, 0)
    _transpose_lists((l1_ref, l2_ref, lg_ref), (i1_ref, i2_ref, g_ref))


def _topk_scratch(n_groups, nj, slots=()):
    lists = (n_groups, nj, V7X_LANES)
    halves = (*slots, 2, PEER_TOPK, V7X_LANES)
    return [pltpu.VMEM(lists, jnp.int32), pltpu.VMEM(lists, jnp.int32), pltpu.VMEM(lists, F32),
            pltpu.VMEM(halves, F32), pltpu.VMEM(halves, F32)]


def peer_topk(sc, n_tokens):
    n_hp, _, nk, _ = sc.shape
    grp = n_tokens // V7X_LANES
    nj = (n_hp // 2) * PEER_TOPK
    out_spec = pl.BlockSpec((n_tokens, nj), lambda i: (0, 0))
    return pl.pallas_call(
        _peer_topk_kernel,
        out_shape=(jax.ShapeDtypeStruct((n_tokens, nj), jnp.int32), jax.ShapeDtypeStruct((n_tokens, nj), jnp.int32),
                   jax.ShapeDtypeStruct((n_tokens, nj), F32)),
        grid=(1,),
        in_specs=[pl.BlockSpec((n_hp, grp, nk, V7X_LANES), lambda i: (0, 0, 0, 0))],
        out_specs=(out_spec, out_spec, out_spec),
        scratch_shapes=_topk_scratch(grp, nj)[:3],
        compiler_params=_params(1),
        name="peer_topk",
    )(sc)


def _peer_mix_kernel(h_ref, xn_ref, f1_ref, f2_ref, fg_ref, sc_ref, ua_ref, ub_ref, va_ref, vb_ref, o_ref, w_ref,
                     ha_ref, hb_ref, acc_ref, i1_ref, i2_ref, g_ref, l1_ref, l2_ref, lg_ref, hv_ref, hn_ref, *,
                     n_chunks, units_per_step):
    tile = pl.program_id(0)
    c = pl.program_id(1)
    last = pl.num_programs(1) - 1
    tm = xn_ref.shape[0]
    nk = PEER_NKEYS
    half = nk // 2
    next_lists = (l1_ref, l2_ref, lg_ref)
    lists = (i1_ref, i2_ref, g_ref)

    @pl.when((c == 0) & (tile == 0))
    def _():
        for src, dst in zip((f1_ref, f2_ref, fg_ref), lists):
            dst[...] = src[...]
        hv_ref[...] = jnp.zeros_like(hv_ref)
        hn_ref[...] = jnp.zeros_like(hn_ref)

    @pl.when((c == 0) & (tile > 0))
    def _():
        _transpose_lists(next_lists, lists)

    @pl.when(c == 0)
    def _():
        acc_ref[...] = jnp.zeros_like(acc_ref)
        hb_ref[...] = jnp.zeros_like(hb_ref)
        r = lax.broadcasted_iota(jnp.int32, (nk, nk), 0).astype(F32).astype(jnp.bfloat16)
        one = jnp.ones((nk, nk), jnp.bfloat16)
        zero = jnp.zeros((nk, nk), jnp.bfloat16)

        def one_hots(t):
            i1 = i1_ref[pl.ds(t, 1), :].astype(F32).astype(jnp.bfloat16)
            i2 = i2_ref[pl.ds(t, 1), :].astype(F32).astype(jnp.bfloat16)
            g = (0.5 * g_ref[pl.ds(t, 1), :]).astype(jnp.bfloat16)
            return jnp.where(r == i1, g, zero), jnp.where(r == i2, one, zero)

        def tok_pair(n, carry):
            (a0, b0), (a1, b1) = one_hots(2 * n), one_hots(2 * n + 1)
            w = lax.dot_general(jnp.concatenate([a0, a1], axis=0), jnp.concatenate([b0, b1], axis=0), NT_DIMS,
                                preferred_element_type=F32)
            for k in range(2):
                wk = w[k * nk:(k + 1) * nk, k * nk:(k + 1) * nk].astype(jnp.bfloat16)
                dst = pl.ds(pl.multiple_of((2 * n + k) * W_PITCH, V7X_SUBLANES), half)
                w_ref[dst, :] = pltpu.bitcast(wk, jnp.uint32)
            return carry

        lax.fori_loop(0, tm // 2, tok_pair, 0, unroll=SCATTER_UNROLL // 2)

    def hidden(u_ref):
        return jnp.dot(xn_ref[...], u_ref[...], preferred_element_type=F32)

    def weighted(hid_ref, chunk, valid):
        pairs = PEER_ECHUNK // (2 * nk)
        first = jnp.clip(chunk, 0, n_chunks - 1) * pairs
        hid = hid_ref[...]
        act = hid * (1.0 + lax.erf(hid * (2.0 ** -0.5)))
        cols = []
        for q in range(pairs):
            packed = w_ref[pl.ds(first + q, tm, stride=W_PITCH), :]
            if valid is not None:
                packed = jnp.where(valid, packed, jnp.uint32(0))
            w_even = lax.bitcast_convert_type(packed << 16, F32)
            w_odd = lax.bitcast_convert_type(packed & jnp.uint32(0xFFFF0000), F32)
            cols += [w_even * act[:, 2 * q * nk:(2 * q + 1) * nk], w_odd * act[:, (2 * q + 1) * nk:(2 * q + 2) * nk]]
        return _mxu(jnp.concatenate(cols, axis=1))

    def accumulate(p, v_ref):
        acc_ref[...] += jnp.dot(p, v_ref[...], preferred_element_type=F32)

    n_heads = sc_ref.shape[0] // 2
    n_units = n_heads * sc_ref.shape[1]
    xs, cands, cand_poss, prev = [], [], [], []
    for s in range(units_per_step):
        n = jnp.minimum(c * units_per_step + s, n_units - 1)
        h, grp = n % n_heads, n // n_heads
        xs += [sc_ref[2 * h, grp], sc_ref[2 * h + 1, grp]]
        cand, cand_pos = _pair_candidates(hv_ref[s, 0], hv_ref[s, 1])
        cands.append(cand)
        cand_poss.append(cand_pos)
        n = jnp.clip((c - 1) * units_per_step + s, 0, n_units - 1)
        prev.append((n % n_heads, n // n_heads, hn_ref[s, 0], hn_ref[s, 1]))
    state = [[jnp.stack(xs), _key_positions(), [], []], [jnp.stack(cands), jnp.stack(cand_poss), [], []]]

    def select(n_steps):
        for _ in range(n_steps):
            for st in state:
                m, sel, st[0] = _extract_max(st[0], st[1])
                st[2].append(m)
                st[3].append(sel)

    part = PEER_TOPK // 8
    select(part)
    ha_ref[...] = hidden(ua_ref)
    select(part)
    p_prev = weighted(hb_ref, 2 * c - 1, None)
    select(part)
    accumulate(p_prev, va_ref)
    select(part)
    hb_ref[...] = hidden(ub_ref)
    select(part)
    p_cur = weighted(ha_ref, 2 * c, c < last)
    select(part)
    accumulate(p_cur, vb_ref)
    select(PEER_TOPK - 6 * part)

    (half_v, half_n), (pair_v, pair_pos) = [
        (jnp.concatenate(st[2], axis=-2), jnp.concatenate(st[3], axis=-2)) for st in state]
    for s, (h, grp, n1, n2) in enumerate(prev):
        hv_ref[s] = half_v[2 * s:2 * s + 2]
        hn_ref[s] = half_n[2 * s:2 * s + 2]
        _emit_lists(pair_v[s], pair_pos[s], n1, n2, h, grp, *next_lists)

    @pl.when(c == last)
    def _():
        o_ref[...] = h_ref[...] + acc_ref[...]


def peer_mix(h, xn, sc, first_lists, u, v, layer):
    t, d = h.shape
    tm = first_lists[0].shape[0]
    nj = first_lists[0].shape[1]
    n_tiles = t // tm
    grp = tm // V7X_LANES
    n_hp = sc.shape[0]
    n_chunks = v.shape[1] // PEER_ECHUNK
    steps = n_chunks // 2 + 1
    n_units = (n_hp // 2) * grp
    units_per_step = -(-n_units // (steps - 1))
    tok_spec = pl.BlockSpec((tm, d), lambda i, c: (i, 0))
    res_spec = pl.BlockSpec((tm, d), lambda i, c: (jnp.where(c >= 1, i, jnp.maximum(i - 1, 0)), 0))
    first_spec = pl.BlockSpec((tm, nj), lambda i, c: (0, 0))
    sc_spec = pl.BlockSpec((n_hp, grp, PEER_NKEYS, V7X_LANES), lambda i, c: (0, jnp.minimum(i + 1, n_tiles - 1), 0, 0))

    def chunk(c, offset):
        return jnp.clip(2 * c + offset, 0, n_chunks - 1)

    def u_spec(offset):
        return pl.BlockSpec((None, None, d, PEER_ECHUNK), lambda i, c: (layer, chunk(c, offset), 0, 0))

    def v_spec(offset):
        return pl.BlockSpec((None, PEER_ECHUNK, d), lambda i, c: (layer, chunk(c, offset), 0))

    return pl.pallas_call(
        functools.partial(_peer_mix_kernel, n_chunks=n_chunks, units_per_step=units_per_step),
        out_shape=jax.ShapeDtypeStruct((t, d), F32),
        grid=(n_tiles, steps),
        in_specs=[res_spec, tok_spec, first_spec, first_spec, first_spec, sc_spec, u_spec(0), u_spec(1),
                  v_spec(-1), v_spec(0)],
        out_specs=tok_spec,
        scratch_shapes=[
            pltpu.VMEM((tm * W_PITCH, PEER_NKEYS), jnp.uint32),
            pltpu.VMEM((tm, PEER_ECHUNK), F32),
            pltpu.VMEM((tm, PEER_ECHUNK), F32),
            pltpu.VMEM((tm, d), F32),
            pltpu.VMEM((tm, nj), jnp.int32),
            pltpu.VMEM((tm, nj), jnp.int32),
            pltpu.VMEM((tm, nj), F32),
            *_topk_scratch(grp, nj, slots=(units_per_step,)),
        ],
        compiler_params=_params(2),
        name="peer_mix",
    )(h, xn, *first_lists, sc, u, u, v, v)


def _final_norm_kernel(h_ref, g_ref, o_ref):
    o_ref[...] = _rms(h_ref[...], g_ref[...])


def final_rmsnorm(h, g):
    t, d = h.shape
    tm = OUT_TM
    return pl.pallas_call(
        _final_norm_kernel,
        out_shape=jax.ShapeDtypeStruct((t, d), F32),
        grid=(t // tm,),
        in_specs=[pl.BlockSpec((tm, d), lambda i: (i, 0)), pl.BlockSpec((1, d), lambda i: (0, 0))],
        out_specs=pl.BlockSpec((tm, d), lambda i: (i, 0)),
        compiler_params=_params(1),
        name="final_norm",
    )(h, g.reshape(1, d))


def _rope_tables(seq):
    half = HEAD_DIM // 2
    inv = ROPE_THETA ** (-jnp.arange(half, dtype=F32) / half)
    ang = jnp.arange(seq, dtype=F32)[:, None] * inv[None, :]
    cos, sin = jnp.cos(ang), jnp.sin(ang)
    return jnp.concatenate([cos, cos], axis=1), jnp.concatenate([-sin, sin], axis=1)


def kernel(x, norm_mix, norm_ffn, even_w_in, even_w_out, odd_w_in, odd_conv, odd_w_out, peer_w_q, peer_sub_keys,
           peer_u, peer_v, final_norm):
    batch, seq, d = x.shape
    depth = norm_mix.shape[0]
    cos_t, sin_t = _rope_tables(seq)
    ret_tables = _retention_tables()
    mh, rh = MOBA_HEADS, RET_HEADS
    rope_blocks = tuple(range(0, 2 * mh)) + tuple(range(3 * mh, 3 * mh + 2 * rh))
    ret_k_blocks = tuple(range(3 * mh + rh, 3 * mh + 2 * rh))

    even_in, even_out_w = cast_weights(even_w_in), cast_weights(even_w_out)
    odd_in, odd_out_w = cast_weights(odd_w_in), cast_weights(odd_w_out)
    w_q, u_t, v = cast_weights(peer_w_q), cast_weights(peer_u, slab=PEER_ECHUNK), cast_weights(peer_v)
    keys = _mxu(peer_sub_keys.reshape(depth, 2 * PEER_HEADS, PEER_NKEYS, PEER_HALF))

    h = x.reshape(batch * seq, d)
    for layer in range(depth):
        i = layer // 2
        if layer % 2 == 0:
            proj = norm_proj(h, norm_mix[layer], even_in, i, cos_t, sin_t, seq, rope_blocks, ret_k_blocks,
                             HEAD_DIM ** -0.5)
            mo_lo, mo_hi = moba(proj, batch, seq)
            ro = retention(proj, ret_tables, batch, seq)
            h = even_out(h, mo_lo, mo_hi, ro, even_out_w, i, seq)
        else:
            proj = norm_proj(h, norm_mix[layer], odd_in, i, cos_t, sin_t, seq)
            h = conv_out(h, proj, odd_conv[i], odd_out_w, i, seq)
        xn, sc = peer_scores(h, norm_ffn[layer], w_q, layer, keys[layer])
        first_lists = peer_topk(sc, min(PEER_TM, h.shape[0]))
        h = peer_mix(h, xn, sc, first_lists, u_t, v, layer)
    return final_rmsnorm(h, final_norm).reshape(batch, seq, d)
```

```python
import functools

import jax
import jax.numpy as jnp
import numpy as np
from jax import lax
from jax.experimental import pallas as pl
from jax.experimental.pallas import tpu as pltpu

HEAD_DIM = 128
MOBA_HEADS = 4
RET_HEADS = 4
MOBA_BLOCK = 256
MOBA_TOPK = 3
RET_CHUNK = 128
PEER_HEADS = 8
PEER_NKEYS = 128
PEER_TOPK = 16
PEER_HALF = 128
ROPE_THETA = 10000.0
EPS = 1e-6

V7X_LANES = 128
V7X_SUBLANES = 8
V7X_VMEM_BYTES = 64 * 2**20
VMEM_LIMIT = V7X_VMEM_BYTES - 8 * 2**20

MXU_DTYPE = jnp.bfloat16
F32 = jnp.float32
NEG_INF = float("-inf")
NT_DIMS = (((1,), (1,)), ((), ()))
TN_DIMS = (((0,), (0,)), ((), ()))

CAST_ROWS = 512
PROJ_TM = 512
PROJ_COLS = 512
CONV_TM = 512
PEER_Q_TM = 512
PEER_TM = 512
PEER_ECHUNK = 2 * PEER_NKEYS
SCATTER_UNROLL = 128
W_PITCH = 72


def _params(n_axes):
    return pltpu.CompilerParams(dimension_semantics=("arbitrary",) * n_axes, vmem_limit_bytes=VMEM_LIMIT)


def _mxu(x):
    return x.astype(MXU_DTYPE)


def _rms(x, g):
    return x * lax.rsqrt(jnp.mean(x * x, axis=-1, keepdims=True) + EPS) * g


def _layer_spec(stack, layer):
    return pl.BlockSpec((None, *stack.shape[1:]), lambda *_: (layer, 0, 0))


def _cast_kernel(w_ref, o_ref, *, slab):
    if slab is None:
        o_ref[...] = _mxu(w_ref[...])
    else:
        for q in range(o_ref.shape[0]):
            o_ref[q] = _mxu(w_ref[q * slab:(q + 1) * slab, :].T)


def cast_weights(w, slab=None):
    n_layers, rows, cols = w.shape
    tr = min(rows, CAST_ROWS)
    if slab is None:
        out_shape, out_block, out_map = w.shape, (None, tr, cols), lambda l, r: (l, r, 0)
    else:
        out_shape = (n_layers, rows // slab, cols, slab)
        out_block, out_map = (None, tr // slab, cols, slab), lambda l, r: (l, r, 0, 0)
    return pl.pallas_call(
        functools.partial(_cast_kernel, slab=slab),
        out_shape=jax.ShapeDtypeStruct(out_shape, MXU_DTYPE),
        grid=(n_layers, rows // tr),
        in_specs=[pl.BlockSpec((None, tr, cols), lambda l, r: (l, r, 0))],
        out_specs=pl.BlockSpec(out_block, out_map),
        compiler_params=_params(2),
        name="cast_weights",
    )(w)


def _norm_proj_kernel(h_ref, g_ref, w_ref, cos_ref, sin_ref, o_ref, *, rope_blocks, scale_blocks, scale):
    xn = _mxu(_rms(h_ref[...], g_ref[...]))
    n = o_ref.shape[1]
    for c0 in range(0, n, PROJ_COLS):
        y = jnp.dot(xn, w_ref[:, c0:c0 + PROJ_COLS], preferred_element_type=F32)
        for u in range(PROJ_COLS // HEAD_DIM):
            blk = (c0 // HEAD_DIM) + u
            t = y[:, u * HEAD_DIM:(u + 1) * HEAD_DIM]
            if blk in rope_blocks:
                t = t * cos_ref[...] + pltpu.roll(t, HEAD_DIM // 2, axis=1) * sin_ref[...]
            if blk in scale_blocks:
                t = t * scale
            o_ref[:, blk * HEAD_DIM:(blk + 1) * HEAD_DIM] = t


def norm_proj(h, g, w, layer, cos_t, sin_t, seq, rope_blocks=(), scale_blocks=(), scale=1.0):
    t, d = h.shape
    n = w.shape[2]
    tm = PROJ_TM
    per_seq = seq // tm
    return pl.pallas_call(
        functools.partial(_norm_proj_kernel, rope_blocks=frozenset(rope_blocks), scale_blocks=frozenset(scale_blocks),
                          scale=scale),
        out_shape=jax.ShapeDtypeStruct((t, n), F32),
        grid=(t // tm,),
        in_specs=[
            pl.BlockSpec((tm, d), lambda i: (i, 0)),
            pl.BlockSpec((1, d), lambda i: (0, 0)),
            _layer_spec(w, layer),
            pl.BlockSpec((tm, HEAD_DIM), lambda i: (i % per_seq, 0)),
            pl.BlockSpec((tm, HEAD_DIM), lambda i: (i % per_seq, 0)),
        ],
        out_specs=pl.BlockSpec((tm, n), lambda i: (i, 0)),
        compiler_params=_params(1),
        name="norm_proj",
    )(h, g.reshape(1, d), w, cos_t, sin_t)


def _moba_kernel(qa_ref, qb_ref, k_ref, v_ref, oa_ref, ob_ref, kb_ref, vt_ref, km_ref, q2_ref, sel_ref, s_ref,
                 acc_ref, l_ref, *, nb, nbp):
    i = pl.program_id(2)
    blk = MOBA_BLOCK
    own = (i, nb - 1 - i)

    @pl.when(i == 0)
    def _():
        km_ref[...] = jnp.zeros_like(km_ref)
        for j in range(nb):
            rows = slice(j * blk, (j + 1) * blk)
            kb_ref[j] = _mxu(k_ref[rows, :])
            vt_ref[j] = _mxu(v_ref[rows, :].T)
            km_ref[j:j + 1, :] = jnp.sum(k_ref[rows, :], axis=0, keepdims=True) / float(blk)

    sub = lax.broadcasted_iota(jnp.int32, (nbp, blk), 0)
    kpos = lax.broadcasted_iota(jnp.int32, (blk, blk), 0)
    qpos = lax.broadcasted_iota(jnp.int32, (blk, blk), 1)

    def scores_t(j, w):
        return lax.dot_general(kb_ref[j], q2_ref[w], NT_DIMS, preferred_element_type=F32)

    m = []
    for w, q_ref in enumerate((qa_ref, qb_ref)):
        q = q_ref[...]
        q2_ref[w] = _mxu(q)
        gate = lax.dot_general(km_ref[...], q, NT_DIMS, precision=lax.Precision.HIGHEST, preferred_element_type=F32)
        rank = jnp.zeros((nbp, blk), jnp.int32)
        for jp in range(nb):
            other = gate[jp:jp + 1, :]
            beats = jnp.where(other > gate, 1, jnp.where(other == gate, jnp.where(jp < sub, 1, 0), 0))
            rank = rank + jnp.where(jp < own[w], beats, 0)
        sel_ref[w] = jnp.where(sub < own[w], jnp.where(rank < MOBA_TOPK, 1.0, 0.0), 0.0)
        s = jnp.where(kpos <= qpos, scores_t(own[w], w), NEG_INF)
        s_ref[nb - 1 + w] = s
        m.append(jnp.max(s, axis=0, keepdims=True))

    def past_pair(t):
        is_b = t >= i
        return is_b, jnp.where(is_b, 1, 0), jnp.where(is_b, t - i, t)

    for t in range(nb - 1):
        is_b, w, j = past_pair(t)
        s = jnp.where(sel_ref[w, pl.ds(j, 1), :] > 0.0, scores_t(j, w), NEG_INF)
        s_ref[t] = s
        mt = jnp.max(s, axis=0, keepdims=True)
        m = [jnp.where(is_b, m[0], jnp.maximum(m[0], mt)), jnp.where(is_b, jnp.maximum(m[1], mt), m[1])]

    c_exp = (HEAD_DIM ** -0.5) * float(np.log2(np.e))

    def weigh(slot, j, mq):
        p = jnp.exp2((s_ref[slot] - mq) * c_exp)
        return jnp.sum(p, axis=0, keepdims=True), jnp.dot(vt_ref[j], _mxu(p), preferred_element_type=F32)

    for w in range(2):
        l_ref[w], acc_ref[w] = weigh(nb - 1 + w, own[w], m[w])
    for t in range(nb - 1):
        is_b, w, j = past_pair(t)
        lt, at = weigh(t, j, jnp.where(is_b, m[1], m[0]))
        l_ref[w] += lt
        acc_ref[w] += at
    oa_ref[...] = (acc_ref[0] / l_ref[0]).T
    ob_ref[...] = (acc_ref[1] / l_ref[1]).T


def moba(proj, batch, seq):
    t = proj.shape[0]
    nb = seq // MOBA_BLOCK
    nbp = max(nb, V7X_SUBLANES)
    hd = HEAD_DIM
    assert nb % 2 == 0
    nh = nb // 2
    out = jax.ShapeDtypeStruct((t // 2, MOBA_HEADS * hd), F32)
    out_spec = pl.BlockSpec((MOBA_BLOCK, hd), lambda b, h, i: (b * nh + i, h))
    return pl.pallas_call(
        functools.partial(_moba_kernel, nb=nb, nbp=nbp),
        out_shape=(out, out),
        grid=(batch, MOBA_HEADS, nh),
        in_specs=[
            pl.BlockSpec((MOBA_BLOCK, hd), lambda b, h, i: (b * nb + i, h)),
            pl.BlockSpec((MOBA_BLOCK, hd), lambda b, h, i: (b * nb + nb - 1 - i, h)),
            pl.BlockSpec((seq, hd), lambda b, h, i: (b, MOBA_HEADS + h)),
            pl.BlockSpec((seq, hd), lambda b, h, i: (b, 2 * MOBA_HEADS + h)),
        ],
        out_specs=(out_spec, out_spec),
        scratch_shapes=[
            pltpu.VMEM((nb, MOBA_BLOCK, hd), MXU_DTYPE),
            pltpu.VMEM((nb, hd, MOBA_BLOCK), MXU_DTYPE),
            pltpu.VMEM((nbp, hd), F32),
            pltpu.VMEM((2, MOBA_BLOCK, hd), MXU_DTYPE),
            pltpu.VMEM((2, nbp, MOBA_BLOCK), F32),
            pltpu.VMEM((nb + 1, MOBA_BLOCK, MOBA_BLOCK), F32),
            pltpu.VMEM((2, hd, MOBA_BLOCK), F32),
            pltpu.VMEM((2, 1, MOBA_BLOCK), F32),
        ],
        compiler_params=_params(3),
        name="moba",
    )(proj, proj, proj, proj)


def _retention_kernel(q_ref, k_ref, v_ref, g_ref, dm_ref, xi_ref, ze_ref, gc_ref, o_ref, st_ref):
    @pl.when(pl.program_id(0) == 0)
    def _():
        st_ref[...] = jnp.zeros_like(st_ref)

    for b in range(q_ref.shape[0]):
        for h in range(RET_HEADS):
            sl = slice(h * HEAD_DIM, (h + 1) * HEAD_DIM)
            q = _mxu(q_ref[b, :, sl])
            k = k_ref[b, :, sl]
            v = _mxu(v_ref[b, :, sl])
            st = st_ref[b, h]
            inner = lax.dot_general(q, _mxu(k), NT_DIMS, preferred_element_type=F32) * dm_ref[h]
            o = (jnp.dot(_mxu(inner), v, preferred_element_type=F32)
                 + jnp.dot(q, _mxu(st), preferred_element_type=F32) * xi_ref[h])
            st_ref[b, h] = (st * gc_ref[h]
                            + lax.dot_general(_mxu(k * ze_ref[h]), v, TN_DIMS, preferred_element_type=F32))
            o = o * lax.rsqrt(jnp.mean(o * o, axis=-1, keepdims=True) + EPS)
            gate = g_ref[b, :, sl]
            o_ref[b, :, sl] = o * (gate * jax.nn.sigmoid(gate))


def retention(proj, tables, batch, seq):
    t, n = proj.shape
    nc = seq // RET_CHUNK
    w = RET_HEADS * HEAD_DIM
    base = 3 * MOBA_HEADS * HEAD_DIM // w
    tab_spec = pl.BlockSpec((RET_HEADS, RET_CHUNK, HEAD_DIM), lambda c: (0, 0, 0))
    proj3 = proj.reshape(batch, seq, n)

    def col_spec(block):
        return pl.BlockSpec((batch, RET_CHUNK, w), lambda c: (0, c, block))

    out = pl.pallas_call(
        _retention_kernel,
        out_shape=jax.ShapeDtypeStruct((batch, seq, w), F32),
        grid=(nc,),
        in_specs=[col_spec(base), col_spec(base + 1), col_spec(base + 2), col_spec(base + 3),
                  tab_spec, tab_spec, tab_spec, tab_spec],
        out_specs=col_spec(0),
        scratch_shapes=[pltpu.VMEM((batch, RET_HEADS, HEAD_DIM, HEAD_DIM), F32)],
        compiler_params=_params(1),
        name="retention",
    )(proj3, proj3, proj3, proj3, *tables)
    return out.reshape(t, w)


def _retention_tables():
    c = RET_CHUNK
    log_g = jnp.log(1.0 - 2.0 ** (-5.0 - jnp.arange(RET_HEADS, dtype=F32)))
    idx = jnp.arange(c, dtype=F32)
    diff = idx[:, None] - idx[None, :]
    dmask = jnp.where(diff >= 0, jnp.exp(log_g[:, None, None] * jnp.maximum(diff, 0.0)), 0.0)
    xi = jnp.exp(log_g[:, None] * (idx + 1.0))[..., None]
    zeta = jnp.exp(log_g[:, None] * (c - 1.0 - idx))[..., None]
    gc = jnp.exp(log_g * c)[:, None, None]
    full = (RET_HEADS, c, HEAD_DIM)
    return (dmask, jnp.broadcast_to(xi, full), jnp.broadcast_to(zeta, full), jnp.broadcast_to(gc, full))


def _even_out_kernel(h_ref, lo_ref, hi_ref, r_ref, w_ref, o_ref, *, nb):
    in_lo = (pl.program_id(0) % nb) < nb // 2
    mo = jnp.where(in_lo, lo_ref[...], hi_ref[...])
    y = jnp.concatenate([_mxu(mo), _mxu(r_ref[...])], axis=1)
    o_ref[...] = h_ref[...] + jnp.dot(y, w_ref[...], preferred_element_type=F32)


def even_out(h, mo_lo, mo_hi, ro, w, layer, seq):
    t, d = h.shape
    tm = MOBA_BLOCK
    nb = seq // tm
    nh = nb // 2

    def lo_map(g):
        return ((g // nb) * nh + jnp.minimum(g % nb, nh - 1), 0)

    def hi_map(g):
        return ((g // nb) * nh + (nb - 1 - jnp.maximum(g % nb, nh)), 0)

    return pl.pallas_call(
        functools.partial(_even_out_kernel, nb=nb),
        out_shape=jax.ShapeDtypeStruct((t, d), F32),
        grid=(t // tm,),
        in_specs=[
            pl.BlockSpec((tm, d), lambda g: (g, 0)),
            pl.BlockSpec((tm, mo_lo.shape[1]), lo_map),
            pl.BlockSpec((tm, mo_hi.shape[1]), hi_map),
            pl.BlockSpec((tm, ro.shape[1]), lambda g: (g, 0)),
            _layer_spec(w, layer),
        ],
        out_specs=pl.BlockSpec((tm, d), lambda g: (g, 0)),
        compiler_params=_params(1),
        name="even_out",
    )(h, mo_lo, mo_hi, ro, w)


def _conv_out_kernel(h_ref, bg_ref, cg_ref, hx_ref, cgp_ref, hxp_ref, cw_ref, w_ref, o_ref, *, per_seq):
    i = pl.program_id(0)
    u = cg_ref[...] * hx_ref[...]
    up = jnp.where(i % per_seq == 0, 0.0, cgp_ref[...] * hxp_ref[...])
    row = lax.broadcasted_iota(jnp.int32, u.shape, 0)
    p1 = up[V7X_SUBLANES - 1:V7X_SUBLANES, :]
    p2 = up[V7X_SUBLANES - 2:V7X_SUBLANES - 1, :]
    u1 = jnp.where(row == 0, p1, pltpu.roll(u, 1, axis=0))
    u2 = jnp.where(row == 0, p2, jnp.where(row == 1, p1, pltpu.roll(u, 2, axis=0)))
    y = cw_ref[0:1, :] * u2 + cw_ref[1:2, :] * u1 + cw_ref[2:3, :] * u
    z = _mxu(bg_ref[...] * y)
    o_ref[...] = h_ref[...] + jnp.dot(z, w_ref[...], preferred_element_type=F32)


def conv_out(h, proj, conv_w, w, layer, seq):
    t, d = h.shape
    tm = CONV_TM
    per_seq = seq // tm
    halo = tm // V7X_SUBLANES
    return pl.pallas_call(
        functools.partial(_conv_out_kernel, per_seq=per_seq),
        out_shape=jax.ShapeDtypeStruct((t, d), F32),
        grid=(t // tm,),
        in_specs=[
            pl.BlockSpec((tm, d), lambda i: (i, 0)),
            pl.BlockSpec((tm, d), lambda i: (i, 0)),
            pl.BlockSpec((tm, d), lambda i: (i, 1)),
            pl.BlockSpec((tm, d), lambda i: (i, 2)),
            pl.BlockSpec((V7X_SUBLANES, d), lambda i: (jnp.maximum(i * halo - 1, 0), 1)),
            pl.BlockSpec((V7X_SUBLANES, d), lambda i: (jnp.maximum(i * halo - 1, 0), 2)),
            pl.BlockSpec(conv_w.shape, lambda i: (0, 0)),
            _layer_spec(w, layer),
        ],
        out_specs=pl.BlockSpec((tm, d), lambda i: (i, 0)),
        compiler_params=_params(1),
        name="conv_out",
    )(h, proj, proj, proj, proj, proj, conv_w, w)


def _peer_scores_kernel(h_ref, g_ref, wq_ref, keys_ref, xn_ref, sc_ref):
    xn = _mxu(_rms(h_ref[...], g_ref[...]))
    xn_ref[...] = xn
    n_hp = keys_ref.shape[0]
    per = PROJ_COLS // PEER_HALF
    for c in range(n_hp // per):
        q = jnp.dot(xn, wq_ref[:, c * PROJ_COLS:(c + 1) * PROJ_COLS], preferred_element_type=F32)
        for u in range(per):
            hp = c * per + u
            qs = _mxu(q[:, u * PEER_HALF:(u + 1) * PEER_HALF])
            sc = lax.dot_general(keys_ref[hp], qs, NT_DIMS, preferred_element_type=F32)
            for grp in range(sc_ref.shape[1]):
                sc_ref[hp, grp] = sc[:, grp * V7X_LANES:(grp + 1) * V7X_LANES]


def peer_scores(h, g, wq, layer, keys):
    t, d = h.shape
    tm = PEER_Q_TM
    n_hp = keys.shape[0]
    grp = tm // V7X_LANES
    return pl.pallas_call(
        _peer_scores_kernel,
        out_shape=(jax.ShapeDtypeStruct((t, d), MXU_DTYPE),
                   jax.ShapeDtypeStruct((n_hp, t // V7X_LANES, PEER_NKEYS, V7X_LANES), F32)),
        grid=(t // tm,),
        in_specs=[
            pl.BlockSpec((tm, d), lambda i: (i, 0)),
            pl.BlockSpec((1, d), lambda i: (0, 0)),
            _layer_spec(wq, layer),
            pl.BlockSpec(keys.shape, lambda i: (0, 0, 0)),
        ],
        out_specs=(pl.BlockSpec((tm, d), lambda i: (i, 0)),
                   pl.BlockSpec((n_hp, grp, PEER_NKEYS, V7X_LANES), lambda i: (0, i, 0, 0))),
        compiler_params=_params(1),
        name="peer_scores",
    )(h, g.reshape(1, d), wq, keys)


def _extract_max(x, pos):
    m = jnp.max(x, axis=-2, keepdims=True)
    sel = jnp.min(jnp.where(x == m, pos, float(2**30)), axis=-2, keepdims=True)
    return m, sel, jnp.where(pos == sel, NEG_INF, x)


def _topk_rows(x, pos, k):
    vals, ids = [], []
    for _ in range(k):
        m, sel, x = _extract_max(x, pos)
        vals.append(m)
        ids.append(sel)
    return jnp.concatenate(vals, axis=-2), jnp.concatenate(ids, axis=-2)


def _lookup_rows(table, idx):
    out = jnp.zeros(idx.shape, table.dtype)
    for a in range(table.shape[0]):
        out = jnp.where(idx == a, table[a:a + 1, :], out)
    return out


def _key_positions():
    return lax.broadcasted_iota(jnp.int32, (PEER_NKEYS, V7X_LANES), 0).astype(F32)


def _pair_candidates(v1, v2):
    k = PEER_TOPK
    n = V7X_LANES
    r16 = lax.broadcasted_iota(jnp.int32, (k, n), 0)
    r8 = lax.broadcasted_iota(jnp.int32, (V7X_SUBLANES, n), 0)
    cand, pos = [v1[0:1, :] + v2], [r16]
    for a in range(1, V7X_SUBLANES):
        piece = v1[a:a + 1, :] + v2[0:V7X_SUBLANES, :]
        cand.append(jnp.where(r8 < k // (a + 1), piece, NEG_INF))
        pos.append(a * k + r8)
    cand.append(v1[V7X_SUBLANES:, :] + v2[0:1, :])
    pos.append((V7X_SUBLANES + r8) * k)
    return jnp.concatenate(cand, axis=0), jnp.concatenate(pos, axis=0).astype(F32)


def _emit_lists(top_s, top_pos, n1, n2, h, grp, l1_ref, l2_ref, lg_ref):
    k = PEER_TOPK
    top_pos = top_pos.astype(jnp.int32)
    e1 = _lookup_rows(n1, lax.shift_right_logical(top_pos, k.bit_length() - 1))
    e2 = _lookup_rows(n2, top_pos & (k - 1))
    ex = jnp.exp(top_s - top_s[0:1, :])
    gate = ex / jnp.sum(ex, axis=0, keepdims=True)
    rows = pl.ds(pl.multiple_of(h * k, k), k)
    l1_ref[grp, rows, :] = e1.astype(jnp.int32)
    l2_ref[grp, rows, :] = e2.astype(jnp.int32)
    lg_ref[grp, rows, :] = gate


def _transpose_lists(src_refs, dst_refs):
    for src, dst in zip(src_refs, dst_refs):
        for grp in range(src.shape[0]):
            dst[grp * V7X_LANES:(grp + 1) * V7X_LANES, :] = src[grp].T


def _peer_topk_kernel(sc_ref, i1_ref, i2_ref, g_ref, l1_ref, l2_ref, lg_ref):
    n_heads = sc_ref.shape[0] // 2
    n_units = n_heads * sc_ref.shape[1]

    def unit_pair(n, carry):
        units = [(u % n_heads, u // n_heads) for u in (n, n + n_units // 2)]
        x = jnp.stack([sc_ref[2 * h + p, grp] for h, grp in units for p in range(2)])
        hv, hn = _topk_rows(x, _key_positions(), PEER_TOPK)
        cands = [_pair_candidates(hv[2 * k], hv[2 * k + 1]) for k in range(2)]
        top_s, top_pos = _topk_rows(jnp.stack([cd[0] for cd in cands]), jnp.stack([cd[1] for cd in cands]), PEER_TOPK)
        for k, (h, grp) in enumerate(units):
            _emit_lists(top_s[k], top_pos[k], hn[2 * k], hn[2 * k + 1], h, grp, l1_ref, l2_ref, lg_ref)
        return carry

    lax.fori_loop(0, n_units // 2, unit_pair, 0)
    _transpose_lists((l1_ref, l2_ref, lg_ref), (i1_ref, i2_ref, g_ref))


def _topk_scratch(n_groups, nj, slots=()):
    lists = (n_groups, nj, V7X_LANES)
    halves = (*slots, 2, PEER_TOPK, V7X_LANES)
    return [pltpu.VMEM(lists, jnp.int32), pltpu.VMEM(lists, jnp.int32), pltpu.VMEM(lists, F32),
            pltpu.VMEM(halves, F32), pltpu.VMEM(halves, F32)]


def peer_topk(sc, n_tokens):
    n_hp, _, nk, _ = sc.shape
    grp = n_tokens // V7X_LANES
    nj = (n_hp // 2) * PEER_TOPK
    out_spec = pl.BlockSpec((n_tokens, nj), lambda i: (0, 0))
    return pl.pallas_call(
        _peer_topk_kernel,
        out_shape=(jax.ShapeDtypeStruct((n_tokens, nj), jnp.int32), jax.ShapeDtypeStruct((n_tokens, nj), jnp.int32),
                   jax.ShapeDtypeStruct((n_tokens, nj), F32)),
        grid=(1,),
        in_specs=[pl.BlockSpec((n_hp, grp, nk, V7X_LANES), lambda i: (0, 0, 0, 0))],
        out_specs=(out_spec, out_spec, out_spec),
        scratch_shapes=_topk_scratch(grp, nj)[:3],
        compiler_params=_params(1),
        name="peer_topk",
    )(sc)


def _peer_mix_kernel(h_ref, xn_ref, f1_ref, f2_ref, fg_ref, sc_ref, ua_ref, ub_ref, va_ref, vb_ref, gain_ref, o_ref,
                     w_ref, ha_ref, hb_ref, acc_ref, i1_ref, i2_ref, g_ref, l1_ref, l2_ref, lg_ref, hv_ref, hn_ref,
                     *, n_chunks, units_per_step, normalise):
    tile = pl.program_id(0)
    c = pl.program_id(1)
    last = pl.num_programs(1) - 1
    tm = xn_ref.shape[0]
    nk = PEER_NKEYS
    half = nk // 2
    next_lists = (l1_ref, l2_ref, lg_ref)
    lists = (i1_ref, i2_ref, g_ref)

    @pl.when((c == 0) & (tile == 0))
    def _():
        for src, dst in zip((f1_ref, f2_ref, fg_ref), lists):
            dst[...] = src[...]
        hv_ref[...] = jnp.zeros_like(hv_ref)
        hn_ref[...] = jnp.zeros_like(hn_ref)

    @pl.when((c == 0) & (tile > 0))
    def _():
        _transpose_lists(next_lists, lists)

    @pl.when(c == 0)
    def _():
        acc_ref[...] = jnp.zeros_like(acc_ref)
        hb_ref[...] = jnp.zeros_like(hb_ref)
        r = lax.broadcasted_iota(jnp.int32, (nk, nk), 0).astype(F32).astype(jnp.bfloat16)
        one = jnp.ones((nk, nk), jnp.bfloat16)
        zero = jnp.zeros((nk, nk), jnp.bfloat16)

        def one_hots(t):
            i1 = i1_ref[pl.ds(t, 1), :].astype(F32).astype(jnp.bfloat16)
            i2 = i2_ref[pl.ds(t, 1), :].astype(F32).astype(jnp.bfloat16)
            g = (0.5 * g_ref[pl.ds(t, 1), :]).astype(jnp.bfloat16)
            return jnp.where(r == i1, g, zero), jnp.where(r == i2, one, zero)

        def tok_pair(n, carry):
            (a0, b0), (a1, b1) = one_hots(2 * n), one_hots(2 * n + 1)
            w = lax.dot_general(jnp.concatenate([a0, a1], axis=0), jnp.concatenate([b0, b1], axis=0), NT_DIMS,
                                preferred_element_type=F32)
            for k in range(2):
                wk = w[k * nk:(k + 1) * nk, k * nk:(k + 1) * nk].astype(jnp.bfloat16)
                dst = pl.ds(pl.multiple_of((2 * n + k) * W_PITCH, V7X_SUBLANES), half)
                w_ref[dst, :] = pltpu.bitcast(wk, jnp.uint32)
            return carry

        lax.fori_loop(0, tm // 2, tok_pair, 0, unroll=SCATTER_UNROLL // 2)

    def hidden(u_ref):
        return jnp.dot(xn_ref[...], u_ref[...], preferred_element_type=F32)

    def weighted(hid_ref, chunk, valid):
        pairs = PEER_ECHUNK // (2 * nk)
        first = jnp.clip(chunk, 0, n_chunks - 1) * pairs
        hid = hid_ref[...]
        act = hid * (1.0 + lax.erf(hid * (2.0 ** -0.5)))
        cols = []
        for q in range(pairs):
            packed = w_ref[pl.ds(first + q, tm, stride=W_PITCH), :]
            if valid is not None:
                packed = jnp.where(valid, packed, jnp.uint32(0))
            w_even = lax.bitcast_convert_type(packed << 16, F32)
            w_odd = lax.bitcast_convert_type(packed & jnp.uint32(0xFFFF0000), F32)
            cols += [w_even * act[:, 2 * q * nk:(2 * q + 1) * nk], w_odd * act[:, (2 * q + 1) * nk:(2 * q + 2) * nk]]
        return _mxu(jnp.concatenate(cols, axis=1))

    def accumulate(p, v_ref):
        acc_ref[...] += jnp.dot(p, v_ref[...], preferred_element_type=F32)

    n_heads = sc_ref.shape[0] // 2
    n_units = n_heads * sc_ref.shape[1]
    xs, cands, cand_poss, prev = [], [], [], []
    for s in range(units_per_step):
        n = jnp.minimum(c * units_per_step + s, n_units - 1)
        h, grp = n % n_heads, n // n_heads
        xs += [sc_ref[2 * h, grp], sc_ref[2 * h + 1, grp]]
        cand, cand_pos = _pair_candidates(hv_ref[s, 0], hv_ref[s, 1])
        cands.append(cand)
        cand_poss.append(cand_pos)
        n = jnp.clip((c - 1) * units_per_step + s, 0, n_units - 1)
        prev.append((n % n_heads, n // n_heads, hn_ref[s, 0], hn_ref[s, 1]))
    state = [[jnp.stack(xs), _key_positions(), [], []], [jnp.stack(cands), jnp.stack(cand_poss), [], []]]

    def select(n_steps):
        for _ in range(n_steps):
            for st in state:
                m, sel, st[0] = _extract_max(st[0], st[1])
                st[2].append(m)
                st[3].append(sel)

    part = PEER_TOPK // 8
    select(part)
    ha_ref[...] = hidden(ua_ref)
    select(part)
    p_prev = weighted(hb_ref, 2 * c - 1, None)
    select(part)
    accumulate(p_prev, va_ref)
    select(part)
    hb_ref[...] = hidden(ub_ref)
    select(part)
    p_cur = weighted(ha_ref, 2 * c, c < last)
    select(part)
    accumulate(p_cur, vb_ref)
    select(PEER_TOPK - 6 * part)

    (half_v, half_n), (pair_v, pair_pos) = [
        (jnp.concatenate(st[2], axis=-2), jnp.concatenate(st[3], axis=-2)) for st in state]
    for s, (h, grp, n1, n2) in enumerate(prev):
        hv_ref[s] = half_v[2 * s:2 * s + 2]
        hn_ref[s] = half_n[2 * s:2 * s + 2]
        _emit_lists(pair_v[s], pair_pos[s], n1, n2, h, grp, *next_lists)

    @pl.when(c == last)
    def _():
        out = h_ref[...] + acc_ref[...]
        o_ref[...] = _rms(out, gain_ref[...]) if normalise else out


def peer_mix(h, xn, sc, first_lists, u, v, layer, gain, normalise):
    t, d = h.shape
    tm = first_lists[0].shape[0]
    nj = first_lists[0].shape[1]
    n_tiles = t // tm
    grp = tm // V7X_LANES
    n_hp = sc.shape[0]
    n_chunks = v.shape[1] // PEER_ECHUNK
    steps = n_chunks // 2 + 1
    n_units = (n_hp // 2) * grp
    units_per_step = -(-n_units // (steps - 1))
    tok_spec = pl.BlockSpec((tm, d), lambda i, c: (i, 0))
    res_spec = pl.BlockSpec((tm, d), lambda i, c: (jnp.where(c >= 1, i, jnp.maximum(i - 1, 0)), 0))
    first_spec = pl.BlockSpec((tm, nj), lambda i, c: (0, 0))
    sc_spec = pl.BlockSpec((n_hp, grp, PEER_NKEYS, V7X_LANES), lambda i, c: (0, jnp.minimum(i + 1, n_tiles - 1), 0, 0))

    def chunk(c, offset):
        return jnp.clip(2 * c + offset, 0, n_chunks - 1)

    def u_spec(offset):
        return pl.BlockSpec((None, None, d, PEER_ECHUNK), lambda i, c: (layer, chunk(c, offset), 0, 0))

    def v_spec(offset):
        return pl.BlockSpec((None, PEER_ECHUNK, d), lambda i, c: (layer, chunk(c, offset), 0))

    return pl.pallas_call(
        functools.partial(_peer_mix_kernel, n_chunks=n_chunks, units_per_step=units_per_step, normalise=normalise),
        out_shape=jax.ShapeDtypeStruct((t, d), F32),
        grid=(n_tiles, steps),
        in_specs=[res_spec, tok_spec, first_spec, first_spec, first_spec, sc_spec, u_spec(0), u_spec(1),
                  v_spec(-1), v_spec(0), pl.BlockSpec((1, d), lambda i, c: (0, 0))],
        out_specs=tok_spec,
        scratch_shapes=[
            pltpu.VMEM((tm * W_PITCH, PEER_NKEYS), jnp.uint32),
            pltpu.VMEM((tm, PEER_ECHUNK), F32),
            pltpu.VMEM((tm, PEER_ECHUNK), F32),
            pltpu.VMEM((tm, d), F32),
            pltpu.VMEM((tm, nj), jnp.int32),
            pltpu.VMEM((tm, nj), jnp.int32),
            pltpu.VMEM((tm, nj), F32),
            *_topk_scratch(grp, nj, slots=(units_per_step,)),
        ],
        compiler_params=_params(2),
        name="peer_mix",
    )(h, xn, *first_lists, sc, u, u, v, v, gain.reshape(1, d))


def _rope_tables(seq):
    half = HEAD_DIM // 2
    inv = ROPE_THETA ** (-jnp.arange(half, dtype=F32) / half)
    ang = jnp.arange(seq, dtype=F32)[:, None] * inv[None, :]
    cos, sin = jnp.cos(ang), jnp.sin(ang)
    return jnp.concatenate([cos, cos], axis=1), jnp.concatenate([-sin, sin], axis=1)


def kernel(x, norm_mix, norm_ffn, even_w_in, even_w_out, odd_w_in, odd_conv, odd_w_out, peer_w_q, peer_sub_keys,
           peer_u, peer_v, final_norm):
    batch, seq, d = x.shape
    depth = norm_mix.shape[0]
    cos_t, sin_t = _rope_tables(seq)
    ret_tables = _retention_tables()
    mh, rh = MOBA_HEADS, RET_HEADS
    rope_blocks = tuple(range(0, 2 * mh)) + tuple(range(3 * mh, 3 * mh + 2 * rh))
    ret_k_blocks = tuple(range(3 * mh + rh, 3 * mh + 2 * rh))

    even_in, even_out_w = cast_weights(even_w_in), cast_weights(even_w_out)
    odd_in, odd_out_w = cast_weights(odd_w_in), cast_weights(odd_w_out)
    w_q, u_t, v = cast_weights(peer_w_q), cast_weights(peer_u, slab=PEER_ECHUNK), cast_weights(peer_v)
    keys = _mxu(peer_sub_keys.reshape(depth, 2 * PEER_HEADS, PEER_NKEYS, PEER_HALF))

    h = x.reshape(batch * seq, d)
    for layer in range(depth):
        i = layer // 2
        if layer % 2 == 0:
            proj = norm_proj(h, norm_mix[layer], even_in, i, cos_t, sin_t, seq, rope_blocks, ret_k_blocks,
                             HEAD_DIM ** -0.5)
            mo_lo, mo_hi = moba(proj, batch, seq)
            ro = retention(proj, ret_tables, batch, seq)
            h = even_out(h, mo_lo, mo_hi, ro, even_out_w, i, seq)
        else:
            proj = norm_proj(h, norm_mix[layer], odd_in, i, cos_t, sin_t, seq)
            h = conv_out(h, proj, odd_conv[i], odd_out_w, i, seq)
        xn, sc = peer_scores(h, norm_ffn[layer], w_q, layer, keys[layer])
        first_lists = peer_topk(sc, min(PEER_TM, h.shape[0]))
        h = peer_mix(h, xn, sc, first_lists, u_t, v, layer, final_norm, layer == depth - 1)
    return h.reshape(batch, seq, d)
```

```python
import functools

import jax
import jax.numpy as jnp
import numpy as np
from jax import lax
from jax.experimental import pallas as pl
from jax.experimental.pallas import tpu as pltpu

HEAD_DIM = 128
MOBA_HEADS = 4
RET_HEADS = 4
MOBA_BLOCK = 256
MOBA_TOPK = 3
RET_CHUNK = 128
PEER_HEADS = 8
PEER_NKEYS = 128
PEER_TOPK = 16
PEER_HALF = 128
ROPE_THETA = 10000.0
EPS = 1e-6

V7X_LANES = 128
V7X_SUBLANES = 8
V7X_VMEM_BYTES = 64 * 2**20
VMEM_LIMIT = V7X_VMEM_BYTES - 8 * 2**20

MXU_DTYPE = jnp.bfloat16
F32 = jnp.float32
NEG_INF = float("-inf")
NT_DIMS = (((1,), (1,)), ((), ()))
TN_DIMS = (((0,), (0,)), ((), ()))

CAST_ROWS = 512
PROJ_TM = 512
PROJ_COLS = 512
CONV_TM = 512
PEER_Q_TM = 512
PEER_TM = 512
PEER_ECHUNK = 2 * PEER_NKEYS
SCATTER_UNROLL = 128
W_PITCH = 72


def _params(n_axes):
    return pltpu.CompilerParams(dimension_semantics=("arbitrary",) * n_axes, vmem_limit_bytes=VMEM_LIMIT)


def _mxu(x):
    return x.astype(MXU_DTYPE)


def _rms(x, g):
    return x * lax.rsqrt(jnp.mean(x * x, axis=-1, keepdims=True) + EPS) * g


def _layer_spec(stack, layer):
    return pl.BlockSpec((None, *stack.shape[1:]), lambda *_: (layer, 0, 0))


def _cast_kernel(w_ref, o_ref, *, slab):
    if slab is None:
        o_ref[...] = _mxu(w_ref[...])
    else:
        for q in range(o_ref.shape[0]):
            o_ref[q] = _mxu(w_ref[q * slab:(q + 1) * slab, :].T)


def cast_weights(w, slab=None):
    n_layers, rows, cols = w.shape
    tr = min(rows, CAST_ROWS)
    if slab is None:
        out_shape, out_block, out_map = w.shape, (None, tr, cols), lambda l, r: (l, r, 0)
    else:
        out_shape = (n_layers, rows // slab, cols, slab)
        out_block, out_map = (None, tr // slab, cols, slab), lambda l, r: (l, r, 0, 0)
    return pl.pallas_call(
        functools.partial(_cast_kernel, slab=slab),
        out_shape=jax.ShapeDtypeStruct(out_shape, MXU_DTYPE),
        grid=(n_layers, rows // tr),
        in_specs=[pl.BlockSpec((None, tr, cols), lambda l, r: (l, r, 0))],
        out_specs=pl.BlockSpec(out_block, out_map),
        compiler_params=_params(2),
        name="cast_weights",
    )(w)


def _norm_proj_kernel(h_ref, g_ref, w_ref, cos_ref, sin_ref, o_ref, *, rope_blocks, scale_blocks, scale):
    xn = _mxu(_rms(h_ref[...], g_ref[...]))
    n = o_ref.shape[1]
    for c0 in range(0, n, PROJ_COLS):
        y = jnp.dot(xn, w_ref[:, c0:c0 + PROJ_COLS], preferred_element_type=F32)
        for u in range(PROJ_COLS // HEAD_DIM):
            blk = (c0 // HEAD_DIM) + u
            t = y[:, u * HEAD_DIM:(u + 1) * HEAD_DIM]
            if blk in rope_blocks:
                t = t * cos_ref[...] + pltpu.roll(t, HEAD_DIM // 2, axis=1) * sin_ref[...]
            if blk in scale_blocks:
                t = t * scale
            o_ref[:, blk * HEAD_DIM:(blk + 1) * HEAD_DIM] = t


def norm_proj(h, g, w, layer, cos_t, sin_t, seq, rope_blocks=(), scale_blocks=(), scale=1.0):
    t, d = h.shape
    n = w.shape[2]
    tm = PROJ_TM
    per_seq = seq // tm
    return pl.pallas_call(
        functools.partial(_norm_proj_kernel, rope_blocks=frozenset(rope_blocks), scale_blocks=frozenset(scale_blocks),
                          scale=scale),
        out_shape=jax.ShapeDtypeStruct((t, n), F32),
        grid=(t // tm,),
        in_specs=[
            pl.BlockSpec((tm, d), lambda i: (i, 0)),
            pl.BlockSpec((1, d), lambda i: (0, 0)),
            _layer_spec(w, layer),
            pl.BlockSpec((tm, HEAD_DIM), lambda i: (i % per_seq, 0)),
            pl.BlockSpec((tm, HEAD_DIM), lambda i: (i % per_seq, 0)),
        ],
        out_specs=pl.BlockSpec((tm, n), lambda i: (i, 0)),
        compiler_params=_params(1),
        name="norm_proj",
    )(h, g.reshape(1, d), w, cos_t, sin_t)


def _moba_kernel(qa_ref, qb_ref, k_ref, v_ref, oa_ref, ob_ref, kb_ref, vt_ref, km_ref, q2_ref, sel_ref, s_ref,
                 acc_ref, l_ref, *, nb, nbp):
    i = pl.program_id(2)
    blk = MOBA_BLOCK
    own = (i, nb - 1 - i)

    @pl.when(i == 0)
    def _():
        km_ref[...] = jnp.zeros_like(km_ref)
        for j in range(nb):
            rows = slice(j * blk, (j + 1) * blk)
            kb_ref[j] = _mxu(k_ref[rows, :])
            vt_ref[j] = _mxu(v_ref[rows, :].T)
            km_ref[j:j + 1, :] = jnp.sum(k_ref[rows, :], axis=0, keepdims=True) / float(blk)

    sub = lax.broadcasted_iota(jnp.int32, (nbp, blk), 0)
    kpos = lax.broadcasted_iota(jnp.int32, (blk, blk), 0)
    qpos = lax.broadcasted_iota(jnp.int32, (blk, blk), 1)

    def scores_t(j, w):
        return lax.dot_general(kb_ref[j], q2_ref[w], NT_DIMS, preferred_element_type=F32)

    m = []
    for w, q_ref in enumerate((qa_ref, qb_ref)):
        q = q_ref[...]
        q2_ref[w] = _mxu(q)
        gate = lax.dot_general(km_ref[...], q, NT_DIMS, precision=lax.Precision.HIGHEST, preferred_element_type=F32)
        rank = jnp.zeros((nbp, blk), jnp.int32)
        for jp in range(nb):
            other = gate[jp:jp + 1, :]
            beats = jnp.where(other > gate, 1, jnp.where(other == gate, jnp.where(jp < sub, 1, 0), 0))
            rank = rank + jnp.where(jp < own[w], beats, 0)
        sel_ref[w] = jnp.where(sub < own[w], jnp.where(rank < MOBA_TOPK, 1.0, 0.0), 0.0)
        s = jnp.where(kpos <= qpos, scores_t(own[w], w), NEG_INF)
        s_ref[nb - 1 + w] = s
        m.append(jnp.max(s, axis=0, keepdims=True))

    def past_pair(t):
        is_b = t >= i
        return is_b, jnp.where(is_b, 1, 0), jnp.where(is_b, t - i, t)

    for t in range(nb - 1):
        is_b, w, j = past_pair(t)
        s = jnp.where(sel_ref[w, pl.ds(j, 1), :] > 0.0, scores_t(j, w), NEG_INF)
        s_ref[t] = s
        mt = jnp.max(s, axis=0, keepdims=True)
        m = [jnp.where(is_b, m[0], jnp.maximum(m[0], mt)), jnp.where(is_b, jnp.maximum(m[1], mt), m[1])]

    c_exp = (HEAD_DIM ** -0.5) * float(np.log2(np.e))

    def weigh(slot, j, mq):
        p = jnp.exp2((s_ref[slot + jnp.minimum(i, 0)] - mq) * c_exp)
        return jnp.sum(p, axis=0, keepdims=True), jnp.dot(vt_ref[j], _mxu(p), preferred_element_type=F32)

    for w in range(2):
        l_ref[w], acc_ref[w] = weigh(nb - 1 + w, own[w], m[w])
    for t in range(nb - 1):
        is_b, w, j = past_pair(t)
        lt, at = weigh(t, j, jnp.where(is_b, m[1], m[0]))
        l_ref[w] += lt
        acc_ref[w] += at
    oa_ref[...] = (acc_ref[0] / l_ref[0]).T
    ob_ref[...] = (acc_ref[1] / l_ref[1]).T


def moba(proj, batch, seq):
    t = proj.shape[0]
    nb = seq // MOBA_BLOCK
    nbp = max(nb, V7X_SUBLANES)
    hd = HEAD_DIM
    assert nb % 2 == 0
    nh = nb // 2
    out = jax.ShapeDtypeStruct((t // 2, MOBA_HEADS * hd), F32)
    out_spec = pl.BlockSpec((MOBA_BLOCK, hd), lambda b, h, i: (b * nh + i, h))
    return pl.pallas_call(
        functools.partial(_moba_kernel, nb=nb, nbp=nbp),
        out_shape=(out, out),
        grid=(batch, MOBA_HEADS, nh),
        in_specs=[
            pl.BlockSpec((MOBA_BLOCK, hd), lambda b, h, i: (b * nb + i, h)),
            pl.BlockSpec((MOBA_BLOCK, hd), lambda b, h, i: (b * nb + nb - 1 - i, h)),
            pl.BlockSpec((seq, hd), lambda b, h, i: (b, MOBA_HEADS + h)),
            pl.BlockSpec((seq, hd), lambda b, h, i: (b, 2 * MOBA_HEADS + h)),
        ],
        out_specs=(out_spec, out_spec),
        scratch_shapes=[
            pltpu.VMEM((nb, MOBA_BLOCK, hd), MXU_DTYPE),
            pltpu.VMEM((nb, hd, MOBA_BLOCK), MXU_DTYPE),
            pltpu.VMEM((nbp, hd), F32),
            pltpu.VMEM((2, MOBA_BLOCK, hd), MXU_DTYPE),
            pltpu.VMEM((2, nbp, MOBA_BLOCK), F32),
            pltpu.VMEM((nb + 1, MOBA_BLOCK, MOBA_BLOCK), F32),
            pltpu.VMEM((2, hd, MOBA_BLOCK), F32),
            pltpu.VMEM((2, 1, MOBA_BLOCK), F32),
        ],
        compiler_params=_params(3),
        name="moba",
    )(proj, proj, proj, proj)


def _retention_kernel(q_ref, k_ref, v_ref, g_ref, dm_ref, xi_ref, ze_ref, gc_ref, o_ref, st_ref):
    @pl.when(pl.program_id(0) == 0)
    def _():
        st_ref[...] = jnp.zeros_like(st_ref)

    for b in range(q_ref.shape[0]):
        for h in range(RET_HEADS):
            sl = slice(h * HEAD_DIM, (h + 1) * HEAD_DIM)
            q = _mxu(q_ref[b, :, sl])
            k = k_ref[b, :, sl]
            v = _mxu(v_ref[b, :, sl])
            st = st_ref[b, h]
            inner = lax.dot_general(q, _mxu(k), NT_DIMS, preferred_element_type=F32) * dm_ref[h]
            o = (jnp.dot(_mxu(inner), v, preferred_element_type=F32)
                 + jnp.dot(q, _mxu(st), preferred_element_type=F32) * xi_ref[h])
            st_ref[b, h] = (st * gc_ref[h]
                            + lax.dot_general(_mxu(k * ze_ref[h]), v, TN_DIMS, preferred_element_type=F32))
            o = o * lax.rsqrt(jnp.mean(o * o, axis=-1, keepdims=True) + EPS)
            gate = g_ref[b, :, sl]
            o_ref[b, :, sl] = o * (gate * jax.nn.sigmoid(gate))


def retention(proj, tables, batch, seq):
    t, n = proj.shape
    nc = seq // RET_CHUNK
    w = RET_HEADS * HEAD_DIM
    base = 3 * MOBA_HEADS * HEAD_DIM // w
    tab_spec = pl.BlockSpec((RET_HEADS, RET_CHUNK, HEAD_DIM), lambda c: (0, 0, 0))
    proj3 = proj.reshape(batch, seq, n)

    def col_spec(block):
        return pl.BlockSpec((batch, RET_CHUNK, w), lambda c: (0, c, block))

    out = pl.pallas_call(
        _retention_kernel,
        out_shape=jax.ShapeDtypeStruct((batch, seq, w), F32),
        grid=(nc,),
        in_specs=[col_spec(base), col_spec(base + 1), col_spec(base + 2), col_spec(base + 3),
                  tab_spec, tab_spec, tab_spec, tab_spec],
        out_specs=col_spec(0),
        scratch_shapes=[pltpu.VMEM((batch, RET_HEADS, HEAD_DIM, HEAD_DIM), F32)],
        compiler_params=_params(1),
        name="retention",
    )(proj3, proj3, proj3, proj3, *tables)
    return out.reshape(t, w)


def _retention_tables():
    c = RET_CHUNK
    log_g = jnp.log(1.0 - 2.0 ** (-5.0 - jnp.arange(RET_HEADS, dtype=F32)))
    idx = jnp.arange(c, dtype=F32)
    diff = idx[:, None] - idx[None, :]
    dmask = jnp.where(diff >= 0, jnp.exp(log_g[:, None, None] * jnp.maximum(diff, 0.0)), 0.0)
    xi = jnp.exp(log_g[:, None] * (idx + 1.0))[..., None]
    zeta = jnp.exp(log_g[:, None] * (c - 1.0 - idx))[..., None]
    gc = jnp.exp(log_g * c)[:, None, None]
    full = (RET_HEADS, c, HEAD_DIM)
    return (dmask, jnp.broadcast_to(xi, full), jnp.broadcast_to(zeta, full), jnp.broadcast_to(gc, full))


def _even_out_kernel(h_ref, lo_ref, hi_ref, r_ref, w_ref, o_ref, *, nb):
    in_lo = (pl.program_id(0) % nb) < nb // 2
    mo = jnp.where(in_lo, lo_ref[...], hi_ref[...])
    y = jnp.concatenate([_mxu(mo), _mxu(r_ref[...])], axis=1)
    o_ref[...] = h_ref[...] + jnp.dot(y, w_ref[...], preferred_element_type=F32)


def even_out(h, mo_lo, mo_hi, ro, w, layer, seq):
    t, d = h.shape
    tm = MOBA_BLOCK
    nb = seq // tm
    nh = nb // 2

    def lo_map(g):
        return ((g // nb) * nh + jnp.minimum(g % nb, nh - 1), 0)

    def hi_map(g):
        return ((g // nb) * nh + (nb - 1 - jnp.maximum(g % nb, nh)), 0)

    return pl.pallas_call(
        functools.partial(_even_out_kernel, nb=nb),
        out_shape=jax.ShapeDtypeStruct((t, d), F32),
        grid=(t // tm,),
        in_specs=[
            pl.BlockSpec((tm, d), lambda g: (g, 0)),
            pl.BlockSpec((tm, mo_lo.shape[1]), lo_map),
            pl.BlockSpec((tm, mo_hi.shape[1]), hi_map),
            pl.BlockSpec((tm, ro.shape[1]), lambda g: (g, 0)),
            _layer_spec(w, layer),
        ],
        out_specs=pl.BlockSpec((tm, d), lambda g: (g, 0)),
        compiler_params=_params(1),
        name="even_out",
    )(h, mo_lo, mo_hi, ro, w)


def _conv_out_kernel(h_ref, bg_ref, cg_ref, hx_ref, cgp_ref, hxp_ref, cw_ref, w_ref, o_ref, *, per_seq):
    i = pl.program_id(0)
    u = cg_ref[...] * hx_ref[...]
    up = jnp.where(i % per_seq == 0, 0.0, cgp_ref[...] * hxp_ref[...])
    row = lax.broadcasted_iota(jnp.int32, u.shape, 0)
    p1 = up[V7X_SUBLANES - 1:V7X_SUBLANES, :]
    p2 = up[V7X_SUBLANES - 2:V7X_SUBLANES - 1, :]
    u1 = jnp.where(row == 0, p1, pltpu.roll(u, 1, axis=0))
    u2 = jnp.where(row == 0, p2, jnp.where(row == 1, p1, pltpu.roll(u, 2, axis=0)))
    y = cw_ref[0:1, :] * u2 + cw_ref[1:2, :] * u1 + cw_ref[2:3, :] * u
    z = _mxu(bg_ref[...] * y)
    o_ref[...] = h_ref[...] + jnp.dot(z, w_ref[...], preferred_element_type=F32)


def conv_out(h, proj, conv_w, w, layer, seq):
    t, d = h.shape
    tm = CONV_TM
    per_seq = seq // tm
    halo = tm // V7X_SUBLANES
    return pl.pallas_call(
        functools.partial(_conv_out_kernel, per_seq=per_seq),
        out_shape=jax.ShapeDtypeStruct((t, d), F32),
        grid=(t // tm,),
        in_specs=[
            pl.BlockSpec((tm, d), lambda i: (i, 0)),
            pl.BlockSpec((tm, d), lambda i: (i, 0)),
            pl.BlockSpec((tm, d), lambda i: (i, 1)),
            pl.BlockSpec((tm, d), lambda i: (i, 2)),
            pl.BlockSpec((V7X_SUBLANES, d), lambda i: (jnp.maximum(i * halo - 1, 0), 1)),
            pl.BlockSpec((V7X_SUBLANES, d), lambda i: (jnp.maximum(i * halo - 1, 0), 2)),
            pl.BlockSpec(conv_w.shape, lambda i: (0, 0)),
            _layer_spec(w, layer),
        ],
        out_specs=pl.BlockSpec((tm, d), lambda i: (i, 0)),
        compiler_params=_params(1),
        name="conv_out",
    )(h, proj, proj, proj, proj, proj, conv_w, w)


def _peer_scores_kernel(h_ref, g_ref, wq_ref, keys_ref, xn_ref, sc_ref):
    xn = _mxu(_rms(h_ref[...], g_ref[...]))
    xn_ref[...] = xn
    n_hp = keys_ref.shape[0]
    per = PROJ_COLS // PEER_HALF
    for c in range(n_hp // per):
        q = jnp.dot(xn, wq_ref[:, c * PROJ_COLS:(c + 1) * PROJ_COLS], preferred_element_type=F32)
        for u in range(per):
            hp = c * per + u
            qs = _mxu(q[:, u * PEER_HALF:(u + 1) * PEER_HALF])
            sc = lax.dot_general(keys_ref[hp], qs, NT_DIMS, preferred_element_type=F32)
            for grp in range(sc_ref.shape[1]):
                sc_ref[hp, grp] = sc[:, grp * V7X_LANES:(grp + 1) * V7X_LANES]


def peer_scores(h, g, wq, layer, keys):
    t, d = h.shape
    tm = PEER_Q_TM
    n_hp = keys.shape[0]
    grp = tm // V7X_LANES
    return pl.pallas_call(
        _peer_scores_kernel,
        out_shape=(jax.ShapeDtypeStruct((t, d), MXU_DTYPE),
                   jax.ShapeDtypeStruct((n_hp, t // V7X_LANES, PEER_NKEYS, V7X_LANES), F32)),
        grid=(t // tm,),
        in_specs=[
            pl.BlockSpec((tm, d), lambda i: (i, 0)),
            pl.BlockSpec((1, d), lambda i: (0, 0)),
            _layer_spec(wq, layer),
            pl.BlockSpec(keys.shape, lambda i: (0, 0, 0)),
        ],
        out_specs=(pl.BlockSpec((tm, d), lambda i: (i, 0)),
                   pl.BlockSpec((n_hp, grp, PEER_NKEYS, V7X_LANES), lambda i: (0, i, 0, 0))),
        compiler_params=_params(1),
        name="peer_scores",
    )(h, g.reshape(1, d), wq, keys)


def _extract_max(x, pos):
    m = jnp.max(x, axis=-2, keepdims=True)
    sel = jnp.min(jnp.where(x == m, pos, float(2**30)), axis=-2, keepdims=True)
    return m, sel, jnp.where(pos == sel, NEG_INF, x)


def _topk_rows(x, pos, k):
    vals, ids = [], []
    for _ in range(k):
        m, sel, x = _extract_max(x, pos)
        vals.append(m)
        ids.append(sel)
    return jnp.concatenate(vals, axis=-2), jnp.concatenate(ids, axis=-2)


def _lookup_rows(table, idx):
    out = jnp.zeros(idx.shape, table.dtype)
    for a in range(table.shape[0]):
        out = jnp.where(idx == a, table[a:a + 1, :], out)
    return out


def _key_positions():
    return lax.broadcasted_iota(jnp.int32, (PEER_NKEYS, V7X_LANES), 0).astype(F32)


def _pair_candidates(v1, v2):
    k = PEER_TOPK
    n = V7X_LANES
    r16 = lax.broadcasted_iota(jnp.int32, (k, n), 0)
    r8 = lax.broadcasted_iota(jnp.int32, (V7X_SUBLANES, n), 0)
    cand, pos = [v1[0:1, :] + v2], [r16]
    for a in range(1, V7X_SUBLANES):
        piece = v1[a:a + 1, :] + v2[0:V7X_SUBLANES, :]
        cand.append(jnp.where(r8 < k // (a + 1), piece, NEG_INF))
        pos.append(a * k + r8)
    cand.append(v1[V7X_SUBLANES:, :] + v2[0:1, :])
    pos.append((V7X_SUBLANES + r8) * k)
    return jnp.concatenate(cand, axis=0), jnp.concatenate(pos, axis=0).astype(F32)


def _emit_lists(top_s, top_pos, n1, n2, h, grp, l1_ref, l2_ref, lg_ref):
    k = PEER_TOPK
    top_pos = top_pos.astype(jnp.int32)
    e1 = _lookup_rows(n1, lax.shift_right_logical(top_pos, k.bit_length() - 1))
    e2 = _lookup_rows(n2, top_pos & (k - 1))
    ex = jnp.exp(top_s - top_s[0:1, :])
    gate = ex / jnp.sum(ex, axis=0, keepdims=True)
    rows = pl.ds(pl.multiple_of(h * k, k), k)
    l1_ref[grp, rows, :] = e1.astype(jnp.int32)
    l2_ref[grp, rows, :] = e2.astype(jnp.int32)
    lg_ref[grp, rows, :] = gate


def _transpose_lists(src_refs, dst_refs):
    for src, dst in zip(src_refs, dst_refs):
        for grp in range(src.shape[0]):
            dst[grp * V7X_LANES:(grp + 1) * V7X_LANES, :] = src[grp].T


def _peer_topk_kernel(sc_ref, i1_ref, i2_ref, g_ref, l1_ref, l2_ref, lg_ref):
    n_heads = sc_ref.shape[0] // 2
    n_units = n_heads * sc_ref.shape[1]

    def unit_pair(n, carry):
        units = [(u % n_heads, u // n_heads) for u in (n, n + n_units // 2)]
        x = jnp.stack([sc_ref[2 * h + p, grp] for h, grp in units for p in range(2)])
        hv, hn = _topk_rows(x, _key_positions(), PEER_TOPK)
        cands = [_pair_candidates(hv[2 * k], hv[2 * k + 1]) for k in range(2)]
        top_s, top_pos = _topk_rows(jnp.stack([cd[0] for cd in cands]), jnp.stack([cd[1] for cd in cands]), PEER_TOPK)
        for k, (h, grp) in enumerate(units):
            _emit_lists(top_s[k], top_pos[k], hn[2 * k], hn[2 * k + 1], h, grp, l1_ref, l2_ref, lg_ref)
        return carry

    lax.fori_loop(0, n_units // 2, unit_pair, 0)
    _transpose_lists((l1_ref, l2_ref, lg_ref), (i1_ref, i2_ref, g_ref))


def _topk_scratch(n_groups, nj, slots=()):
    lists = (n_groups, nj, V7X_LANES)
    halves = (*slots, 2, PEER_TOPK, V7X_LANES)
    return [pltpu.VMEM(lists, jnp.int32), pltpu.VMEM(lists, jnp.int32), pltpu.VMEM(lists, F32),
            pltpu.VMEM(halves, F32), pltpu.VMEM(halves, F32)]


def peer_topk(sc, n_tokens):
    n_hp, _, nk, _ = sc.shape
    grp = n_tokens // V7X_LANES
    nj = (n_hp // 2) * PEER_TOPK
    out_spec = pl.BlockSpec((n_tokens, nj), lambda i: (0, 0))
    return pl.pallas_call(
        _peer_topk_kernel,
        out_shape=(jax.ShapeDtypeStruct((n_tokens, nj), jnp.int32), jax.ShapeDtypeStruct((n_tokens, nj), jnp.int32),
                   jax.ShapeDtypeStruct((n_tokens, nj), F32)),
        grid=(1,),
        in_specs=[pl.BlockSpec((n_hp, grp, nk, V7X_LANES), lambda i: (0, 0, 0, 0))],
        out_specs=(out_spec, out_spec, out_spec),
        scratch_shapes=_topk_scratch(grp, nj)[:3],
        compiler_params=_params(1),
        name="peer_topk",
    )(sc)


def _peer_mix_kernel(h_ref, xn_ref, f1_ref, f2_ref, fg_ref, sc_ref, ua_ref, ub_ref, va_ref, vb_ref, gain_ref, o_ref,
                     w_ref, ha_ref, hb_ref, acc_ref, i1_ref, i2_ref, g_ref, l1_ref, l2_ref, lg_ref, hv_ref, hn_ref,
                     *, n_chunks, units_per_step, normalise):
    tile = pl.program_id(0)
    c = pl.program_id(1)
    last = pl.num_programs(1) - 1
    tm = xn_ref.shape[0]
    nk = PEER_NKEYS
    half = nk // 2
    next_lists = (l1_ref, l2_ref, lg_ref)
    lists = (i1_ref, i2_ref, g_ref)

    @pl.when((c == 0) & (tile == 0))
    def _():
        for src, dst in zip((f1_ref, f2_ref, fg_ref), lists):
            dst[...] = src[...]
        hv_ref[...] = jnp.zeros_like(hv_ref)
        hn_ref[...] = jnp.zeros_like(hn_ref)

    @pl.when((c == 0) & (tile > 0))
    def _():
        _transpose_lists(next_lists, lists)

    @pl.when(c == 0)
    def _():
        acc_ref[...] = jnp.zeros_like(acc_ref)
        hb_ref[...] = jnp.zeros_like(hb_ref)
        r = lax.broadcasted_iota(jnp.int32, (nk, nk), 0).astype(F32).astype(jnp.bfloat16)
        one = jnp.ones((nk, nk), jnp.bfloat16)
        zero = jnp.zeros((nk, nk), jnp.bfloat16)

        def one_hots(t):
            i1 = i1_ref[pl.ds(t, 1), :].astype(F32).astype(jnp.bfloat16)
            i2 = i2_ref[pl.ds(t, 1), :].astype(F32).astype(jnp.bfloat16)
            g = (0.5 * g_ref[pl.ds(t, 1), :]).astype(jnp.bfloat16)
            return jnp.where(r == i1, g, zero), jnp.where(r == i2, one, zero)

        def tok_pair(n, carry):
            (a0, b0), (a1, b1) = one_hots(2 * n), one_hots(2 * n + 1)
            w = lax.dot_general(jnp.concatenate([a0, a1], axis=0), jnp.concatenate([b0, b1], axis=0), NT_DIMS,
                                preferred_element_type=F32)
            for k in range(2):
                wk = w[k * nk:(k + 1) * nk, k * nk:(k + 1) * nk].astype(jnp.bfloat16)
                dst = pl.ds(pl.multiple_of((2 * n + k) * W_PITCH, V7X_SUBLANES), half)
                w_ref[dst, :] = pltpu.bitcast(wk, jnp.uint32)
            return carry

        lax.fori_loop(0, tm // 2, tok_pair, 0, unroll=SCATTER_UNROLL // 2)

    def hidden(u_ref):
        return jnp.dot(xn_ref[...], u_ref[...], preferred_element_type=F32)

    def weighted(hid_ref, chunk, valid):
        pairs = PEER_ECHUNK // (2 * nk)
        first = jnp.clip(chunk, 0, n_chunks - 1) * pairs
        hid = hid_ref[...]
        act = hid * (1.0 + lax.erf(hid * (2.0 ** -0.5)))
        cols = []
        for q in range(pairs):
            packed = w_ref[pl.ds(first + q, tm, stride=W_PITCH), :]
            if valid is not None:
                packed = jnp.where(valid, packed, jnp.uint32(0))
            w_even = lax.bitcast_convert_type(packed << 16, F32)
            w_odd = lax.bitcast_convert_type(packed & jnp.uint32(0xFFFF0000), F32)
            cols += [w_even * act[:, 2 * q * nk:(2 * q + 1) * nk], w_odd * act[:, (2 * q + 1) * nk:(2 * q + 2) * nk]]
        return _mxu(jnp.concatenate(cols, axis=1))

    def accumulate(p, v_ref):
        acc_ref[...] += jnp.dot(p, v_ref[...], preferred_element_type=F32)

    n_heads = sc_ref.shape[0] // 2
    n_units = n_heads * sc_ref.shape[1]
    xs, cands, cand_poss, prev = [], [], [], []
    for s in range(units_per_step):
        n = jnp.minimum(c * units_per_step + s, n_units - 1)
        h, grp = n % n_heads, n // n_heads
        xs += [sc_ref[2 * h, grp], sc_ref[2 * h + 1, grp]]
        cand, cand_pos = _pair_candidates(hv_ref[s, 0], hv_ref[s, 1])
        cands.append(cand)
        cand_poss.append(cand_pos)
        n = jnp.clip((c - 1) * units_per_step + s, 0, n_units - 1)
        prev.append((n % n_heads, n // n_heads, hn_ref[s, 0], hn_ref[s, 1]))
    state = [[jnp.stack(xs), _key_positions(), [], []], [jnp.stack(cands), jnp.stack(cand_poss), [], []]]

    def select(n_steps):
        for _ in range(n_steps):
            for st in state:
                m, sel, st[0] = _extract_max(st[0], st[1])
                st[2].append(m)
                st[3].append(sel)

    part = PEER_TOPK // 8
    select(part)
    ha_ref[...] = hidden(ua_ref)
    select(part)
    p_prev = weighted(hb_ref, 2 * c - 1, None)
    select(part)
    accumulate(p_prev, va_ref)
    select(part)
    hb_ref[...] = hidden(ub_ref)
    select(part)
    p_cur = weighted(ha_ref, 2 * c, c < last)
    select(part)
    accumulate(p_cur, vb_ref)
    select(PEER_TOPK - 6 * part)

    (half_v, half_n), (pair_v, pair_pos) = [
        (jnp.concatenate(st[2], axis=-2), jnp.concatenate(st[3], axis=-2)) for st in state]
    for s, (h, grp, n1, n2) in enumerate(prev):
        hv_ref[s] = half_v[2 * s:2 * s + 2]
        hn_ref[s] = half_n[2 * s:2 * s + 2]
        _emit_lists(pair_v[s], pair_pos[s], n1, n2, h, grp, *next_lists)

    @pl.when(c == last)
    def _():
        out = h_ref[...] + acc_ref[...]
        o_ref[...] = _rms(out, gain_ref[...]) if normalise else out


def peer_mix(h, xn, sc, first_lists, u, v, layer, gain, normalise):
    t, d = h.shape
    tm = first_lists[0].shape[0]
    nj = first_lists[0].shape[1]
    n_tiles = t // tm
    grp = tm // V7X_LANES
    n_hp = sc.shape[0]
    n_chunks = v.shape[1] // PEER_ECHUNK
    steps = n_chunks // 2 + 1
    n_units = (n_hp // 2) * grp
    units_per_step = -(-n_units // (steps - 1))
    tok_spec = pl.BlockSpec((tm, d), lambda i, c: (i, 0))
    res_spec = pl.BlockSpec((tm, d), lambda i, c: (jnp.where(c >= 1, i, jnp.maximum(i - 1, 0)), 0))
    first_spec = pl.BlockSpec((tm, nj), lambda i, c: (0, 0))
    sc_spec = pl.BlockSpec((n_hp, grp, PEER_NKEYS, V7X_LANES), lambda i, c: (0, jnp.minimum(i + 1, n_tiles - 1), 0, 0))

    def chunk(c, offset):
        return jnp.clip(2 * c + offset, 0, n_chunks - 1)

    def u_spec(offset):
        return pl.BlockSpec((None, None, d, PEER_ECHUNK), lambda i, c: (layer, chunk(c, offset), 0, 0))

    def v_spec(offset):
        return pl.BlockSpec((None, PEER_ECHUNK, d), lambda i, c: (layer, chunk(c, offset), 0))

    return pl.pallas_call(
        functools.partial(_peer_mix_kernel, n_chunks=n_chunks, units_per_step=units_per_step, normalise=normalise),
        out_shape=jax.ShapeDtypeStruct((t, d), F32),
        grid=(n_tiles, steps),
        in_specs=[res_spec, tok_spec, first_spec, first_spec, first_spec, sc_spec, u_spec(0), u_spec(1),
                  v_spec(-1), v_spec(0), pl.BlockSpec((1, d), lambda i, c: (0, 0))],
        out_specs=tok_spec,
        scratch_shapes=[
            pltpu.VMEM((tm * W_PITCH, PEER_NKEYS), jnp.uint32),
            pltpu.VMEM((tm, PEER_ECHUNK), F32),
            pltpu.VMEM((tm, PEER_ECHUNK), F32),
            pltpu.VMEM((tm, d), F32),
            pltpu.VMEM((tm, nj), jnp.int32),
            pltpu.VMEM((tm, nj), jnp.int32),
            pltpu.VMEM((tm, nj), F32),
            *_topk_scratch(grp, nj, slots=(units_per_step,)),
        ],
        compiler_params=_params(2),
        name="peer_mix",
    )(h, xn, *first_lists, sc, u, u, v, v, gain.reshape(1, d))


def _rope_tables(seq):
    half = HEAD_DIM // 2
    inv = ROPE_THETA ** (-jnp.arange(half, dtype=F32) / half)
    ang = jnp.arange(seq, dtype=F32)[:, None] * inv[None, :]
    cos, sin = jnp.cos(ang), jnp.sin(ang)
    return jnp.concatenate([cos, cos], axis=1), jnp.concatenate([-sin, sin], axis=1)


def kernel(x, norm_mix, norm_ffn, even_w_in, even_w_out, odd_w_in, odd_conv, odd_w_out, peer_w_q, peer_sub_keys,
           peer_u, peer_v, final_norm):
    batch, seq, d = x.shape
    depth = norm_mix.shape[0]
    cos_t, sin_t = _rope_tables(seq)
    ret_tables = _retention_tables()
    mh, rh = MOBA_HEADS, RET_HEADS
    rope_blocks = tuple(range(0, 2 * mh)) + tuple(range(3 * mh, 3 * mh + 2 * rh))
    ret_k_blocks = tuple(range(3 * mh + rh, 3 * mh + 2 * rh))

    even_in, even_out_w = cast_weights(even_w_in), cast_weights(even_w_out)
    odd_in, odd_out_w = cast_weights(odd_w_in), cast_weights(odd_w_out)
    w_q, u_t, v = cast_weights(peer_w_q), cast_weights(peer_u, slab=PEER_ECHUNK), cast_weights(peer_v)
    keys = _mxu(peer_sub_keys.reshape(depth, 2 * PEER_HEADS, PEER_NKEYS, PEER_HALF))

    h = x.reshape(batch * seq, d)
    for layer in range(depth):
        i = layer // 2
        if layer % 2 == 0:
            proj = norm_proj(h, norm_mix[layer], even_in, i, cos_t, sin_t, seq, rope_blocks, ret_k_blocks,
                             HEAD_DIM ** -0.5)
            mo_lo, mo_hi = moba(proj, batch, seq)
            ro = retention(proj, ret_tables, batch, seq)
            h = even_out(h, mo_lo, mo_hi, ro, even_out_w, i, seq)
        else:
            proj = norm_proj(h, norm_mix[layer], odd_in, i, cos_t, sin_t, seq)
            h = conv_out(h, proj, odd_conv[i], odd_out_w, i, seq)
        xn, sc = peer_scores(h, norm_ffn[layer], w_q, layer, keys[layer])
        first_lists = peer_topk(sc, min(PEER_TM, h.shape[0]))
        h = peer_mix(h, xn, sc, first_lists, u_t, v, layer, final_norm, layer == depth - 1)
    return h.reshape(batch, seq, d)
```

```python
import functools

import jax
import jax.numpy as jnp
import numpy as np
from jax import lax
from jax.experimental import pallas as pl
from jax.experimental.pallas import tpu as pltpu

HEAD_DIM = 128
MOBA_HEADS = 4
RET_HEADS = 4
MOBA_BLOCK = 256
MOBA_TOPK = 3
RET_CHUNK = 128
PEER_HEADS = 8
PEER_NKEYS = 128
PEER_TOPK = 16
PEER_HALF = 128
ROPE_THETA = 10000.0
EPS = 1e-6

V7X_LANES = 128
V7X_SUBLANES = 8
V7X_VMEM_BYTES = 64 * 2**20
VMEM_LIMIT = V7X_VMEM_BYTES - 8 * 2**20

MXU_DTYPE = jnp.bfloat16
F32 = jnp.float32
NEG_INF = float("-inf")
NT_DIMS = (((1,), (1,)), ((), ()))
TN_DIMS = (((0,), (0,)), ((), ()))

CAST_ROWS = 2048
PROJ_TM = 512
PROJ_COLS = 512
CONV_TM = 512
PEER_Q_TM = 512
PEER_TM = 512
PEER_ECHUNK = 2 * PEER_NKEYS
SCATTER_UNROLL = 256
SELECT_SPLIT = (2, 2, 2, 2, 2, 2, 4)
W_PITCH = 72


def _params(n_axes):
    return pltpu.CompilerParams(dimension_semantics=("arbitrary",) * n_axes, vmem_limit_bytes=VMEM_LIMIT)


def _mxu(x):
    return x.astype(MXU_DTYPE)


def _rms(x, g):
    return x * lax.rsqrt(jnp.mean(x * x, axis=-1, keepdims=True) + EPS) * g


def _layer_spec(stack, layer):
    return pl.BlockSpec((None, *stack.shape[1:]), lambda *_: (layer, 0, 0))


def _cast_kernel(w_ref, o_ref, *, slab):
    if slab is None:
        o_ref[...] = _mxu(w_ref[...])
    else:
        for q in range(o_ref.shape[0]):
            o_ref[q] = _mxu(w_ref[q * slab:(q + 1) * slab, :].T)


def cast_weights(w, slab=None):
    n_layers, rows, cols = w.shape
    tr = min(rows, CAST_ROWS)
    if slab is None:
        out_shape, out_block, out_map = w.shape, (None, tr, cols), lambda l, r: (l, r, 0)
    else:
        out_shape = (n_layers, rows // slab, cols, slab)
        out_block, out_map = (None, tr // slab, cols, slab), lambda l, r: (l, r, 0, 0)
    return pl.pallas_call(
        functools.partial(_cast_kernel, slab=slab),
        out_shape=jax.ShapeDtypeStruct(out_shape, MXU_DTYPE),
        grid=(n_layers, rows // tr),
        in_specs=[pl.BlockSpec((None, tr, cols), lambda l, r: (l, r, 0))],
        out_specs=pl.BlockSpec(out_block, out_map),
        compiler_params=_params(2),
        name="cast_weights",
    )(w)


def _norm_proj_kernel(h_ref, g_ref, w_ref, cos_ref, sin_ref, o_ref, *, rope_blocks, scale_blocks, scale):
    xn = _mxu(_rms(h_ref[...], g_ref[...]))
    n = o_ref.shape[1]
    for c0 in range(0, n, PROJ_COLS):
        y = jnp.dot(xn, w_ref[:, c0:c0 + PROJ_COLS], preferred_element_type=F32)
        for u in range(PROJ_COLS // HEAD_DIM):
            blk = (c0 // HEAD_DIM) + u
            t = y[:, u * HEAD_DIM:(u + 1) * HEAD_DIM]
            if blk in rope_blocks:
                t = t * cos_ref[...] + pltpu.roll(t, HEAD_DIM // 2, axis=1) * sin_ref[...]
            if blk in scale_blocks:
                t = t * scale
            o_ref[:, blk * HEAD_DIM:(blk + 1) * HEAD_DIM] = t


def norm_proj(h, g, w, layer, cos_t, sin_t, seq, rope_blocks=(), scale_blocks=(), scale=1.0):
    t, d = h.shape
    n = w.shape[2]
    tm = PROJ_TM
    per_seq = seq // tm
    return pl.pallas_call(
        functools.partial(_norm_proj_kernel, rope_blocks=frozenset(rope_blocks), scale_blocks=frozenset(scale_blocks),
                          scale=scale),
        out_shape=jax.ShapeDtypeStruct((t, n), F32),
        grid=(t // tm,),
        in_specs=[
            pl.BlockSpec((tm, d), lambda i: (i, 0)),
            pl.BlockSpec((1, d), lambda i: (0, 0)),
            _layer_spec(w, layer),
            pl.BlockSpec((tm, HEAD_DIM), lambda i: (i % per_seq, 0)),
            pl.BlockSpec((tm, HEAD_DIM), lambda i: (i % per_seq, 0)),
        ],
        out_specs=pl.BlockSpec((tm, n), lambda i: (i, 0)),
        compiler_params=_params(1),
        name="norm_proj",
    )(h, g.reshape(1, d), w, cos_t, sin_t)


def _moba_kernel(qa_ref, qb_ref, k_ref, v_ref, oa_ref, ob_ref, kb_ref, vt_ref, km_ref, q2_ref, sel_ref, s_ref,
                 acc_ref, l_ref, *, nb, nbp):
    i = pl.program_id(2)
    blk = MOBA_BLOCK
    own = (i, nb - 1 - i)

    @pl.when(i == 0)
    def _():
        km_ref[...] = jnp.zeros_like(km_ref)
        for j in range(nb):
            rows = slice(j * blk, (j + 1) * blk)
            kb_ref[j] = _mxu(k_ref[rows, :])
            vt_ref[j] = _mxu(v_ref[rows, :].T)
            km_ref[j:j + 1, :] = jnp.sum(k_ref[rows, :], axis=0, keepdims=True) / float(blk)

    sub = lax.broadcasted_iota(jnp.int32, (nbp, blk), 0)
    kpos = lax.broadcasted_iota(jnp.int32, (blk, blk), 0)
    qpos = lax.broadcasted_iota(jnp.int32, (blk, blk), 1)

    def scores_t(j, w):
        return lax.dot_general(kb_ref[j], q2_ref[w], NT_DIMS, preferred_element_type=F32)

    m = []
    for w, q_ref in enumerate((qa_ref, qb_ref)):
        q = q_ref[...]
        q2_ref[w] = _mxu(q)
        gate = lax.dot_general(km_ref[...], q, NT_DIMS, precision=lax.Precision.HIGHEST, preferred_element_type=F32)
        rank = jnp.zeros((nbp, blk), jnp.int32)
        for jp in range(nb):
            other = gate[jp:jp + 1, :]
            beats = jnp.where(other > gate, 1, jnp.where(other == gate, jnp.where(jp < sub, 1, 0), 0))
            rank = rank + jnp.where(jp < own[w], beats, 0)
        sel_ref[w] = jnp.where(sub < own[w], jnp.where(rank < MOBA_TOPK, 1.0, 0.0), 0.0)
        s = jnp.where(kpos <= qpos, scores_t(own[w], w), NEG_INF)
        s_ref[nb - 1 + w] = s
        m.append(jnp.max(s, axis=0, keepdims=True))

    def past_pair(t):
        is_b = t >= i
        return is_b, jnp.where(is_b, 1, 0), jnp.where(is_b, t - i, t)

    for t in range(nb - 1):
        is_b, w, j = past_pair(t)
        s = jnp.where(sel_ref[w, pl.ds(j, 1), :] > 0.0, scores_t(j, w), NEG_INF)
        s_ref[t] = s
        mt = jnp.max(s, axis=0, keepdims=True)
        m = [jnp.where(is_b, m[0], jnp.maximum(m[0], mt)), jnp.where(is_b, jnp.maximum(m[1], mt), m[1])]

    c_exp = (HEAD_DIM ** -0.5) * float(np.log2(np.e))

    def weigh(slot, j, mq):
        p = jnp.exp2((s_ref[slot + jnp.minimum(i, 0)] - mq) * c_exp)
        return jnp.sum(p, axis=0, keepdims=True), jnp.dot(vt_ref[j], _mxu(p), preferred_element_type=F32)

    for w in range(2):
        l_ref[w], acc_ref[w] = weigh(nb - 1 + w, own[w], m[w])
    for t in range(nb - 1):
        is_b, w, j = past_pair(t)
        lt, at = weigh(t, j, jnp.where(is_b, m[1], m[0]))
        l_ref[w] += lt
        acc_ref[w] += at
    oa_ref[...] = (acc_ref[0] / l_ref[0]).T
    ob_ref[...] = (acc_ref[1] / l_ref[1]).T


def moba(proj, batch, seq):
    t = proj.shape[0]
    nb = seq // MOBA_BLOCK
    nbp = max(nb, V7X_SUBLANES)
    hd = HEAD_DIM
    assert nb % 2 == 0
    nh = nb // 2
    out = jax.ShapeDtypeStruct((t // 2, MOBA_HEADS * hd), F32)
    out_spec = pl.BlockSpec((MOBA_BLOCK, hd), lambda b, h, i: (b * nh + i, h))
    return pl.pallas_call(
        functools.partial(_moba_kernel, nb=nb, nbp=nbp),
        out_shape=(out, out),
        grid=(batch, MOBA_HEADS, nh),
        in_specs=[
            pl.BlockSpec((MOBA_BLOCK, hd), lambda b, h, i: (b * nb + i, h)),
            pl.BlockSpec((MOBA_BLOCK, hd), lambda b, h, i: (b * nb + nb - 1 - i, h)),
            pl.BlockSpec((seq, hd), lambda b, h, i: (b, MOBA_HEADS + h)),
            pl.BlockSpec((seq, hd), lambda b, h, i: (b, 2 * MOBA_HEADS + h)),
        ],
        out_specs=(out_spec, out_spec),
        scratch_shapes=[
            pltpu.VMEM((nb, MOBA_BLOCK, hd), MXU_DTYPE),
            pltpu.VMEM((nb, hd, MOBA_BLOCK), MXU_DTYPE),
            pltpu.VMEM((nbp, hd), F32),
            pltpu.VMEM((2, MOBA_BLOCK, hd), MXU_DTYPE),
            pltpu.VMEM((2, nbp, MOBA_BLOCK), F32),
            pltpu.VMEM((nb + 1, MOBA_BLOCK, MOBA_BLOCK), F32),
            pltpu.VMEM((2, hd, MOBA_BLOCK), F32),
            pltpu.VMEM((2, 1, MOBA_BLOCK), F32),
        ],
        compiler_params=_params(3),
        name="moba",
    )(proj, proj, proj, proj)


def _retention_kernel(q_ref, k_ref, v_ref, g_ref, dm_ref, xi_ref, ze_ref, gc_ref, o_ref, st_ref):
    @pl.when(pl.program_id(0) == 0)
    def _():
        st_ref[...] = jnp.zeros_like(st_ref)

    for b in range(q_ref.shape[0]):
        for h in range(RET_HEADS):
            sl = slice(h * HEAD_DIM, (h + 1) * HEAD_DIM)
            q = _mxu(q_ref[b, :, sl])
            k = k_ref[b, :, sl]
            v = _mxu(v_ref[b, :, sl])
            st = st_ref[b, h]
            inner = lax.dot_general(q, _mxu(k), NT_DIMS, preferred_element_type=F32) * dm_ref[h]
            o = (jnp.dot(_mxu(inner), v, preferred_element_type=F32)
                 + jnp.dot(q, _mxu(st), preferred_element_type=F32) * xi_ref[h])
            st_ref[b, h] = (st * gc_ref[h]
                            + lax.dot_general(_mxu(k * ze_ref[h]), v, TN_DIMS, preferred_element_type=F32))
            o = o * lax.rsqrt(jnp.mean(o * o, axis=-1, keepdims=True) + EPS)
            gate = g_ref[b, :, sl]
            o_ref[b, :, sl] = o * (gate * jax.nn.sigmoid(gate))


def retention(proj, tables, batch, seq):
    t, n = proj.shape
    nc = seq // RET_CHUNK
    w = RET_HEADS * HEAD_DIM
    base = 3 * MOBA_HEADS * HEAD_DIM // w
    tab_spec = pl.BlockSpec((RET_HEADS, RET_CHUNK, HEAD_DIM), lambda c: (0, 0, 0))
    proj3 = proj.reshape(batch, seq, n)

    def col_spec(block):
        return pl.BlockSpec((batch, RET_CHUNK, w), lambda c: (0, c, block))

    out = pl.pallas_call(
        _retention_kernel,
        out_shape=jax.ShapeDtypeStruct((batch, seq, w), F32),
        grid=(nc,),
        in_specs=[col_spec(base), col_spec(base + 1), col_spec(base + 2), col_spec(base + 3),
                  tab_spec, tab_spec, tab_spec, tab_spec],
        out_specs=col_spec(0),
        scratch_shapes=[pltpu.VMEM((batch, RET_HEADS, HEAD_DIM, HEAD_DIM), F32)],
        compiler_params=_params(1),
        name="retention",
    )(proj3, proj3, proj3, proj3, *tables)
    return out.reshape(t, w)


def _retention_tables():
    c = RET_CHUNK
    log_g = jnp.log(1.0 - 2.0 ** (-5.0 - jnp.arange(RET_HEADS, dtype=F32)))
    idx = jnp.arange(c, dtype=F32)
    diff = idx[:, None] - idx[None, :]
    dmask = jnp.where(diff >= 0, jnp.exp(log_g[:, None, None] * jnp.maximum(diff, 0.0)), 0.0)
    xi = jnp.exp(log_g[:, None] * (idx + 1.0))[..., None]
    zeta = jnp.exp(log_g[:, None] * (c - 1.0 - idx))[..., None]
    gc = jnp.exp(log_g * c)[:, None, None]
    full = (RET_HEADS, c, HEAD_DIM)
    return (dmask, jnp.broadcast_to(xi, full), jnp.broadcast_to(zeta, full), jnp.broadcast_to(gc, full))


def _even_out_kernel(h_ref, lo_ref, hi_ref, r_ref, w_ref, o_ref, *, nb):
    in_lo = (pl.program_id(0) % nb) < nb // 2
    mo = jnp.where(in_lo, lo_ref[...], hi_ref[...])
    y = jnp.concatenate([_mxu(mo), _mxu(r_ref[...])], axis=1)
    o_ref[...] = h_ref[...] + jnp.dot(y, w_ref[...], preferred_element_type=F32)


def even_out(h, mo_lo, mo_hi, ro, w, layer, seq):
    t, d = h.shape
    tm = MOBA_BLOCK
    nb = seq // tm
    nh = nb // 2

    def lo_map(g):
        return ((g // nb) * nh + jnp.minimum(g % nb, nh - 1), 0)

    def hi_map(g):
        return ((g // nb) * nh + (nb - 1 - jnp.maximum(g % nb, nh)), 0)

    return pl.pallas_call(
        functools.partial(_even_out_kernel, nb=nb),
        out_shape=jax.ShapeDtypeStruct((t, d), F32),
        grid=(t // tm,),
        in_specs=[
            pl.BlockSpec((tm, d), lambda g: (g, 0)),
            pl.BlockSpec((tm, mo_lo.shape[1]), lo_map),
            pl.BlockSpec((tm, mo_hi.shape[1]), hi_map),
            pl.BlockSpec((tm, ro.shape[1]), lambda g: (g, 0)),
            _layer_spec(w, layer),
        ],
        out_specs=pl.BlockSpec((tm, d), lambda g: (g, 0)),
        compiler_params=_params(1),
        name="even_out",
    )(h, mo_lo, mo_hi, ro, w)


def _conv_out_kernel(h_ref, bg_ref, cg_ref, hx_ref, cgp_ref, hxp_ref, cw_ref, w_ref, o_ref, *, per_seq):
    i = pl.program_id(0)
    u = cg_ref[...] * hx_ref[...]
    up = jnp.where(i % per_seq == 0, 0.0, cgp_ref[...] * hxp_ref[...])
    row = lax.broadcasted_iota(jnp.int32, u.shape, 0)
    p1 = up[V7X_SUBLANES - 1:V7X_SUBLANES, :]
    p2 = up[V7X_SUBLANES - 2:V7X_SUBLANES - 1, :]
    u1 = jnp.where(row == 0, p1, pltpu.roll(u, 1, axis=0))
    u2 = jnp.where(row == 0, p2, jnp.where(row == 1, p1, pltpu.roll(u, 2, axis=0)))
    y = cw_ref[0:1, :] * u2 + cw_ref[1:2, :] * u1 + cw_ref[2:3, :] * u
    z = _mxu(bg_ref[...] * y)
    o_ref[...] = h_ref[...] + jnp.dot(z, w_ref[...], preferred_element_type=F32)


def conv_out(h, proj, conv_w, w, layer, seq):
    t, d = h.shape
    tm = CONV_TM
    per_seq = seq // tm
    halo = tm // V7X_SUBLANES
    return pl.pallas_call(
        functools.partial(_conv_out_kernel, per_seq=per_seq),
        out_shape=jax.ShapeDtypeStruct((t, d), F32),
        grid=(t // tm,),
        in_specs=[
            pl.BlockSpec((tm, d), lambda i: (i, 0)),
            pl.BlockSpec((tm, d), lambda i: (i, 0)),
            pl.BlockSpec((tm, d), lambda i: (i, 1)),
            pl.BlockSpec((tm, d), lambda i: (i, 2)),
            pl.BlockSpec((V7X_SUBLANES, d), lambda i: (jnp.maximum(i * halo - 1, 0), 1)),
            pl.BlockSpec((V7X_SUBLANES, d), lambda i: (jnp.maximum(i * halo - 1, 0), 2)),
            pl.BlockSpec(conv_w.shape, lambda i: (0, 0)),
            _layer_spec(w, layer),
        ],
        out_specs=pl.BlockSpec((tm, d), lambda i: (i, 0)),
        compiler_params=_params(1),
        name="conv_out",
    )(h, proj, proj, proj, proj, proj, conv_w, w)


def _peer_scores_kernel(h_ref, g_ref, wq_ref, keys_ref, xn_ref, sc_ref):
    xn = _mxu(_rms(h_ref[...], g_ref[...]))
    xn_ref[...] = xn
    n_hp = keys_ref.shape[0]
    per = PROJ_COLS // PEER_HALF
    for c in range(n_hp // per):
        q = jnp.dot(xn, wq_ref[:, c * PROJ_COLS:(c + 1) * PROJ_COLS], preferred_element_type=F32)
        for u in range(per):
            hp = c * per + u
            qs = _mxu(q[:, u * PEER_HALF:(u + 1) * PEER_HALF])
            sc = lax.dot_general(keys_ref[hp], qs, NT_DIMS, preferred_element_type=F32)
            for grp in range(sc_ref.shape[1]):
                sc_ref[hp, grp] = sc[:, grp * V7X_LANES:(grp + 1) * V7X_LANES]


def peer_scores(h, g, wq, layer, keys):
    t, d = h.shape
    tm = PEER_Q_TM
    n_hp = keys.shape[0]
    grp = tm // V7X_LANES
    return pl.pallas_call(
        _peer_scores_kernel,
        out_shape=(jax.ShapeDtypeStruct((t, d), MXU_DTYPE),
                   jax.ShapeDtypeStruct((n_hp, t // V7X_LANES, PEER_NKEYS, V7X_LANES), F32)),
        grid=(t // tm,),
        in_specs=[
            pl.BlockSpec((tm, d), lambda i: (i, 0)),
            pl.BlockSpec((1, d), lambda i: (0, 0)),
            _layer_spec(wq, layer),
            pl.BlockSpec(keys.shape, lambda i: (0, 0, 0)),
        ],
        out_specs=(pl.BlockSpec((tm, d), lambda i: (i, 0)),
                   pl.BlockSpec((n_hp, grp, PEER_NKEYS, V7X_LANES), lambda i: (0, i, 0, 0))),
        compiler_params=_params(1),
        name="peer_scores",
    )(h, g.reshape(1, d), wq, keys)


def _extract_max(x, pos):
    m = jnp.max(x, axis=-2, keepdims=True)
    sel = jnp.min(jnp.where(x == m, pos, float(2**30)), axis=-2, keepdims=True)
    return m, sel, jnp.where(pos == sel, NEG_INF, x)


def _topk_rows(x, pos, k):
    vals, ids = [], []
    for _ in range(k):
        m, sel, x = _extract_max(x, pos)
        vals.append(m)
        ids.append(sel)
    return jnp.concatenate(vals, axis=-2), jnp.concatenate(ids, axis=-2)


def _lookup_rows(table, idx):
    out = jnp.zeros(idx.shape, table.dtype)
    for a in range(table.shape[0]):
        out = jnp.where(idx == a, table[a:a + 1, :], out)
    return out


def _key_positions():
    return lax.broadcasted_iota(jnp.int32, (PEER_NKEYS, V7X_LANES), 0).astype(F32)


def _pair_candidates(v1, v2):
    k = PEER_TOPK
    n = V7X_LANES
    r16 = lax.broadcasted_iota(jnp.int32, (k, n), 0)
    r8 = lax.broadcasted_iota(jnp.int32, (V7X_SUBLANES, n), 0)
    cand, pos = [v1[0:1, :] + v2], [r16]
    for a in range(1, V7X_SUBLANES):
        piece = v1[a:a + 1, :] + v2[0:V7X_SUBLANES, :]
        cand.append(jnp.where(r8 < k // (a + 1), piece, NEG_INF))
        pos.append(a * k + r8)
    cand.append(v1[V7X_SUBLANES:, :] + v2[0:1, :])
    pos.append((V7X_SUBLANES + r8) * k)
    return jnp.concatenate(cand, axis=0), jnp.concatenate(pos, axis=0).astype(F32)


def _emit_lists(top_s, top_pos, n1, n2, h, grp, l1_ref, l2_ref, lg_ref):
    k = PEER_TOPK
    top_pos = top_pos.astype(jnp.int32)
    e1 = _lookup_rows(n1, lax.shift_right_logical(top_pos, k.bit_length() - 1))
    e2 = _lookup_rows(n2, top_pos & (k - 1))
    ex = jnp.exp(top_s - top_s[0:1, :])
    gate = ex / jnp.sum(ex, axis=0, keepdims=True)
    rows = pl.ds(pl.multiple_of(h * k, k), k)
    l1_ref[grp, rows, :] = e1.astype(jnp.int32)
    l2_ref[grp, rows, :] = e2.astype(jnp.int32)
    lg_ref[grp, rows, :] = gate


def _transpose_lists(src_refs, dst_refs):
    for src, dst in zip(src_refs, dst_refs):
        for grp in range(src.shape[0]):
            dst[grp * V7X_LANES:(grp + 1) * V7X_LANES, :] = src[grp].T


def _peer_topk_kernel(sc_ref, i1_ref, i2_ref, g_ref, l1_ref, l2_ref, lg_ref):
    n_heads = sc_ref.shape[0] // 2
    n_units = n_heads * sc_ref.shape[1]

    def unit_pair(n, carry):
        units = [(u % n_heads, u // n_heads) for u in (n, n + n_units // 2)]
        x = jnp.stack([sc_ref[2 * h + p, grp] for h, grp in units for p in range(2)])
        hv, hn = _topk_rows(x, _key_positions(), PEER_TOPK)
        cands = [_pair_candidates(hv[2 * k], hv[2 * k + 1]) for k in range(2)]
        top_s, top_pos = _topk_rows(jnp.stack([cd[0] for cd in cands]), jnp.stack([cd[1] for cd in cands]), PEER_TOPK)
        for k, (h, grp) in enumerate(units):
            _emit_lists(top_s[k], top_pos[k], hn[2 * k], hn[2 * k + 1], h, grp, l1_ref, l2_ref, lg_ref)
        return carry

    lax.fori_loop(0, n_units // 2, unit_pair, 0)
    _transpose_lists((l1_ref, l2_ref, lg_ref), (i1_ref, i2_ref, g_ref))


def _topk_scratch(n_groups, nj, slots=()):
    lists = (n_groups, nj, V7X_LANES)
    halves = (*slots, 2, PEER_TOPK, V7X_LANES)
    return [pltpu.VMEM(lists, jnp.int32), pltpu.VMEM(lists, jnp.int32), pltpu.VMEM(lists, F32),
            pltpu.VMEM(halves, F32), pltpu.VMEM(halves, F32)]


def peer_topk(sc, n_tokens):
    n_hp, _, nk, _ = sc.shape
    grp = n_tokens // V7X_LANES
    nj = (n_hp // 2) * PEER_TOPK
    out_spec = pl.BlockSpec((n_tokens, nj), lambda i: (0, 0))
    return pl.pallas_call(
        _peer_topk_kernel,
        out_shape=(jax.ShapeDtypeStruct((n_tokens, nj), jnp.int32), jax.ShapeDtypeStruct((n_tokens, nj), jnp.int32),
                   jax.ShapeDtypeStruct((n_tokens, nj), F32)),
        grid=(1,),
        in_specs=[pl.BlockSpec((n_hp, grp, nk, V7X_LANES), lambda i: (0, 0, 0, 0))],
        out_specs=(out_spec, out_spec, out_spec),
        scratch_shapes=_topk_scratch(grp, nj)[:3],
        compiler_params=_params(1),
        name="peer_topk",
    )(sc)


def _peer_mix_kernel(h_ref, xn_ref, f1_ref, f2_ref, fg_ref, sc_ref, ua_ref, ub_ref, va_ref, vb_ref, gain_ref, o_ref,
                     w_ref, ha_ref, hb_ref, acc_ref, i1_ref, i2_ref, g_ref, l1_ref, l2_ref, lg_ref, hv_ref, hn_ref,
                     *, n_chunks, units_per_step, normalise):
    tile = pl.program_id(0)
    c = pl.program_id(1)
    last = pl.num_programs(1) - 1
    tm = xn_ref.shape[0]
    nk = PEER_NKEYS
    half = nk // 2
    next_lists = (l1_ref, l2_ref, lg_ref)
    lists = (i1_ref, i2_ref, g_ref)

    @pl.when((c == 0) & (tile == 0))
    def _():
        for src, dst in zip((f1_ref, f2_ref, fg_ref), lists):
            dst[...] = src[...]
        hv_ref[...] = jnp.zeros_like(hv_ref)
        hn_ref[...] = jnp.zeros_like(hn_ref)

    @pl.when((c == 0) & (tile > 0))
    def _():
        _transpose_lists(next_lists, lists)

    @pl.when(c == 0)
    def _():
        acc_ref[...] = jnp.zeros_like(acc_ref)
        hb_ref[...] = jnp.zeros_like(hb_ref)
        r = lax.broadcasted_iota(jnp.int32, (nk, nk), 0).astype(F32).astype(jnp.bfloat16)
        one = jnp.ones((nk, nk), jnp.bfloat16)
        zero = jnp.zeros((nk, nk), jnp.bfloat16)

        def one_hots(t):
            i1 = i1_ref[pl.ds(t, 1), :].astype(F32).astype(jnp.bfloat16)
            i2 = i2_ref[pl.ds(t, 1), :].astype(F32).astype(jnp.bfloat16)
            g = (0.5 * g_ref[pl.ds(t, 1), :]).astype(jnp.bfloat16)
            return jnp.where(r == i1, g, zero), jnp.where(r == i2, one, zero)

        def tok_pair(n, carry):
            (a0, b0), (a1, b1) = one_hots(2 * n), one_hots(2 * n + 1)
            w = lax.dot_general(jnp.concatenate([a0, a1], axis=0), jnp.concatenate([b0, b1], axis=0), NT_DIMS,
                                preferred_element_type=F32)
            for k in range(2):
                wk = w[k * nk:(k + 1) * nk, k * nk:(k + 1) * nk].astype(jnp.bfloat16)
                dst = pl.ds(pl.multiple_of((2 * n + k) * W_PITCH, V7X_SUBLANES), half)
                w_ref[dst, :] = pltpu.bitcast(wk, jnp.uint32)
            return carry

        lax.fori_loop(0, tm // 2, tok_pair, 0, unroll=SCATTER_UNROLL // 2)

    def hidden(u_ref):
        return jnp.dot(xn_ref[...], u_ref[...], preferred_element_type=F32)

    def weighted(hid_ref, chunk, valid):
        pairs = PEER_ECHUNK // (2 * nk)
        first = jnp.clip(chunk, 0, n_chunks - 1) * pairs
        hid = hid_ref[...]
        act = hid * (1.0 + lax.erf(hid * (2.0 ** -0.5)))
        cols = []
        for q in range(pairs):
            packed = w_ref[pl.ds(first + q, tm, stride=W_PITCH), :]
            if valid is not None:
                packed = jnp.where(valid, packed, jnp.uint32(0))
            w_even = lax.bitcast_convert_type(packed << 16, F32)
            w_odd = lax.bitcast_convert_type(packed & jnp.uint32(0xFFFF0000), F32)
            cols += [w_even * act[:, 2 * q * nk:(2 * q + 1) * nk], w_odd * act[:, (2 * q + 1) * nk:(2 * q + 2) * nk]]
        return _mxu(jnp.concatenate(cols, axis=1))

    def accumulate(p, v_ref):
        acc_ref[...] += jnp.dot(p, v_ref[...], preferred_element_type=F32)

    n_heads = sc_ref.shape[0] // 2
    n_units = n_heads * sc_ref.shape[1]
    xs, cands, cand_poss, prev = [], [], [], []
    for s in range(units_per_step):
        n = jnp.minimum(c * units_per_step + s, n_units - 1)
        h, grp = n % n_heads, n // n_heads
        xs += [sc_ref[2 * h, grp], sc_ref[2 * h + 1, grp]]
        cand, cand_pos = _pair_candidates(hv_ref[s, 0], hv_ref[s, 1])
        cands.append(cand)
        cand_poss.append(cand_pos)
        n = jnp.clip((c - 1) * units_per_step + s, 0, n_units - 1)
        prev.append((n % n_heads, n // n_heads, hn_ref[s, 0], hn_ref[s, 1]))
    state = [[jnp.stack(xs), _key_positions(), [], []], [jnp.stack(cands), jnp.stack(cand_poss), [], []]]

    def select(n_steps):
        for _ in range(n_steps):
            for st in state:
                m, sel, st[0] = _extract_max(st[0], st[1])
                st[2].append(m)
                st[3].append(sel)

    assert sum(SELECT_SPLIT) == PEER_TOPK
    select(SELECT_SPLIT[0])
    ha_ref[...] = hidden(ua_ref)
    select(SELECT_SPLIT[1])
    p_prev = weighted(hb_ref, 2 * c - 1, None)
    select(SELECT_SPLIT[2])
    accumulate(p_prev, va_ref)
    select(SELECT_SPLIT[3])
    hb_ref[...] = hidden(ub_ref)
    select(SELECT_SPLIT[4])
    p_cur = weighted(ha_ref, 2 * c, c < last)
    select(SELECT_SPLIT[5])
    accumulate(p_cur, vb_ref)
    select(SELECT_SPLIT[6])

    (half_v, half_n), (pair_v, pair_pos) = [
        (jnp.concatenate(st[2], axis=-2), jnp.concatenate(st[3], axis=-2)) for st in state]
    for s, (h, grp, n1, n2) in enumerate(prev):
        hv_ref[s] = half_v[2 * s:2 * s + 2]
        hn_ref[s] = half_n[2 * s:2 * s + 2]
        _emit_lists(pair_v[s], pair_pos[s], n1, n2, h, grp, *next_lists)

    @pl.when(c == last)
    def _():
        out = h_ref[...] + acc_ref[...]
        o_ref[...] = _rms(out, gain_ref[...]) if normalise else out


def peer_mix(h, xn, sc, first_lists, u, v, layer, gain, normalise):
    t, d = h.shape
    tm = first_lists[0].shape[0]
    nj = first_lists[0].shape[1]
    n_tiles = t // tm
    grp = tm // V7X_LANES
    n_hp = sc.shape[0]
    n_chunks = v.shape[1] // PEER_ECHUNK
    steps = n_chunks // 2 + 1
    n_units = (n_hp // 2) * grp
    units_per_step = -(-n_units // (steps - 1))
    tok_spec = pl.BlockSpec((tm, d), lambda i, c: (i, 0))
    res_spec = pl.BlockSpec((tm, d), lambda i, c: (jnp.where(c >= 1, i, jnp.maximum(i - 1, 0)), 0))
    first_spec = pl.BlockSpec((tm, nj), lambda i, c: (0, 0))
    sc_spec = pl.BlockSpec((n_hp, grp, PEER_NKEYS, V7X_LANES), lambda i, c: (0, jnp.minimum(i + 1, n_tiles - 1), 0, 0))

    def chunk(c, offset):
        return jnp.clip(2 * c + offset, 0, n_chunks - 1)

    def u_spec(offset):
        return pl.BlockSpec((None, None, d, PEER_ECHUNK), lambda i, c: (layer, chunk(c, offset), 0, 0))

    def v_spec(offset):
        return pl.BlockSpec((None, PEER_ECHUNK, d), lambda i, c: (layer, chunk(c, offset), 0))

    return pl.pallas_call(
        functools.partial(_peer_mix_kernel, n_chunks=n_chunks, units_per_step=units_per_step, normalise=normalise),
        out_shape=jax.ShapeDtypeStruct((t, d), F32),
        grid=(n_tiles, steps),
        in_specs=[res_spec, tok_spec, first_spec, first_spec, first_spec, sc_spec, u_spec(0), u_spec(1),
                  v_spec(-1), v_spec(0), pl.BlockSpec((1, d), lambda i, c: (0, 0))],
        out_specs=tok_spec,
        scratch_shapes=[
            pltpu.VMEM((tm * W_PITCH, PEER_NKEYS), jnp.uint32),
            pltpu.VMEM((tm, PEER_ECHUNK), F32),
            pltpu.VMEM((tm, PEER_ECHUNK), F32),
            pltpu.VMEM((tm, d), F32),
            pltpu.VMEM((tm, nj), jnp.int32),
            pltpu.VMEM((tm, nj), jnp.int32),
            pltpu.VMEM((tm, nj), F32),
            *_topk_scratch(grp, nj, slots=(units_per_step,)),
        ],
        compiler_params=_params(2),
        name="peer_mix",
    )(h, xn, *first_lists, sc, u, u, v, v, gain.reshape(1, d))


def _rope_tables(seq):
    half = HEAD_DIM // 2
    inv = ROPE_THETA ** (-jnp.arange(half, dtype=F32) / half)
    ang = jnp.arange(seq, dtype=F32)[:, None] * inv[None, :]
    cos, sin = jnp.cos(ang), jnp.sin(ang)
    return jnp.concatenate([cos, cos], axis=1), jnp.concatenate([-sin, sin], axis=1)


def kernel(x, norm_mix, norm_ffn, even_w_in, even_w_out, odd_w_in, odd_conv, odd_w_out, peer_w_q, peer_sub_keys,
           peer_u, peer_v, final_norm):
    batch, seq, d = x.shape
    depth = norm_mix.shape[0]
    cos_t, sin_t = _rope_tables(seq)
    ret_tables = _retention_tables()
    mh, rh = MOBA_HEADS, RET_HEADS
    rope_blocks = tuple(range(0, 2 * mh)) + tuple(range(3 * mh, 3 * mh + 2 * rh))
    ret_k_blocks = tuple(range(3 * mh + rh, 3 * mh + 2 * rh))

    even_in, even_out_w = cast_weights(even_w_in), cast_weights(even_w_out)
    odd_in, odd_out_w = cast_weights(odd_w_in), cast_weights(odd_w_out)
    w_q, u_t, v = cast_weights(peer_w_q), cast_weights(peer_u, slab=PEER_ECHUNK), cast_weights(peer_v)
    keys = _mxu(peer_sub_keys.reshape(depth, 2 * PEER_HEADS, PEER_NKEYS, PEER_HALF))

    h = x.reshape(batch * seq, d)
    for layer in range(depth):
        i = layer // 2
        if layer % 2 == 0:
            proj = norm_proj(h, norm_mix[layer], even_in, i, cos_t, sin_t, seq, rope_blocks, ret_k_blocks,
                             HEAD_DIM ** -0.5)
            mo_lo, mo_hi = moba(proj, batch, seq)
            ro = retention(proj, ret_tables, batch, seq)
            h = even_out(h, mo_lo, mo_hi, ro, even_out_w, i, seq)
        else:
            proj = norm_proj(h, norm_mix[layer], odd_in, i, cos_t, sin_t, seq)
            h = conv_out(h, proj, odd_conv[i], odd_out_w, i, seq)
        xn, sc = peer_scores(h, norm_ffn[layer], w_q, layer, keys[layer])
        first_lists = peer_topk(sc, min(PEER_TM, h.shape[0]))
        h = peer_mix(h, xn, sc, first_lists, u_t, v, layer, final_norm, layer == depth - 1)
    return h.reshape(batch, seq, d)
```

```python
import functools

import jax
import jax.numpy as jnp
import numpy as np
from jax import lax
from jax.experimental import pallas as pl
from jax.experimental.pallas import tpu as pltpu

HEAD_DIM = 128
MOBA_HEADS = 4
RET_HEADS = 4
MOBA_BLOCK = 256
MOBA_TOPK = 3
RET_CHUNK = 128
PEER_HEADS = 8
PEER_NKEYS = 128
PEER_TOPK = 16
PEER_HALF = 128
ROPE_THETA = 10000.0
EPS = 1e-6

V7X_LANES = 128
V7X_SUBLANES = 8
V7X_VMEM_BYTES = 64 * 2**20
VMEM_LIMIT = V7X_VMEM_BYTES - 8 * 2**20

MXU_DTYPE = jnp.bfloat16
F32 = jnp.float32
NEG_INF = float("-inf")
NT_DIMS = (((1,), (1,)), ((), ()))
TN_DIMS = (((0,), (0,)), ((), ()))

CAST_ROWS = 2048
PROJ_TM = 512
PROJ_COLS = 512
CONV_TM = 512
PEER_Q_TM = 1024
PEER_TM = 512
PEER_ECHUNK = 2 * PEER_NKEYS
SCATTER_UNROLL = 256
SELECT_SPLIT = (2, 2, 2, 2, 2, 2, 4)
W_PITCH = 72


def _params(n_axes):
    return pltpu.CompilerParams(dimension_semantics=("arbitrary",) * n_axes, vmem_limit_bytes=VMEM_LIMIT)


def _mxu(x):
    return x.astype(MXU_DTYPE)


def _rms(x, g):
    return x * lax.rsqrt(jnp.mean(x * x, axis=-1, keepdims=True) + EPS) * g


def _layer_spec(stack, layer):
    return pl.BlockSpec((None, *stack.shape[1:]), lambda *_: (layer, 0, 0))


def _cast_kernel(w_ref, o_ref, *, slab):
    if slab is None:
        o_ref[...] = _mxu(w_ref[...])
    else:
        for q in range(o_ref.shape[0]):
            o_ref[q] = _mxu(w_ref[q * slab:(q + 1) * slab, :].T)


def cast_weights(w, slab=None):
    n_layers, rows, cols = w.shape
    tr = min(rows, CAST_ROWS)
    if slab is None:
        out_shape, out_block, out_map = w.shape, (None, tr, cols), lambda l, r: (l, r, 0)
    else:
        out_shape = (n_layers, rows // slab, cols, slab)
        out_block, out_map = (None, tr // slab, cols, slab), lambda l, r: (l, r, 0, 0)
    return pl.pallas_call(
        functools.partial(_cast_kernel, slab=slab),
        out_shape=jax.ShapeDtypeStruct(out_shape, MXU_DTYPE),
        grid=(n_layers, rows // tr),
        in_specs=[pl.BlockSpec((None, tr, cols), lambda l, r: (l, r, 0))],
        out_specs=pl.BlockSpec(out_block, out_map),
        compiler_params=_params(2),
        name="cast_weights",
    )(w)


def _norm_proj_kernel(h_ref, g_ref, w_ref, cos_ref, sin_ref, o_ref, *, rope_blocks, scale_blocks, scale):
    xn = _mxu(_rms(h_ref[...], g_ref[...]))
    n = o_ref.shape[1]
    for c0 in range(0, n, PROJ_COLS):
        y = jnp.dot(xn, w_ref[:, c0:c0 + PROJ_COLS], preferred_element_type=F32)
        for u in range(PROJ_COLS // HEAD_DIM):
            blk = (c0 // HEAD_DIM) + u
            t = y[:, u * HEAD_DIM:(u + 1) * HEAD_DIM]
            if blk in rope_blocks:
                t = t * cos_ref[...] + pltpu.roll(t, HEAD_DIM // 2, axis=1) * sin_ref[...]
            if blk in scale_blocks:
                t = t * scale
            o_ref[:, blk * HEAD_DIM:(blk + 1) * HEAD_DIM] = t


def norm_proj(h, g, w, layer, cos_t, sin_t, seq, rope_blocks=(), scale_blocks=(), scale=1.0):
    t, d = h.shape
    n = w.shape[2]
    tm = PROJ_TM
    per_seq = seq // tm
    return pl.pallas_call(
        functools.partial(_norm_proj_kernel, rope_blocks=frozenset(rope_blocks), scale_blocks=frozenset(scale_blocks),
                          scale=scale),
        out_shape=jax.ShapeDtypeStruct((t, n), F32),
        grid=(t // tm,),
        in_specs=[
            pl.BlockSpec((tm, d), lambda i: (i, 0)),
            pl.BlockSpec((1, d), lambda i: (0, 0)),
            _layer_spec(w, layer),
            pl.BlockSpec((tm, HEAD_DIM), lambda i: (i % per_seq, 0)),
            pl.BlockSpec((tm, HEAD_DIM), lambda i: (i % per_seq, 0)),
        ],
        out_specs=pl.BlockSpec((tm, n), lambda i: (i, 0)),
        compiler_params=_params(1),
        name="norm_proj",
    )(h, g.reshape(1, d), w, cos_t, sin_t)


def _moba_kernel(qa_ref, qb_ref, k_ref, v_ref, oa_ref, ob_ref, kb_ref, vt_ref, km_ref, q2_ref, sel_ref, s_ref,
                 acc_ref, l_ref, *, nb, nbp):
    i = pl.program_id(2)
    blk = MOBA_BLOCK
    own = (i, nb - 1 - i)

    @pl.when(i == 0)
    def _():
        km_ref[...] = jnp.zeros_like(km_ref)
        for j in range(nb):
            rows = slice(j * blk, (j + 1) * blk)
            kb_ref[j] = _mxu(k_ref[rows, :])
            vt_ref[j] = _mxu(v_ref[rows, :].T)
            km_ref[j:j + 1, :] = jnp.sum(k_ref[rows, :], axis=0, keepdims=True) / float(blk)

    sub = lax.broadcasted_iota(jnp.int32, (nbp, blk), 0)
    kpos = lax.broadcasted_iota(jnp.int32, (blk, blk), 0)
    qpos = lax.broadcasted_iota(jnp.int32, (blk, blk), 1)

    def scores_t(j, w):
        return lax.dot_general(kb_ref[j], q2_ref[w], NT_DIMS, preferred_element_type=F32)

    m = []
    for w, q_ref in enumerate((qa_ref, qb_ref)):
        q = q_ref[...]
        q2_ref[w] = _mxu(q)
        gate = lax.dot_general(km_ref[...], q, NT_DIMS, precision=lax.Precision.HIGHEST, preferred_element_type=F32)
        rank = jnp.zeros((nbp, blk), jnp.int32)
        for jp in range(nb):
            other = gate[jp:jp + 1, :]
            beats = jnp.where(other > gate, 1, jnp.where(other == gate, jnp.where(jp < sub, 1, 0), 0))
            rank = rank + jnp.where(jp < own[w], beats, 0)
        sel_ref[w] = jnp.where(sub < own[w], jnp.where(rank < MOBA_TOPK, 1.0, 0.0), 0.0)
        s = jnp.where(kpos <= qpos, scores_t(own[w], w), NEG_INF)
        s_ref[nb - 1 + w] = s
        m.append(jnp.max(s, axis=0, keepdims=True))

    def past_pair(t):
        is_b = t >= i
        return is_b, jnp.where(is_b, 1, 0), jnp.where(is_b, t - i, t)

    for t in range(nb - 1):
        is_b, w, j = past_pair(t)
        s = jnp.where(sel_ref[w, pl.ds(j, 1), :] > 0.0, scores_t(j, w), NEG_INF)
        s_ref[t] = s
        mt = jnp.max(s, axis=0, keepdims=True)
        m = [jnp.where(is_b, m[0], jnp.maximum(m[0], mt)), jnp.where(is_b, jnp.maximum(m[1], mt), m[1])]

    c_exp = (HEAD_DIM ** -0.5) * float(np.log2(np.e))

    def weigh(slot, j, mq):
        p = jnp.exp2((s_ref[slot + jnp.minimum(i, 0)] - mq) * c_exp)
        return jnp.sum(p, axis=0, keepdims=True), jnp.dot(vt_ref[j], _mxu(p), preferred_element_type=F32)

    for w in range(2):
        l_ref[w], acc_ref[w] = weigh(nb - 1 + w, own[w], m[w])
    for t in range(nb - 1):
        is_b, w, j = past_pair(t)
        lt, at = weigh(t, j, jnp.where(is_b, m[1], m[0]))
        l_ref[w] += lt
        acc_ref[w] += at
    oa_ref[...] = (acc_ref[0] / l_ref[0]).T
    ob_ref[...] = (acc_ref[1] / l_ref[1]).T


def moba(proj, batch, seq):
    t = proj.shape[0]
    nb = seq // MOBA_BLOCK
    nbp = max(nb, V7X_SUBLANES)
    hd = HEAD_DIM
    assert nb % 2 == 0
    nh = nb // 2
    out = jax.ShapeDtypeStruct((t // 2, MOBA_HEADS * hd), F32)
    out_spec = pl.BlockSpec((MOBA_BLOCK, hd), lambda b, h, i: (b * nh + i, h))
    return pl.pallas_call(
        functools.partial(_moba_kernel, nb=nb, nbp=nbp),
        out_shape=(out, out),
        grid=(batch, MOBA_HEADS, nh),
        in_specs=[
            pl.BlockSpec((MOBA_BLOCK, hd), lambda b, h, i: (b * nb + i, h)),
            pl.BlockSpec((MOBA_BLOCK, hd), lambda b, h, i: (b * nb + nb - 1 - i, h)),
            pl.BlockSpec((seq, hd), lambda b, h, i: (b, MOBA_HEADS + h)),
            pl.BlockSpec((seq, hd), lambda b, h, i: (b, 2 * MOBA_HEADS + h)),
        ],
        out_specs=(out_spec, out_spec),
        scratch_shapes=[
            pltpu.VMEM((nb, MOBA_BLOCK, hd), MXU_DTYPE),
            pltpu.VMEM((nb, hd, MOBA_BLOCK), MXU_DTYPE),
            pltpu.VMEM((nbp, hd), F32),
            pltpu.VMEM((2, MOBA_BLOCK, hd), MXU_DTYPE),
            pltpu.VMEM((2, nbp, MOBA_BLOCK), F32),
            pltpu.VMEM((nb + 1, MOBA_BLOCK, MOBA_BLOCK), F32),
            pltpu.VMEM((2, hd, MOBA_BLOCK), F32),
            pltpu.VMEM((2, 1, MOBA_BLOCK), F32),
        ],
        compiler_params=_params(3),
        name="moba",
    )(proj, proj, proj, proj)


def _retention_kernel(q_ref, k_ref, v_ref, g_ref, dm_ref, xi_ref, ze_ref, gc_ref, o_ref, st_ref):
    @pl.when(pl.program_id(0) == 0)
    def _():
        st_ref[...] = jnp.zeros_like(st_ref)

    for b in range(q_ref.shape[0]):
        for h in range(RET_HEADS):
            sl = slice(h * HEAD_DIM, (h + 1) * HEAD_DIM)
            q = _mxu(q_ref[b, :, sl])
            k = k_ref[b, :, sl]
            v = _mxu(v_ref[b, :, sl])
            st = st_ref[b, h]
            inner = lax.dot_general(q, _mxu(k), NT_DIMS, preferred_element_type=F32) * dm_ref[h]
            o = (jnp.dot(_mxu(inner), v, preferred_element_type=F32)
                 + jnp.dot(q, _mxu(st), preferred_element_type=F32) * xi_ref[h])
            st_ref[b, h] = (st * gc_ref[h]
                            + lax.dot_general(_mxu(k * ze_ref[h]), v, TN_DIMS, preferred_element_type=F32))
            o = o * lax.rsqrt(jnp.mean(o * o, axis=-1, keepdims=True) + EPS)
            gate = g_ref[b, :, sl]
            o_ref[b, :, sl] = o * (gate * jax.nn.sigmoid(gate))


def retention(proj, tables, batch, seq):
    t, n = proj.shape
    nc = seq // RET_CHUNK
    w = RET_HEADS * HEAD_DIM
    base = 3 * MOBA_HEADS * HEAD_DIM // w
    tab_spec = pl.BlockSpec((RET_HEADS, RET_CHUNK, HEAD_DIM), lambda c: (0, 0, 0))
    proj3 = proj.reshape(batch, seq, n)

    def col_spec(block):
        return pl.BlockSpec((batch, RET_CHUNK, w), lambda c: (0, c, block))

    out = pl.pallas_call(
        _retention_kernel,
        out_shape=jax.ShapeDtypeStruct((batch, seq, w), F32),
        grid=(nc,),
        in_specs=[col_spec(base), col_spec(base + 1), col_spec(base + 2), col_spec(base + 3),
                  tab_spec, tab_spec, tab_spec, tab_spec],
        out_specs=col_spec(0),
        scratch_shapes=[pltpu.VMEM((batch, RET_HEADS, HEAD_DIM, HEAD_DIM), F32)],
        compiler_params=_params(1),
        name="retention",
    )(proj3, proj3, proj3, proj3, *tables)
    return out.reshape(t, w)


def _retention_tables():
    c = RET_CHUNK
    log_g = jnp.log(1.0 - 2.0 ** (-5.0 - jnp.arange(RET_HEADS, dtype=F32)))
    idx = jnp.arange(c, dtype=F32)
    diff = idx[:, None] - idx[None, :]
    dmask = jnp.where(diff >= 0, jnp.exp(log_g[:, None, None] * jnp.maximum(diff, 0.0)), 0.0)
    xi = jnp.exp(log_g[:, None] * (idx + 1.0))[..., None]
    zeta = jnp.exp(log_g[:, None] * (c - 1.0 - idx))[..., None]
    gc = jnp.exp(log_g * c)[:, None, None]
    full = (RET_HEADS, c, HEAD_DIM)
    return (dmask, jnp.broadcast_to(xi, full), jnp.broadcast_to(zeta, full), jnp.broadcast_to(gc, full))


def _even_out_kernel(h_ref, lo_ref, hi_ref, r_ref, w_ref, o_ref, *, nb):
    in_lo = (pl.program_id(0) % nb) < nb // 2
    mo = jnp.where(in_lo, lo_ref[...], hi_ref[...])
    y = jnp.concatenate([_mxu(mo), _mxu(r_ref[...])], axis=1)
    o_ref[...] = h_ref[...] + jnp.dot(y, w_ref[...], preferred_element_type=F32)


def even_out(h, mo_lo, mo_hi, ro, w, layer, seq):
    t, d = h.shape
    tm = MOBA_BLOCK
    nb = seq // tm
    nh = nb // 2

    def lo_map(g):
        return ((g // nb) * nh + jnp.minimum(g % nb, nh - 1), 0)

    def hi_map(g):
        return ((g // nb) * nh + (nb - 1 - jnp.maximum(g % nb, nh)), 0)

    return pl.pallas_call(
        functools.partial(_even_out_kernel, nb=nb),
        out_shape=jax.ShapeDtypeStruct((t, d), F32),
        grid=(t // tm,),
        in_specs=[
            pl.BlockSpec((tm, d), lambda g: (g, 0)),
            pl.BlockSpec((tm, mo_lo.shape[1]), lo_map),
            pl.BlockSpec((tm, mo_hi.shape[1]), hi_map),
            pl.BlockSpec((tm, ro.shape[1]), lambda g: (g, 0)),
            _layer_spec(w, layer),
        ],
        out_specs=pl.BlockSpec((tm, d), lambda g: (g, 0)),
        compiler_params=_params(1),
        name="even_out",
    )(h, mo_lo, mo_hi, ro, w)


def _conv_out_kernel(h_ref, bg_ref, cg_ref, hx_ref, cgp_ref, hxp_ref, cw_ref, w_ref, o_ref, *, per_seq):
    i = pl.program_id(0)
    u = cg_ref[...] * hx_ref[...]
    up = jnp.where(i % per_seq == 0, 0.0, cgp_ref[...] * hxp_ref[...])
    row = lax.broadcasted_iota(jnp.int32, u.shape, 0)
    p1 = up[V7X_SUBLANES - 1:V7X_SUBLANES, :]
    p2 = up[V7X_SUBLANES - 2:V7X_SUBLANES - 1, :]
    u1 = jnp.where(row == 0, p1, pltpu.roll(u, 1, axis=0))
    u2 = jnp.where(row == 0, p2, jnp.where(row == 1, p1, pltpu.roll(u, 2, axis=0)))
    y = cw_ref[0:1, :] * u2 + cw_ref[1:2, :] * u1 + cw_ref[2:3, :] * u
    z = _mxu(bg_ref[...] * y)
    o_ref[...] = h_ref[...] + jnp.dot(z, w_ref[...], preferred_element_type=F32)


def conv_out(h, proj, conv_w, w, layer, seq):
    t, d = h.shape
    tm = CONV_TM
    per_seq = seq // tm
    halo = tm // V7X_SUBLANES
    return pl.pallas_call(
        functools.partial(_conv_out_kernel, per_seq=per_seq),
        out_shape=jax.ShapeDtypeStruct((t, d), F32),
        grid=(t // tm,),
        in_specs=[
            pl.BlockSpec((tm, d), lambda i: (i, 0)),
            pl.BlockSpec((tm, d), lambda i: (i, 0)),
            pl.BlockSpec((tm, d), lambda i: (i, 1)),
            pl.BlockSpec((tm, d), lambda i: (i, 2)),
            pl.BlockSpec((V7X_SUBLANES, d), lambda i: (jnp.maximum(i * halo - 1, 0), 1)),
            pl.BlockSpec((V7X_SUBLANES, d), lambda i: (jnp.maximum(i * halo - 1, 0), 2)),
            pl.BlockSpec(conv_w.shape, lambda i: (0, 0)),
            _layer_spec(w, layer),
        ],
        out_specs=pl.BlockSpec((tm, d), lambda i: (i, 0)),
        compiler_params=_params(1),
        name="conv_out",
    )(h, proj, proj, proj, proj, proj, conv_w, w)


def _peer_scores_kernel(h_ref, g_ref, wq_ref, keys_ref, xn_ref, sc_ref):
    xn = _mxu(_rms(h_ref[...], g_ref[...]))
    xn_ref[...] = xn
    n_hp = keys_ref.shape[0]
    per = PROJ_COLS // PEER_HALF
    for c in range(n_hp // per):
        q = jnp.dot(xn, wq_ref[:, c * PROJ_COLS:(c + 1) * PROJ_COLS], preferred_element_type=F32)
        for u in range(per):
            hp = c * per + u
            qs = _mxu(q[:, u * PEER_HALF:(u + 1) * PEER_HALF])
            sc = lax.dot_general(keys_ref[hp], qs, NT_DIMS, preferred_element_type=F32)
            for grp in range(sc_ref.shape[1]):
                sc_ref[hp, grp] = sc[:, grp * V7X_LANES:(grp + 1) * V7X_LANES]


def peer_scores(h, g, wq, layer, keys):
    t, d = h.shape
    tm = PEER_Q_TM
    n_hp = keys.shape[0]
    grp = tm // V7X_LANES
    return pl.pallas_call(
        _peer_scores_kernel,
        out_shape=(jax.ShapeDtypeStruct((t, d), MXU_DTYPE),
                   jax.ShapeDtypeStruct((n_hp, t // V7X_LANES, PEER_NKEYS, V7X_LANES), F32)),
        grid=(t // tm,),
        in_specs=[
            pl.BlockSpec((tm, d), lambda i: (i, 0)),
            pl.BlockSpec((1, d), lambda i: (0, 0)),
            _layer_spec(wq, layer),
            pl.BlockSpec(keys.shape, lambda i: (0, 0, 0)),
        ],
        out_specs=(pl.BlockSpec((tm, d), lambda i: (i, 0)),
                   pl.BlockSpec((n_hp, grp, PEER_NKEYS, V7X_LANES), lambda i: (0, i, 0, 0))),
        compiler_params=_params(1),
        name="peer_scores",
    )(h, g.reshape(1, d), wq, keys)


def _extract_max(x, pos):
    m = jnp.max(x, axis=-2, keepdims=True)
    sel = jnp.min(jnp.where(x == m, pos, float(2**30)), axis=-2, keepdims=True)
    return m, sel, jnp.where(pos == sel, NEG_INF, x)


def _topk_rows(x, pos, k):
    vals, ids = [], []
    for _ in range(k):
        m, sel, x = _extract_max(x, pos)
        vals.append(m)
        ids.append(sel)
    return jnp.concatenate(vals, axis=-2), jnp.concatenate(ids, axis=-2)


def _lookup_rows(table, idx):
    out = jnp.zeros(idx.shape, table.dtype)
    for a in range(table.shape[0]):
        out = jnp.where(idx == a, table[a:a + 1, :], out)
    return out


def _key_positions():
    return lax.broadcasted_iota(jnp.int32, (PEER_NKEYS, V7X_LANES), 0).astype(F32)


def _pair_candidates(v1, v2):
    k = PEER_TOPK
    n = V7X_LANES
    r16 = lax.broadcasted_iota(jnp.int32, (k, n), 0)
    r8 = lax.broadcasted_iota(jnp.int32, (V7X_SUBLANES, n), 0)
    cand, pos = [v1[0:1, :] + v2], [r16]
    for a in range(1, V7X_SUBLANES):
        piece = v1[a:a + 1, :] + v2[0:V7X_SUBLANES, :]
        cand.append(jnp.where(r8 < k // (a + 1), piece, NEG_INF))
        pos.append(a * k + r8)
    cand.append(v1[V7X_SUBLANES:, :] + v2[0:1, :])
    pos.append((V7X_SUBLANES + r8) * k)
    return jnp.concatenate(cand, axis=0), jnp.concatenate(pos, axis=0).astype(F32)


def _emit_lists(top_s, top_pos, n1, n2, h, grp, l1_ref, l2_ref, lg_ref):
    k = PEER_TOPK
    top_pos = top_pos.astype(jnp.int32)
    e1 = _lookup_rows(n1, lax.shift_right_logical(top_pos, k.bit_length() - 1))
    e2 = _lookup_rows(n2, top_pos & (k - 1))
    ex = jnp.exp(top_s - top_s[0:1, :])
    gate = ex / jnp.sum(ex, axis=0, keepdims=True)
    rows = pl.ds(pl.multiple_of(h * k, k), k)
    l1_ref[grp, rows, :] = e1.astype(jnp.int32)
    l2_ref[grp, rows, :] = e2.astype(jnp.int32)
    lg_ref[grp, rows, :] = gate


def _transpose_lists(src_refs, dst_refs):
    for src, dst in zip(src_refs, dst_refs):
        for grp in range(src.shape[0]):
            dst[grp * V7X_LANES:(grp + 1) * V7X_LANES, :] = src[grp].T


def _peer_topk_kernel(sc_ref, i1_ref, i2_ref, g_ref, l1_ref, l2_ref, lg_ref):
    n_heads = sc_ref.shape[0] // 2
    n_units = n_heads * sc_ref.shape[1]

    def unit_pair(n, carry):
        units = [(u % n_heads, u // n_heads) for u in (n, n + n_units // 2)]
        x = jnp.stack([sc_ref[2 * h + p, grp] for h, grp in units for p in range(2)])
        hv, hn = _topk_rows(x, _key_positions(), PEER_TOPK)
        cands = [_pair_candidates(hv[2 * k], hv[2 * k + 1]) for k in range(2)]
        top_s, top_pos = _topk_rows(jnp.stack([cd[0] for cd in cands]), jnp.stack([cd[1] for cd in cands]), PEER_TOPK)
        for k, (h, grp) in enumerate(units):
            _emit_lists(top_s[k], top_pos[k], hn[2 * k], hn[2 * k + 1], h, grp, l1_ref, l2_ref, lg_ref)
        return carry

    lax.fori_loop(0, n_units // 2, unit_pair, 0)
    _transpose_lists((l1_ref, l2_ref, lg_ref), (i1_ref, i2_ref, g_ref))


def _topk_scratch(n_groups, nj, slots=()):
    lists = (n_groups, nj, V7X_LANES)
    halves = (*slots, 2, PEER_TOPK, V7X_LANES)
    return [pltpu.VMEM(lists, jnp.int32), pltpu.VMEM(lists, jnp.int32), pltpu.VMEM(lists, F32),
            pltpu.VMEM(halves, F32), pltpu.VMEM(halves, F32)]


def peer_topk(sc, n_tokens):
    n_hp, _, nk, _ = sc.shape
    grp = n_tokens // V7X_LANES
    nj = (n_hp // 2) * PEER_TOPK
    out_spec = pl.BlockSpec((n_tokens, nj), lambda i: (0, 0))
    return pl.pallas_call(
        _peer_topk_kernel,
        out_shape=(jax.ShapeDtypeStruct((n_tokens, nj), jnp.int32), jax.ShapeDtypeStruct((n_tokens, nj), jnp.int32),
                   jax.ShapeDtypeStruct((n_tokens, nj), F32)),
        grid=(1,),
        in_specs=[pl.BlockSpec((n_hp, grp, nk, V7X_LANES), lambda i: (0, 0, 0, 0))],
        out_specs=(out_spec, out_spec, out_spec),
        scratch_shapes=_topk_scratch(grp, nj)[:3],
        compiler_params=_params(1),
        name="peer_topk",
    )(sc)


def _peer_mix_kernel(h_ref, xn_ref, f1_ref, f2_ref, fg_ref, sc_ref, ua_ref, ub_ref, va_ref, vb_ref, gain_ref, o_ref,
                     w_ref, ha_ref, hb_ref, acc_ref, i1_ref, i2_ref, g_ref, l1_ref, l2_ref, lg_ref, hv_ref, hn_ref,
                     *, n_chunks, units_per_step, normalise):
    tile = pl.program_id(0)
    c = pl.program_id(1)
    last = pl.num_programs(1) - 1
    tm = xn_ref.shape[0]
    nk = PEER_NKEYS
    half = nk // 2
    next_lists = (l1_ref, l2_ref, lg_ref)
    lists = (i1_ref, i2_ref, g_ref)

    @pl.when((c == 0) & (tile == 0))
    def _():
        for src, dst in zip((f1_ref, f2_ref, fg_ref), lists):
            dst[...] = src[...]
        hv_ref[...] = jnp.zeros_like(hv_ref)
        hn_ref[...] = jnp.zeros_like(hn_ref)

    @pl.when((c == 0) & (tile > 0))
    def _():
        _transpose_lists(next_lists, lists)

    @pl.when(c == 0)
    def _():
        acc_ref[...] = jnp.zeros_like(acc_ref)
        hb_ref[...] = jnp.zeros_like(hb_ref)
        r = lax.broadcasted_iota(jnp.int32, (nk, nk), 0).astype(F32).astype(jnp.bfloat16)
        one = jnp.ones((nk, nk), jnp.bfloat16)
        zero = jnp.zeros((nk, nk), jnp.bfloat16)

        def one_hots(t):
            i1 = i1_ref[pl.ds(t, 1), :].astype(F32).astype(jnp.bfloat16)
            i2 = i2_ref[pl.ds(t, 1), :].astype(F32).astype(jnp.bfloat16)
            g = (0.5 * g_ref[pl.ds(t, 1), :]).astype(jnp.bfloat16)
            return jnp.where(r == i1, g, zero), jnp.where(r == i2, one, zero)

        def tok_pair(n, carry):
            (a0, b0), (a1, b1) = one_hots(2 * n), one_hots(2 * n + 1)
            w = lax.dot_general(jnp.concatenate([a0, a1], axis=0), jnp.concatenate([b0, b1], axis=0), NT_DIMS,
                                preferred_element_type=F32)
            for k in range(2):
                wk = w[k * nk:(k + 1) * nk, k * nk:(k + 1) * nk].astype(jnp.bfloat16)
                dst = pl.ds(pl.multiple_of((2 * n + k) * W_PITCH, V7X_SUBLANES), half)
                w_ref[dst, :] = pltpu.bitcast(wk, jnp.uint32)
            return carry

        lax.fori_loop(0, tm // 2, tok_pair, 0, unroll=SCATTER_UNROLL // 2)

    def hidden(u_ref):
        return jnp.dot(xn_ref[...], u_ref[...], preferred_element_type=F32)

    def weighted(hid_ref, chunk, valid):
        pairs = PEER_ECHUNK // (2 * nk)
        first = jnp.clip(chunk, 0, n_chunks - 1) * pairs
        hid = hid_ref[...]
        act = hid * (1.0 + lax.erf(hid * (2.0 ** -0.5)))
        cols = []
        for q in range(pairs):
            packed = w_ref[pl.ds(first + q, tm, stride=W_PITCH), :]
            if valid is not None:
                packed = jnp.where(valid, packed, jnp.uint32(0))
            w_even = lax.bitcast_convert_type(packed << 16, F32)
            w_odd = lax.bitcast_convert_type(packed & jnp.uint32(0xFFFF0000), F32)
            cols += [w_even * act[:, 2 * q * nk:(2 * q + 1) * nk], w_odd * act[:, (2 * q + 1) * nk:(2 * q + 2) * nk]]
        return _mxu(jnp.concatenate(cols, axis=1))

    def accumulate(p, v_ref):
        acc_ref[...] += jnp.dot(p, v_ref[...], preferred_element_type=F32)

    n_heads = sc_ref.shape[0] // 2
    n_units = n_heads * sc_ref.shape[1]
    xs, cands, cand_poss, prev = [], [], [], []
    for s in range(units_per_step):
        n = jnp.minimum(c * units_per_step + s, n_units - 1)
        h, grp = n % n_heads, n // n_heads
        xs += [sc_ref[2 * h, grp], sc_ref[2 * h + 1, grp]]
        cand, cand_pos = _pair_candidates(hv_ref[s, 0], hv_ref[s, 1])
        cands.append(cand)
        cand_poss.append(cand_pos)
        n = jnp.clip((c - 1) * units_per_step + s, 0, n_units - 1)
        prev.append((n % n_heads, n // n_heads, hn_ref[s, 0], hn_ref[s, 1]))
    state = [[jnp.stack(xs), _key_positions(), [], []], [jnp.stack(cands), jnp.stack(cand_poss), [], []]]

    def select(n_steps):
        for _ in range(n_steps):
            for st in state:
                m, sel, st[0] = _extract_max(st[0], st[1])
                st[2].append(m)
                st[3].append(sel)

    assert sum(SELECT_SPLIT) == PEER_TOPK
    select(SELECT_SPLIT[0])
    ha_ref[...] = hidden(ua_ref)
    select(SELECT_SPLIT[1])
    p_prev = weighted(hb_ref, 2 * c - 1, None)
    select(SELECT_SPLIT[2])
    accumulate(p_prev, va_ref)
    select(SELECT_SPLIT[3])
    hb_ref[...] = hidden(ub_ref)
    select(SELECT_SPLIT[4])
    p_cur = weighted(ha_ref, 2 * c, c < last)
    select(SELECT_SPLIT[5])
    accumulate(p_cur, vb_ref)
    select(SELECT_SPLIT[6])

    (half_v, half_n), (pair_v, pair_pos) = [
        (jnp.concatenate(st[2], axis=-2), jnp.concatenate(st[3], axis=-2)) for st in state]
    for s, (h, grp, n1, n2) in enumerate(prev):
        hv_ref[s] = half_v[2 * s:2 * s + 2]
        hn_ref[s] = half_n[2 * s:2 * s + 2]
        _emit_lists(pair_v[s], pair_pos[s], n1, n2, h, grp, *next_lists)

    @pl.when(c == last)
    def _():
        out = h_ref[...] + acc_ref[...]
        o_ref[...] = _rms(out, gain_ref[...]) if normalise else out


def peer_mix(h, xn, sc, first_lists, u, v, layer, gain, normalise):
    t, d = h.shape
    tm = first_lists[0].shape[0]
    nj = first_lists[0].shape[1]
    n_tiles = t // tm
    grp = tm // V7X_LANES
    n_hp = sc.shape[0]
    n_chunks = v.shape[1] // PEER_ECHUNK
    steps = n_chunks // 2 + 1
    n_units = (n_hp // 2) * grp
    units_per_step = -(-n_units // (steps - 1))
    tok_spec = pl.BlockSpec((tm, d), lambda i, c: (i, 0))
    res_spec = pl.BlockSpec((tm, d), lambda i, c: (jnp.where(c >= 1, i, jnp.maximum(i - 1, 0)), 0))
    first_spec = pl.BlockSpec((tm, nj), lambda i, c: (0, 0))
    sc_spec = pl.BlockSpec((n_hp, grp, PEER_NKEYS, V7X_LANES), lambda i, c: (0, jnp.minimum(i + 1, n_tiles - 1), 0, 0))

    def chunk(c, offset):
        return jnp.clip(2 * c + offset, 0, n_chunks - 1)

    def u_spec(offset):
        return pl.BlockSpec((None, None, d, PEER_ECHUNK), lambda i, c: (layer, chunk(c, offset), 0, 0))

    def v_spec(offset):
        return pl.BlockSpec((None, PEER_ECHUNK, d), lambda i, c: (layer, chunk(c, offset), 0))

    return pl.pallas_call(
        functools.partial(_peer_mix_kernel, n_chunks=n_chunks, units_per_step=units_per_step, normalise=normalise),
        out_shape=jax.ShapeDtypeStruct((t, d), F32),
        grid=(n_tiles, steps),
        in_specs=[res_spec, tok_spec, first_spec, first_spec, first_spec, sc_spec, u_spec(0), u_spec(1),
                  v_spec(-1), v_spec(0), pl.BlockSpec((1, d), lambda i, c: (0, 0))],
        out_specs=tok_spec,
        scratch_shapes=[
            pltpu.VMEM((tm * W_PITCH, PEER_NKEYS), jnp.uint32),
            pltpu.VMEM((tm, PEER_ECHUNK), F32),
            pltpu.VMEM((tm, PEER_ECHUNK), F32),
            pltpu.VMEM((tm, d), F32),
            pltpu.VMEM((tm, nj), jnp.int32),
            pltpu.VMEM((tm, nj), jnp.int32),
            pltpu.VMEM((tm, nj), F32),
            *_topk_scratch(grp, nj, slots=(units_per_step,)),
        ],
        compiler_params=_params(2),
        name="peer_mix",
    )(h, xn, *first_lists, sc, u, u, v, v, gain.reshape(1, d))


def _rope_tables(seq):
    half = HEAD_DIM // 2
    inv = ROPE_THETA ** (-jnp.arange(half, dtype=F32) / half)
    ang = jnp.arange(seq, dtype=F32)[:, None] * inv[None, :]
    cos, sin = jnp.cos(ang), jnp.sin(ang)
    return jnp.concatenate([cos, cos], axis=1), jnp.concatenate([-sin, sin], axis=1)


def kernel(x, norm_mix, norm_ffn, even_w_in, even_w_out, odd_w_in, odd_conv, odd_w_out, peer_w_q, peer_sub_keys,
           peer_u, peer_v, final_norm):
    batch, seq, d = x.shape
    depth = norm_mix.shape[0]
    cos_t, sin_t = _rope_tables(seq)
    ret_tables = _retention_tables()
    mh, rh = MOBA_HEADS, RET_HEADS
    rope_blocks = tuple(range(0, 2 * mh)) + tuple(range(3 * mh, 3 * mh + 2 * rh))
    ret_k_blocks = tuple(range(3 * mh + rh, 3 * mh + 2 * rh))

    even_in, even_out_w = cast_weights(even_w_in), cast_weights(even_w_out)
    odd_in, odd_out_w = cast_weights(odd_w_in), cast_weights(odd_w_out)
    w_q, u_t, v = cast_weights(peer_w_q), cast_weights(peer_u, slab=PEER_ECHUNK), cast_weights(peer_v)
    keys = _mxu(peer_sub_keys.reshape(depth, 2 * PEER_HEADS, PEER_NKEYS, PEER_HALF))

    h = x.reshape(batch * seq, d)
    for layer in range(depth):
        i = layer // 2
        if layer % 2 == 0:
            proj = norm_proj(h, norm_mix[layer], even_in, i, cos_t, sin_t, seq, rope_blocks, ret_k_blocks,
                             HEAD_DIM ** -0.5)
            mo_lo, mo_hi = moba(proj, batch, seq)
            ro = retention(proj, ret_tables, batch, seq)
            h = even_out(h, mo_lo, mo_hi, ro, even_out_w, i, seq)
        else:
            proj = norm_proj(h, norm_mix[layer], odd_in, i, cos_t, sin_t, seq)
            h = conv_out(h, proj, odd_conv[i], odd_out_w, i, seq)
        xn, sc = peer_scores(h, norm_ffn[layer], w_q, layer, keys[layer])
        first_lists = peer_topk(sc, min(PEER_TM, h.shape[0]))
        h = peer_mix(h, xn, sc, first_lists, u_t, v, layer, final_norm, layer == depth - 1)
    return h.reshape(batch, seq, d)
```

```python
import functools

import jax
import jax.numpy as jnp
import numpy as np
from jax import lax
from jax.experimental import pallas as pl
from jax.experimental.pallas import tpu as pltpu

HEAD_DIM = 128
MOBA_HEADS = 4
RET_HEADS = 4
MOBA_BLOCK = 256
MOBA_TOPK = 3
RET_CHUNK = 128
PEER_HEADS = 8
PEER_NKEYS = 128
PEER_TOPK = 16
PEER_HALF = 128
ROPE_THETA = 10000.0
EPS = 1e-6

V7X_LANES = 128
V7X_SUBLANES = 8
V7X_VMEM_BYTES = 64 * 2**20
VMEM_LIMIT = V7X_VMEM_BYTES - 8 * 2**20

MXU_DTYPE = jnp.bfloat16
F32 = jnp.float32
NEG_INF = float("-inf")
NT_DIMS = (((1,), (1,)), ((), ()))
TN_DIMS = (((0,), (0,)), ((), ()))

CAST_ROWS = 2048
PROJ_TM = 1024
PROJ_COLS = 512
CONV_TM = 512
PEER_Q_TM = 1024
PEER_TM = 512
PEER_ECHUNK = 2 * PEER_NKEYS
SCATTER_UNROLL = 256
SELECT_SPLIT = (2, 2, 2, 2, 2, 2, 4)
W_PITCH = 72


def _params(n_axes):
    return pltpu.CompilerParams(dimension_semantics=("arbitrary",) * n_axes, vmem_limit_bytes=VMEM_LIMIT)


def _mxu(x):
    return x.astype(MXU_DTYPE)


def _rms(x, g):
    return x * lax.rsqrt(jnp.mean(x * x, axis=-1, keepdims=True) + EPS) * g


def _layer_spec(stack, layer):
    return pl.BlockSpec((None, *stack.shape[1:]), lambda *_: (layer, 0, 0))


def _cast_kernel(w_ref, o_ref, *, slab):
    if slab is None:
        o_ref[...] = _mxu(w_ref[...])
    else:
        for q in range(o_ref.shape[0]):
            o_ref[q] = _mxu(w_ref[q * slab:(q + 1) * slab, :].T)


def cast_weights(w, slab=None):
    n_layers, rows, cols = w.shape
    tr = min(rows, CAST_ROWS)
    if slab is None:
        out_shape, out_block, out_map = w.shape, (None, tr, cols), lambda l, r: (l, r, 0)
    else:
        out_shape = (n_layers, rows // slab, cols, slab)
        out_block, out_map = (None, tr // slab, cols, slab), lambda l, r: (l, r, 0, 0)
    return pl.pallas_call(
        functools.partial(_cast_kernel, slab=slab),
        out_shape=jax.ShapeDtypeStruct(out_shape, MXU_DTYPE),
        grid=(n_layers, rows // tr),
        in_specs=[pl.BlockSpec((None, tr, cols), lambda l, r: (l, r, 0))],
        out_specs=pl.BlockSpec(out_block, out_map),
        compiler_params=_params(2),
        name="cast_weights",
    )(w)


def _norm_proj_kernel(h_ref, g_ref, w_ref, cos_ref, sin_ref, o_ref, *, rope_blocks, scale_blocks, scale):
    xn = _mxu(_rms(h_ref[...], g_ref[...]))
    n = o_ref.shape[1]
    for c0 in range(0, n, PROJ_COLS):
        y = jnp.dot(xn, w_ref[:, c0:c0 + PROJ_COLS], preferred_element_type=F32)
        for u in range(PROJ_COLS // HEAD_DIM):
            blk = (c0 // HEAD_DIM) + u
            t = y[:, u * HEAD_DIM:(u + 1) * HEAD_DIM]
            if blk in rope_blocks:
                t = t * cos_ref[...] + pltpu.roll(t, HEAD_DIM // 2, axis=1) * sin_ref[...]
            if blk in scale_blocks:
                t = t * scale
            o_ref[:, blk * HEAD_DIM:(blk + 1) * HEAD_DIM] = t


def norm_proj(h, g, w, layer, cos_t, sin_t, seq, rope_blocks=(), scale_blocks=(), scale=1.0):
    t, d = h.shape
    n = w.shape[2]
    tm = PROJ_TM
    per_seq = seq // tm
    return pl.pallas_call(
        functools.partial(_norm_proj_kernel, rope_blocks=frozenset(rope_blocks), scale_blocks=frozenset(scale_blocks),
                          scale=scale),
        out_shape=jax.ShapeDtypeStruct((t, n), F32),
        grid=(t // tm,),
        in_specs=[
            pl.BlockSpec((tm, d), lambda i: (i, 0)),
            pl.BlockSpec((1, d), lambda i: (0, 0)),
            _layer_spec(w, layer),
            pl.BlockSpec((tm, HEAD_DIM), lambda i: (i % per_seq, 0)),
            pl.BlockSpec((tm, HEAD_DIM), lambda i: (i % per_seq, 0)),
        ],
        out_specs=pl.BlockSpec((tm, n), lambda i: (i, 0)),
        compiler_params=_params(1),
        name="norm_proj",
    )(h, g.reshape(1, d), w, cos_t, sin_t)


def _moba_kernel(qa_ref, qb_ref, k_ref, v_ref, oa_ref, ob_ref, kb_ref, vt_ref, km_ref, q2_ref, sel_ref, s_ref,
                 acc_ref, l_ref, *, nb, nbp):
    i = pl.program_id(2)
    blk = MOBA_BLOCK
    own = (i, nb - 1 - i)

    @pl.when(i == 0)
    def _():
        km_ref[...] = jnp.zeros_like(km_ref)
        for j in range(nb):
            rows = slice(j * blk, (j + 1) * blk)
            kb_ref[j] = _mxu(k_ref[rows, :])
            vt_ref[j] = _mxu(v_ref[rows, :].T)
            km_ref[j:j + 1, :] = jnp.sum(k_ref[rows, :], axis=0, keepdims=True) / float(blk)

    sub = lax.broadcasted_iota(jnp.int32, (nbp, blk), 0)
    kpos = lax.broadcasted_iota(jnp.int32, (blk, blk), 0)
    qpos = lax.broadcasted_iota(jnp.int32, (blk, blk), 1)

    def scores_t(j, w):
        return lax.dot_general(kb_ref[j], q2_ref[w], NT_DIMS, preferred_element_type=F32)

    m = []
    for w, q_ref in enumerate((qa_ref, qb_ref)):
        q = q_ref[...]
        q2_ref[w] = _mxu(q)
        gate = lax.dot_general(km_ref[...], q, NT_DIMS, precision=lax.Precision.HIGHEST, preferred_element_type=F32)
        rank = jnp.zeros((nbp, blk), jnp.int32)
        for jp in range(nb):
            other = gate[jp:jp + 1, :]
            beats = jnp.where(other > gate, 1, jnp.where(other == gate, jnp.where(jp < sub, 1, 0), 0))
            rank = rank + jnp.where(jp < own[w], beats, 0)
        sel_ref[w] = jnp.where(sub < own[w], jnp.where(rank < MOBA_TOPK, 1.0, 0.0), 0.0)
        s = jnp.where(kpos <= qpos, scores_t(own[w], w), NEG_INF)
        s_ref[nb - 1 + w] = s
        m.append(jnp.max(s, axis=0, keepdims=True))

    def past_pair(t):
        is_b = t >= i
        return is_b, jnp.where(is_b, 1, 0), jnp.where(is_b, t - i, t)

    for t in range(nb - 1):
        is_b, w, j = past_pair(t)
        s = jnp.where(sel_ref[w, pl.ds(j, 1), :] > 0.0, scores_t(j, w), NEG_INF)
        s_ref[t] = s
        mt = jnp.max(s, axis=0, keepdims=True)
        m = [jnp.where(is_b, m[0], jnp.maximum(m[0], mt)), jnp.where(is_b, jnp.maximum(m[1], mt), m[1])]

    c_exp = (HEAD_DIM ** -0.5) * float(np.log2(np.e))

    def weigh(slot, j, mq):
        p = jnp.exp2((s_ref[slot + jnp.minimum(i, 0)] - mq) * c_exp)
        return jnp.sum(p, axis=0, keepdims=True), jnp.dot(vt_ref[j], _mxu(p), preferred_element_type=F32)

    for w in range(2):
        l_ref[w], acc_ref[w] = weigh(nb - 1 + w, own[w], m[w])
    for t in range(nb - 1):
        is_b, w, j = past_pair(t)
        lt, at = weigh(t, j, jnp.where(is_b, m[1], m[0]))
        l_ref[w] += lt
        acc_ref[w] += at
    oa_ref[...] = (acc_ref[0] / l_ref[0]).T
    ob_ref[...] = (acc_ref[1] / l_ref[1]).T


def moba(proj, batch, seq):
    t = proj.shape[0]
    nb = seq // MOBA_BLOCK
    nbp = max(nb, V7X_SUBLANES)
    hd = HEAD_DIM
    assert nb % 2 == 0
    nh = nb // 2
    out = jax.ShapeDtypeStruct((t // 2, MOBA_HEADS * hd), F32)
    out_spec = pl.BlockSpec((MOBA_BLOCK, hd), lambda b, h, i: (b * nh + i, h))
    return pl.pallas_call(
        functools.partial(_moba_kernel, nb=nb, nbp=nbp),
        out_shape=(out, out),
        grid=(batch, MOBA_HEADS, nh),
        in_specs=[
            pl.BlockSpec((MOBA_BLOCK, hd), lambda b, h, i: (b * nb + i, h)),
            pl.BlockSpec((MOBA_BLOCK, hd), lambda b, h, i: (b * nb + nb - 1 - i, h)),
            pl.BlockSpec((seq, hd), lambda b, h, i: (b, MOBA_HEADS + h)),
            pl.BlockSpec((seq, hd), lambda b, h, i: (b, 2 * MOBA_HEADS + h)),
        ],
        out_specs=(out_spec, out_spec),
        scratch_shapes=[
            pltpu.VMEM((nb, MOBA_BLOCK, hd), MXU_DTYPE),
            pltpu.VMEM((nb, hd, MOBA_BLOCK), MXU_DTYPE),
            pltpu.VMEM((nbp, hd), F32),
            pltpu.VMEM((2, MOBA_BLOCK, hd), MXU_DTYPE),
            pltpu.VMEM((2, nbp, MOBA_BLOCK), F32),
            pltpu.VMEM((nb + 1, MOBA_BLOCK, MOBA_BLOCK), F32),
            pltpu.VMEM((2, hd, MOBA_BLOCK), F32),
            pltpu.VMEM((2, 1, MOBA_BLOCK), F32),
        ],
        compiler_params=_params(3),
        name="moba",
    )(proj, proj, proj, proj)


def _retention_kernel(q_ref, k_ref, v_ref, g_ref, dm_ref, xi_ref, ze_ref, gc_ref, o_ref, st_ref):
    @pl.when(pl.program_id(0) == 0)
    def _():
        st_ref[...] = jnp.zeros_like(st_ref)

    for b in range(q_ref.shape[0]):
        for h in range(RET_HEADS):
            sl = slice(h * HEAD_DIM, (h + 1) * HEAD_DIM)
            q = _mxu(q_ref[b, :, sl])
            k = k_ref[b, :, sl]
            v = _mxu(v_ref[b, :, sl])
            st = st_ref[b, h]
            inner = lax.dot_general(q, _mxu(k), NT_DIMS, preferred_element_type=F32) * dm_ref[h]
            o = (jnp.dot(_mxu(inner), v, preferred_element_type=F32)
                 + jnp.dot(q, _mxu(st), preferred_element_type=F32) * xi_ref[h])
            st_ref[b, h] = (st * gc_ref[h]
                            + lax.dot_general(_mxu(k * ze_ref[h]), v, TN_DIMS, preferred_element_type=F32))
            o = o * lax.rsqrt(jnp.mean(o * o, axis=-1, keepdims=True) + EPS)
            gate = g_ref[b, :, sl]
            o_ref[b, :, sl] = o * (gate * jax.nn.sigmoid(gate))


def retention(proj, tables, batch, seq):
    t, n = proj.shape
    nc = seq // RET_CHUNK
    w = RET_HEADS * HEAD_DIM
    base = 3 * MOBA_HEADS * HEAD_DIM // w
    tab_spec = pl.BlockSpec((RET_HEADS, RET_CHUNK, HEAD_DIM), lambda c: (0, 0, 0))
    proj3 = proj.reshape(batch, seq, n)

    def col_spec(block):
        return pl.BlockSpec((batch, RET_CHUNK, w), lambda c: (0, c, block))

    out = pl.pallas_call(
        _retention_kernel,
        out_shape=jax.ShapeDtypeStruct((batch, seq, w), F32),
        grid=(nc,),
        in_specs=[col_spec(base), col_spec(base + 1), col_spec(base + 2), col_spec(base + 3),
                  tab_spec, tab_spec, tab_spec, tab_spec],
        out_specs=col_spec(0),
        scratch_shapes=[pltpu.VMEM((batch, RET_HEADS, HEAD_DIM, HEAD_DIM), F32)],
        compiler_params=_params(1),
        name="retention",
    )(proj3, proj3, proj3, proj3, *tables)
    return out.reshape(t, w)


def _retention_tables():
    c = RET_CHUNK
    log_g = jnp.log(1.0 - 2.0 ** (-5.0 - jnp.arange(RET_HEADS, dtype=F32)))
    idx = jnp.arange(c, dtype=F32)
    diff = idx[:, None] - idx[None, :]
    dmask = jnp.where(diff >= 0, jnp.exp(log_g[:, None, None] * jnp.maximum(diff, 0.0)), 0.0)
    xi = jnp.exp(log_g[:, None] * (idx + 1.0))[..., None]
    zeta = jnp.exp(log_g[:, None] * (c - 1.0 - idx))[..., None]
    gc = jnp.exp(log_g * c)[:, None, None]
    full = (RET_HEADS, c, HEAD_DIM)
    return (dmask, jnp.broadcast_to(xi, full), jnp.broadcast_to(zeta, full), jnp.broadcast_to(gc, full))


def _even_out_kernel(h_ref, lo_ref, hi_ref, r_ref, w_ref, o_ref, *, nb):
    in_lo = (pl.program_id(0) % nb) < nb // 2
    mo = jnp.where(in_lo, lo_ref[...], hi_ref[...])
    y = jnp.concatenate([_mxu(mo), _mxu(r_ref[...])], axis=1)
    o_ref[...] = h_ref[...] + jnp.dot(y, w_ref[...], preferred_element_type=F32)


def even_out(h, mo_lo, mo_hi, ro, w, layer, seq):
    t, d = h.shape
    tm = MOBA_BLOCK
    nb = seq // tm
    nh = nb // 2

    def lo_map(g):
        return ((g // nb) * nh + jnp.minimum(g % nb, nh - 1), 0)

    def hi_map(g):
        return ((g // nb) * nh + (nb - 1 - jnp.maximum(g % nb, nh)), 0)

    return pl.pallas_call(
        functools.partial(_even_out_kernel, nb=nb),
        out_shape=jax.ShapeDtypeStruct((t, d), F32),
        grid=(t // tm,),
        in_specs=[
            pl.BlockSpec((tm, d), lambda g: (g, 0)),
            pl.BlockSpec((tm, mo_lo.shape[1]), lo_map),
            pl.BlockSpec((tm, mo_hi.shape[1]), hi_map),
            pl.BlockSpec((tm, ro.shape[1]), lambda g: (g, 0)),
            _layer_spec(w, layer),
        ],
        out_specs=pl.BlockSpec((tm, d), lambda g: (g, 0)),
        compiler_params=_params(1),
        name="even_out",
    )(h, mo_lo, mo_hi, ro, w)


def _conv_out_kernel(h_ref, bg_ref, cg_ref, hx_ref, cgp_ref, hxp_ref, cw_ref, w_ref, o_ref, *, per_seq):
    i = pl.program_id(0)
    u = cg_ref[...] * hx_ref[...]
    up = jnp.where(i % per_seq == 0, 0.0, cgp_ref[...] * hxp_ref[...])
    row = lax.broadcasted_iota(jnp.int32, u.shape, 0)
    p1 = up[V7X_SUBLANES - 1:V7X_SUBLANES, :]
    p2 = up[V7X_SUBLANES - 2:V7X_SUBLANES - 1, :]
    u1 = jnp.where(row == 0, p1, pltpu.roll(u, 1, axis=0))
    u2 = jnp.where(row == 0, p2, jnp.where(row == 1, p1, pltpu.roll(u, 2, axis=0)))
    y = cw_ref[0:1, :] * u2 + cw_ref[1:2, :] * u1 + cw_ref[2:3, :] * u
    z = _mxu(bg_ref[...] * y)
    o_ref[...] = h_ref[...] + jnp.dot(z, w_ref[...], preferred_element_type=F32)


def conv_out(h, proj, conv_w, w, layer, seq):
    t, d = h.shape
    tm = CONV_TM
    per_seq = seq // tm
    halo = tm // V7X_SUBLANES
    return pl.pallas_call(
        functools.partial(_conv_out_kernel, per_seq=per_seq),
        out_shape=jax.ShapeDtypeStruct((t, d), F32),
        grid=(t // tm,),
        in_specs=[
            pl.BlockSpec((tm, d), lambda i: (i, 0)),
            pl.BlockSpec((tm, d), lambda i: (i, 0)),
            pl.BlockSpec((tm, d), lambda i: (i, 1)),
            pl.BlockSpec((tm, d), lambda i: (i, 2)),
            pl.BlockSpec((V7X_SUBLANES, d), lambda i: (jnp.maximum(i * halo - 1, 0), 1)),
            pl.BlockSpec((V7X_SUBLANES, d), lambda i: (jnp.maximum(i * halo - 1, 0), 2)),
            pl.BlockSpec(conv_w.shape, lambda i: (0, 0)),
            _layer_spec(w, layer),
        ],
        out_specs=pl.BlockSpec((tm, d), lambda i: (i, 0)),
        compiler_params=_params(1),
        name="conv_out",
    )(h, proj, proj, proj, proj, proj, conv_w, w)


def _peer_scores_kernel(h_ref, g_ref, wq_ref, keys_ref, xn_ref, sc_ref):
    xn = _mxu(_rms(h_ref[...], g_ref[...]))
    xn_ref[...] = xn
    n_hp = keys_ref.shape[0]
    per = PROJ_COLS // PEER_HALF
    for c in range(n_hp // per):
        q = jnp.dot(xn, wq_ref[:, c * PROJ_COLS:(c + 1) * PROJ_COLS], preferred_element_type=F32)
        for u in range(per):
            hp = c * per + u
            qs = _mxu(q[:, u * PEER_HALF:(u + 1) * PEER_HALF])
            sc = lax.dot_general(keys_ref[hp], qs, NT_DIMS, preferred_element_type=F32)
            for grp in range(sc_ref.shape[1]):
                sc_ref[hp, grp] = sc[:, grp * V7X_LANES:(grp + 1) * V7X_LANES]


def peer_scores(h, g, wq, layer, keys):
    t, d = h.shape
    tm = PEER_Q_TM
    n_hp = keys.shape[0]
    grp = tm // V7X_LANES
    return pl.pallas_call(
        _peer_scores_kernel,
        out_shape=(jax.ShapeDtypeStruct((t, d), MXU_DTYPE),
                   jax.ShapeDtypeStruct((n_hp, t // V7X_LANES, PEER_NKEYS, V7X_LANES), F32)),
        grid=(t // tm,),
        in_specs=[
            pl.BlockSpec((tm, d), lambda i: (i, 0)),
            pl.BlockSpec((1, d), lambda i: (0, 0)),
            _layer_spec(wq, layer),
            pl.BlockSpec(keys.shape, lambda i: (0, 0, 0)),
        ],
        out_specs=(pl.BlockSpec((tm, d), lambda i: (i, 0)),
                   pl.BlockSpec((n_hp, grp, PEER_NKEYS, V7X_LANES), lambda i: (0, i, 0, 0))),
        compiler_params=_params(1),
        name="peer_scores",
    )(h, g.reshape(1, d), wq, keys)


def _extract_max(x, pos):
    m = jnp.max(x, axis=-2, keepdims=True)
    sel = jnp.min(jnp.where(x == m, pos, float(2**30)), axis=-2, keepdims=True)
    return m, sel, jnp.where(pos == sel, NEG_INF, x)


def _topk_rows(x, pos, k):
    vals, ids = [], []
    for _ in range(k):
        m, sel, x = _extract_max(x, pos)
        vals.append(m)
        ids.append(sel)
    return jnp.concatenate(vals, axis=-2), jnp.concatenate(ids, axis=-2)


def _lookup_rows(table, idx):
    out = jnp.zeros(idx.shape, table.dtype)
    for a in range(table.shape[0]):
        out = jnp.where(idx == a, table[a:a + 1, :], out)
    return out


def _key_positions():
    return lax.broadcasted_iota(jnp.int32, (PEER_NKEYS, V7X_LANES), 0).astype(F32)


def _pair_candidates(v1, v2):
    k = PEER_TOPK
    n = V7X_LANES
    r16 = lax.broadcasted_iota(jnp.int32, (k, n), 0)
    r8 = lax.broadcasted_iota(jnp.int32, (V7X_SUBLANES, n), 0)
    cand, pos = [v1[0:1, :] + v2], [r16]
    for a in range(1, V7X_SUBLANES):
        piece = v1[a:a + 1, :] + v2[0:V7X_SUBLANES, :]
        cand.append(jnp.where(r8 < k // (a + 1), piece, NEG_INF))
        pos.append(a * k + r8)
    cand.append(v1[V7X_SUBLANES:, :] + v2[0:1, :])
    pos.append((V7X_SUBLANES + r8) * k)
    return jnp.concatenate(cand, axis=0), jnp.concatenate(pos, axis=0).astype(F32)


def _emit_lists(top_s, top_pos, n1, n2, h, grp, l1_ref, l2_ref, lg_ref):
    k = PEER_TOPK
    top_pos = top_pos.astype(jnp.int32)
    e1 = _lookup_rows(n1, lax.shift_right_logical(top_pos, k.bit_length() - 1))
    e2 = _lookup_rows(n2, top_pos & (k - 1))
    ex = jnp.exp(top_s - top_s[0:1, :])
    gate = ex / jnp.sum(ex, axis=0, keepdims=True)
    rows = pl.ds(pl.multiple_of(h * k, k), k)
    l1_ref[grp, rows, :] = e1.astype(jnp.int32)
    l2_ref[grp, rows, :] = e2.astype(jnp.int32)
    lg_ref[grp, rows, :] = gate


def _transpose_lists(src_refs, dst_refs):
    for src, dst in zip(src_refs, dst_refs):
        for grp in range(src.shape[0]):
            dst[grp * V7X_LANES:(grp + 1) * V7X_LANES, :] = src[grp].T


def _peer_topk_kernel(sc_ref, i1_ref, i2_ref, g_ref, l1_ref, l2_ref, lg_ref):
    n_heads = sc_ref.shape[0] // 2
    n_units = n_heads * sc_ref.shape[1]

    def unit_pair(n, carry):
        units = [(u % n_heads, u // n_heads) for u in (n, n + n_units // 2)]
        x = jnp.stack([sc_ref[2 * h + p, grp] for h, grp in units for p in range(2)])
        hv, hn = _topk_rows(x, _key_positions(), PEER_TOPK)
        cands = [_pair_candidates(hv[2 * k], hv[2 * k + 1]) for k in range(2)]
        top_s, top_pos = _topk_rows(jnp.stack([cd[0] for cd in cands]), jnp.stack([cd[1] for cd in cands]), PEER_TOPK)
        for k, (h, grp) in enumerate(units):
            _emit_lists(top_s[k], top_pos[k], hn[2 * k], hn[2 * k + 1], h, grp, l1_ref, l2_ref, lg_ref)
        return carry

    lax.fori_loop(0, n_units // 2, unit_pair, 0)
    _transpose_lists((l1_ref, l2_ref, lg_ref), (i1_ref, i2_ref, g_ref))


def _topk_scratch(n_groups, nj, slots=()):
    lists = (n_groups, nj, V7X_LANES)
    halves = (*slots, 2, PEER_TOPK, V7X_LANES)
    return [pltpu.VMEM(lists, jnp.int32), pltpu.VMEM(lists, jnp.int32), pltpu.VMEM(lists, F32),
            pltpu.VMEM(halves, F32), pltpu.VMEM(halves, F32)]


def peer_topk(sc, n_tokens):
    n_hp, _, nk, _ = sc.shape
    grp = n_tokens // V7X_LANES
    nj = (n_hp // 2) * PEER_TOPK
    out_spec = pl.BlockSpec((n_tokens, nj), lambda i: (0, 0))
    return pl.pallas_call(
        _peer_topk_kernel,
        out_shape=(jax.ShapeDtypeStruct((n_tokens, nj), jnp.int32), jax.ShapeDtypeStruct((n_tokens, nj), jnp.int32),
                   jax.ShapeDtypeStruct((n_tokens, nj), F32)),
        grid=(1,),
        in_specs=[pl.BlockSpec((n_hp, grp, nk, V7X_LANES), lambda i: (0, 0, 0, 0))],
        out_specs=(out_spec, out_spec, out_spec),
        scratch_shapes=_topk_scratch(grp, nj)[:3],
        compiler_params=_params(1),
        name="peer_topk",
    )(sc)


def _peer_mix_kernel(h_ref, xn_ref, f1_ref, f2_ref, fg_ref, sc_ref, ua_ref, ub_ref, va_ref, vb_ref, gain_ref, o_ref,
                     w_ref, ha_ref, hb_ref, acc_ref, i1_ref, i2_ref, g_ref, l1_ref, l2_ref, lg_ref, hv_ref, hn_ref,
                     *, n_chunks, units_per_step, normalise):
    tile = pl.program_id(0)
    c = pl.program_id(1)
    last = pl.num_programs(1) - 1
    tm = xn_ref.shape[0]
    nk = PEER_NKEYS
    half = nk // 2
    next_lists = (l1_ref, l2_ref, lg_ref)
    lists = (i1_ref, i2_ref, g_ref)

    @pl.when((c == 0) & (tile == 0))
    def _():
        for src, dst in zip((f1_ref, f2_ref, fg_ref), lists):
            dst[...] = src[...]
        hv_ref[...] = jnp.zeros_like(hv_ref)
        hn_ref[...] = jnp.zeros_like(hn_ref)

    @pl.when((c == 0) & (tile > 0))
    def _():
        _transpose_lists(next_lists, lists)

    @pl.when(c == 0)
    def _():
        acc_ref[...] = jnp.zeros_like(acc_ref)
        hb_ref[...] = jnp.zeros_like(hb_ref)
        r = lax.broadcasted_iota(jnp.int32, (nk, nk), 0).astype(F32).astype(jnp.bfloat16)
        one = jnp.ones((nk, nk), jnp.bfloat16)
        zero = jnp.zeros((nk, nk), jnp.bfloat16)

        def one_hots(t):
            i1 = i1_ref[pl.ds(t, 1), :].astype(F32).astype(jnp.bfloat16)
            i2 = i2_ref[pl.ds(t, 1), :].astype(F32).astype(jnp.bfloat16)
            g = (0.5 * g_ref[pl.ds(t, 1), :]).astype(jnp.bfloat16)
            return jnp.where(r == i1, g, zero), jnp.where(r == i2, one, zero)

        def tok_pair(n, carry):
            (a0, b0), (a1, b1) = one_hots(2 * n), one_hots(2 * n + 1)
            w = lax.dot_general(jnp.concatenate([a0, a1], axis=0), jnp.concatenate([b0, b1], axis=0), NT_DIMS,
                                preferred_element_type=F32)
            for k in range(2):
                wk = w[k * nk:(k + 1) * nk, k * nk:(k + 1) * nk].astype(jnp.bfloat16)
                dst = pl.ds(pl.multiple_of((2 * n + k) * W_PITCH, V7X_SUBLANES), half)
                w_ref[dst, :] = pltpu.bitcast(wk, jnp.uint32)
            return carry

        lax.fori_loop(0, tm // 2, tok_pair, 0, unroll=SCATTER_UNROLL // 2)

    def hidden(u_ref):
        return jnp.dot(xn_ref[...], u_ref[...], preferred_element_type=F32)

    def weighted(hid_ref, chunk, valid):
        pairs = PEER_ECHUNK // (2 * nk)
        first = jnp.clip(chunk, 0, n_chunks - 1) * pairs
        hid = hid_ref[...]
        act = hid * (1.0 + lax.erf(hid * (2.0 ** -0.5)))
        cols = []
        for q in range(pairs):
            packed = w_ref[pl.ds(first + q, tm, stride=W_PITCH), :]
            if valid is not None:
                packed = jnp.where(valid, packed, jnp.uint32(0))
            w_even = lax.bitcast_convert_type(packed << 16, F32)
            w_odd = lax.bitcast_convert_type(packed & jnp.uint32(0xFFFF0000), F32)
            cols += [w_even * act[:, 2 * q * nk:(2 * q + 1) * nk], w_odd * act[:, (2 * q + 1) * nk:(2 * q + 2) * nk]]
        return _mxu(jnp.concatenate(cols, axis=1))

    def accumulate(p, v_ref):
        acc_ref[...] += jnp.dot(p, v_ref[...], preferred_element_type=F32)

    n_heads = sc_ref.shape[0] // 2
    n_units = n_heads * sc_ref.shape[1]
    xs, cands, cand_poss, prev = [], [], [], []
    for s in range(units_per_step):
        n = jnp.minimum(c * units_per_step + s, n_units - 1)
        h, grp = n % n_heads, n // n_heads
        xs += [sc_ref[2 * h, grp], sc_ref[2 * h + 1, grp]]
        cand, cand_pos = _pair_candidates(hv_ref[s, 0], hv_ref[s, 1])
        cands.append(cand)
        cand_poss.append(cand_pos)
        n = jnp.clip((c - 1) * units_per_step + s, 0, n_units - 1)
        prev.append((n % n_heads, n // n_heads, hn_ref[s, 0], hn_ref[s, 1]))
    state = [[jnp.stack(xs), _key_positions(), [], []], [jnp.stack(cands), jnp.stack(cand_poss), [], []]]

    def select(n_steps):
        for _ in range(n_steps):
            for st in state:
                m, sel, st[0] = _extract_max(st[0], st[1])
                st[2].append(m)
                st[3].append(sel)

    assert sum(SELECT_SPLIT) == PEER_TOPK
    select(SELECT_SPLIT[0])
    ha_ref[...] = hidden(ua_ref)
    select(SELECT_SPLIT[1])
    p_prev = weighted(hb_ref, 2 * c - 1, None)
    select(SELECT_SPLIT[2])
    accumulate(p_prev, va_ref)
    select(SELECT_SPLIT[3])
    hb_ref[...] = hidden(ub_ref)
    select(SELECT_SPLIT[4])
    p_cur = weighted(ha_ref, 2 * c, c < last)
    select(SELECT_SPLIT[5])
    accumulate(p_cur, vb_ref)
    select(SELECT_SPLIT[6])

    (half_v, half_n), (pair_v, pair_pos) = [
        (jnp.concatenate(st[2], axis=-2), jnp.concatenate(st[3], axis=-2)) for st in state]
    for s, (h, grp, n1, n2) in enumerate(prev):
        hv_ref[s] = half_v[2 * s:2 * s + 2]
        hn_ref[s] = half_n[2 * s:2 * s + 2]
        _emit_lists(pair_v[s], pair_pos[s], n1, n2, h, grp, *next_lists)

    @pl.when(c == last)
    def _():
        out = h_ref[...] + acc_ref[...]
        o_ref[...] = _rms(out, gain_ref[...]) if normalise else out


def peer_mix(h, xn, sc, first_lists, u, v, layer, gain, normalise):
    t, d = h.shape
    tm = first_lists[0].shape[0]
    nj = first_lists[0].shape[1]
    n_tiles = t // tm
    grp = tm // V7X_LANES
    n_hp = sc.shape[0]
    n_chunks = v.shape[1] // PEER_ECHUNK
    steps = n_chunks // 2 + 1
    n_units = (n_hp // 2) * grp
    units_per_step = -(-n_units // (steps - 1))
    tok_spec = pl.BlockSpec((tm, d), lambda i, c: (i, 0))
    res_spec = pl.BlockSpec((tm, d), lambda i, c: (jnp.where(c >= 1, i, jnp.maximum(i - 1, 0)), 0))
    first_spec = pl.BlockSpec((tm, nj), lambda i, c: (0, 0))
    sc_spec = pl.BlockSpec((n_hp, grp, PEER_NKEYS, V7X_LANES), lambda i, c: (0, jnp.minimum(i + 1, n_tiles - 1), 0, 0))

    def chunk(c, offset):
        return jnp.clip(2 * c + offset, 0, n_chunks - 1)

    def u_spec(offset):
        return pl.BlockSpec((None, None, d, PEER_ECHUNK), lambda i, c: (layer, chunk(c, offset), 0, 0))

    def v_spec(offset):
        return pl.BlockSpec((None, PEER_ECHUNK, d), lambda i, c: (layer, chunk(c, offset), 0))

    return pl.pallas_call(
        functools.partial(_peer_mix_kernel, n_chunks=n_chunks, units_per_step=units_per_step, normalise=normalise),
        out_shape=jax.ShapeDtypeStruct((t, d), F32),
        grid=(n_tiles, steps),
        in_specs=[res_spec, tok_spec, first_spec, first_spec, first_spec, sc_spec, u_spec(0), u_spec(1),
                  v_spec(-1), v_spec(0), pl.BlockSpec((1, d), lambda i, c: (0, 0))],
        out_specs=tok_spec,
        scratch_shapes=[
            pltpu.VMEM((tm * W_PITCH, PEER_NKEYS), jnp.uint32),
            pltpu.VMEM((tm, PEER_ECHUNK), F32),
            pltpu.VMEM((tm, PEER_ECHUNK), F32),
            pltpu.VMEM((tm, d), F32),
            pltpu.VMEM((tm, nj), jnp.int32),
            pltpu.VMEM((tm, nj), jnp.int32),
            pltpu.VMEM((tm, nj), F32),
            *_topk_scratch(grp, nj, slots=(units_per_step,)),
        ],
        compiler_params=_params(2),
        name="peer_mix",
    )(h, xn, *first_lists, sc, u, u, v, v, gain.reshape(1, d))


def _rope_tables(seq):
    half = HEAD_DIM // 2
    inv = ROPE_THETA ** (-jnp.arange(half, dtype=F32) / half)
    ang = jnp.arange(seq, dtype=F32)[:, None] * inv[None, :]
    cos, sin = jnp.cos(ang), jnp.sin(ang)
    return jnp.concatenate([cos, cos], axis=1), jnp.concatenate([-sin, sin], axis=1)


def kernel(x, norm_mix, norm_ffn, even_w_in, even_w_out, odd_w_in, odd_conv, odd_w_out, peer_w_q, peer_sub_keys,
           peer_u, peer_v, final_norm):
    batch, seq, d = x.shape
    depth = norm_mix.shape[0]
    cos_t, sin_t = _rope_tables(seq)
    ret_tables = _retention_tables()
    mh, rh = MOBA_HEADS, RET_HEADS
    rope_blocks = tuple(range(0, 2 * mh)) + tuple(range(3 * mh, 3 * mh + 2 * rh))
    ret_k_blocks = tuple(range(3 * mh + rh, 3 * mh + 2 * rh))

    even_in, even_out_w = cast_weights(even_w_in), cast_weights(even_w_out)
    odd_in, odd_out_w = cast_weights(odd_w_in), cast_weights(odd_w_out)
    w_q, u_t, v = cast_weights(peer_w_q), cast_weights(peer_u, slab=PEER_ECHUNK), cast_weights(peer_v)
    keys = _mxu(peer_sub_keys.reshape(depth, 2 * PEER_HEADS, PEER_NKEYS, PEER_HALF))

    h = x.reshape(batch * seq, d)
    for layer in range(depth):
        i = layer // 2
        if layer % 2 == 0:
            proj = norm_proj(h, norm_mix[layer], even_in, i, cos_t, sin_t, seq, rope_blocks, ret_k_blocks,
                             HEAD_DIM ** -0.5)
            mo_lo, mo_hi = moba(proj, batch, seq)
            ro = retention(proj, ret_tables, batch, seq)
            h = even_out(h, mo_lo, mo_hi, ro, even_out_w, i, seq)
        else:
            proj = norm_proj(h, norm_mix[layer], odd_in, i, cos_t, sin_t, seq)
            h = conv_out(h, proj, odd_conv[i], odd_out_w, i, seq)
        xn, sc = peer_scores(h, norm_ffn[layer], w_q, layer, keys[layer])
        first_lists = peer_topk(sc, min(PEER_TM, h.shape[0]))
        h = peer_mix(h, xn, sc, first_lists, u_t, v, layer, final_norm, layer == depth - 1)
    return h.reshape(batch, seq, d)
```
